```python
import jax, jax.numpy as jnp
from jax import lax
import numpy as np

D_MODEL = 1024
BATCH = 8
SEQ = 2048
DEPTH = 1
DEC_BATCH = 128
DEC_SEQ = 8
PAST_LEN = 16384
PAGE_SIZE = 128

CHUNK = 128
GM_WIDTH = D_MODEL
GM_GROUPS = 8
GM_GROUP_DIM = GM_WIDTH // GM_GROUPS
RET_HEADS = 8
RET_DK = D_MODEL // RET_HEADS
RET_DV = 2 * RET_DK
RET_QK = RET_HEADS * RET_DK
RET_V = RET_HEADS * RET_DV
D_FF = 2816
N_MOD = 9
EPS = 1e-6
ROPE_BASE = 10000.0
IN_SIZES = (GM_WIDTH, GM_WIDTH, RET_QK, RET_QK, RET_V, RET_V, D_MODEL, D_MODEL)
IN_COLS = sum(IN_SIZES)
SPLIT_POINTS = tuple(int(s) for s in np.cumsum(IN_SIZES)[:-1])

kernel_name = "hybrid_gmlp_retention_macaron_adaln_step"


def rms_norm(x, gain):
    xf = x.astype(jnp.float32)
    y = xf * lax.rsqrt(jnp.mean(xf * xf, axis=-1, keepdims=True) + EPS)
    return (y * gain.astype(jnp.float32)).astype(x.dtype)


def layer_norm(x, gain, bias):
    xf = x.astype(jnp.float32)
    mu = jnp.mean(xf, axis=-1, keepdims=True)
    var = jnp.mean(jnp.square(xf - mu), axis=-1, keepdims=True)
    y = (xf - mu) * lax.rsqrt(var + EPS)
    return (y * gain.astype(jnp.float32) + bias.astype(jnp.float32)).astype(x.dtype)


def head_rms(o):
    return o * lax.rsqrt(jnp.mean(o * o, axis=-1, keepdims=True) + EPS)


def modulate(n, shift, scale):
    return n * (1 + scale) + shift


def swiglu(h, w_gate, w_up, w_down):
    return (jax.nn.silu(h @ w_gate) * (h @ w_up)) @ w_down


def rotary(x, pos):
    half = x.shape[-1] // 2
    inv = ROPE_BASE ** (-jnp.arange(half, dtype=jnp.float32) / half)
    ang = pos.astype(jnp.float32)[:, None] * inv[None, :]
    cos = jnp.cos(ang)[None, :, None, :]
    sin = jnp.sin(ang)[None, :, None, :]
    xf = x.astype(jnp.float32)
    x1, x2 = xf[..., :half], xf[..., half:]
    return jnp.concatenate([x1 * cos - x2 * sin, x1 * sin + x2 * cos], axis=-1)


def retention_log_gamma():
    return jnp.log1p(-jnp.power(2.0, -5.0 - jnp.arange(RET_HEADS, dtype=jnp.float32)))


def retention_chunk(q, k, v, S, log_gamma):
    L = q.shape[1]
    idx = jnp.arange(L, dtype=jnp.float32)
    diff = idx[:, None] - idx[None, :]
    causal = diff >= 0
    decay = jnp.where(causal[None], jnp.exp(jnp.maximum(diff, 0.0)[None] * log_gamma[:, None, None]), 0.0)
    scores = jnp.einsum('blhd,bshd->bhls', q, k) * decay[None]
    intra = jnp.einsum('bhls,bshe->blhe', scores, v)
    q_decay = jnp.exp((idx + 1.0)[:, None] * log_gamma[None, :])
    inter = jnp.einsum('blhd,bhde->blhe', q, S) * q_decay[None, :, :, None]
    k_decay = jnp.exp((L - 1.0 - idx)[:, None] * log_gamma[None, :])
    S_new = jnp.exp(L * log_gamma)[None, :, None, None] * S + jnp.einsum(
        'blhd,blhe->bhde', k * k_decay[None, :, :, None], v)
    return intra + inter, S_new


def retention(q, k, v, S0):
    B, L = q.shape[0], q.shape[1]
    lc = min(L, CHUNK)
    nc = L // lc
    log_gamma = retention_log_gamma()

    def to_blocks(a):
        return jnp.moveaxis(a.reshape(B, nc, lc, *a.shape[2:]), 1, 0)

    def step(S, qkv):
        o, S = retention_chunk(qkv[0], qkv[1], qkv[2], S, log_gamma)
        return S, o

    S_new, o = lax.scan(step, S0, (to_blocks(q), to_blocks(k), to_blocks(v)))
    o = jnp.moveaxis(o, 0, 1).reshape(B, L, RET_HEADS, RET_DV)
    return o, S_new


def chunk_spatial_mix(v, gm_ws, gm_bs):
    B, L, W = v.shape
    lc = min(L, CHUNK)
    nc = L // lc
    w = gm_ws[:, :lc, :lc] * jnp.tril(jnp.ones((lc, lc), gm_ws.dtype))
    vg = v.reshape(B, nc, lc, GM_GROUPS, GM_GROUP_DIM)
    mix = jnp.einsum('gts,bnsgd->bntgd', w, vg) + jnp.transpose(gm_bs[:, :lc])[None, None, :, :, None]
    return mix.reshape(B, L, W)


def token_mix(h, pos0, S0, w_in, gm_ln_g, gm_ln_b, gm_ws, gm_bs, w_a, w_b, w_o):
    B, L, _ = h.shape
    z = h @ w_in
    u, v, q, k, rv, rg, ga, gb = jnp.split(z, SPLIT_POINTS, axis=-1)
    u = jax.nn.gelu(u)
    v = layer_norm(jax.nn.gelu(v), gm_ln_g, gm_ln_b)
    o_a = u * chunk_spatial_mix(v, gm_ws, gm_bs)
    pos = pos0 + jnp.arange(L)
    qh = rotary(q.reshape(B, L, RET_HEADS, RET_DK), pos)
    kh = rotary(k.reshape(B, L, RET_HEADS, RET_DK), pos) * (RET_DK ** -0.5)
    vh = rv.reshape(B, L, RET_HEADS, RET_DV).astype(jnp.float32)
    o_r, S_new = retention(qh, kh, vh, S0)
    o_b = head_rms(o_r).reshape(B, L, RET_V).astype(h.dtype) * jax.nn.silu(rg)
    merged = jax.nn.sigmoid(ga) * (o_a @ w_a) + jax.nn.sigmoid(gb) * (o_b @ w_b)
    return merged @ w_o, S_new, v


def decoder_layer(x, c, pos0, S0, w_ada, b_ada, n1_g, w1_gate, w1_up, w1_down,
                  nm_g, w_in, gm_ln_g, gm_ln_b, gm_ws, gm_bs, w_a, w_b, w_o,
                  n2_g, w2_gate, w2_up, w2_down):
    m = jax.nn.silu(c) @ w_ada + b_ada
    sh1, sc1, g1, sh2, sc2, g2, sh3, sc3, g3 = jnp.split(m[:, None, :], N_MOD, axis=-1)
    h = modulate(rms_norm(x, n1_g), sh1, sc1)
    x = x + 0.5 * g1 * swiglu(h, w1_gate, w1_up, w1_down)
    h = modulate(rms_norm(x, nm_g), sh2, sc2)
    mix, S_new, v_rows = token_mix(h, pos0, S0, w_in, gm_ln_g, gm_ln_b, gm_ws, gm_bs, w_a, w_b, w_o)
    x = x + g2 * mix
    h = modulate(rms_norm(x, n2_g), sh3, sc3)
    x = x + 0.5 * g3 * swiglu(h, w2_gate, w2_up, w2_down)
    return x, S_new, v_rows


def setup_inputs(seed: int = 0) -> dict:
    key = jax.random.key(seed)
    ks = jax.random.split(key, 32)
    f32 = jnp.float32

    def nrm(k, shape, scale):
        return jax.random.normal(k, shape, f32) * scale

    def gain(k, shape):
        return 1.0 + 0.02 * jax.random.normal(k, shape, f32)

    return {
        "x_prompt": nrm(ks[0], (BATCH, SEQ, D_MODEL), 1.0),
        "x_sample": nrm(ks[1], (DEC_BATCH, DEC_SEQ, D_MODEL), 1.0),
        "state_ret": nrm(ks[2], (DEPTH, DEC_BATCH, RET_HEADS, RET_DK, RET_DV), 0.5),
        "c_prompt": nrm(ks[3], (BATCH, D_MODEL), 1.0),
        "c_sample": nrm(ks[4], (DEC_BATCH, D_MODEL), 1.0),
        "w_ada": nrm(ks[5], (DEPTH, D_MODEL, N_MOD * D_MODEL), 0.5 * D_MODEL ** -0.5),
        "b_ada": nrm(ks[6], (DEPTH, N_MOD * D_MODEL), 0.01),
        "n1_g": gain(ks[7], (DEPTH, D_MODEL)),
        "w1_gate": nrm(ks[8], (DEPTH, D_MODEL, D_FF), D_MODEL ** -0.5),
        "w1_up": nrm(ks[9], (DEPTH, D_MODEL, D_FF), D_MODEL ** -0.5),
        "w1_down": nrm(ks[10], (DEPTH, D_FF, D_MODEL), D_FF ** -0.5),
        "nm_g": gain(ks[11], (DEPTH, D_MODEL)),
        "w_in": nrm(ks[12], (DEPTH, D_MODEL, IN_COLS), D_MODEL ** -0.5),
        "gm_ln_g": gain(ks[13], (DEPTH, GM_WIDTH)),
        "gm_ln_b": nrm(ks[14], (DEPTH, GM_WIDTH), 0.01),
        "gm_ws": nrm(ks[15], (DEPTH, GM_GROUPS, CHUNK, CHUNK), CHUNK ** -0.5),
        "gm_bs": gain(ks[16], (DEPTH, GM_GROUPS, CHUNK)),
        "w_a": nrm(ks[17], (DEPTH, GM_WIDTH, D_MODEL), GM_WIDTH ** -0.5),
        "w_b": nrm(ks[18], (DEPTH, RET_V, D_MODEL), RET_V ** -0.5),
        "w_o": nrm(ks[19], (DEPTH, D_MODEL, D_MODEL), D_MODEL ** -0.5),
        "n2_g": gain(ks[20], (DEPTH, D_MODEL)),
        "w2_gate": nrm(ks[21], (DEPTH, D_MODEL, D_FF), D_MODEL ** -0.5),
        "w2_up": nrm(ks[22], (DEPTH, D_MODEL, D_FF), D_MODEL ** -0.5),
        "w2_down": nrm(ks[23], (DEPTH, D_FF, D_MODEL), D_FF ** -0.5),
        "final_g": gain(ks[24], (D_MODEL,)),
    }


def reference(x_prompt, x_sample, state_ret, c_prompt, c_sample, w_ada, b_ada, n1_g,
              w1_gate, w1_up, w1_down, nm_g, w_in, gm_ln_g, gm_ln_b, gm_ws, gm_bs,
              w_a, w_b, w_o, n2_g, w2_gate, w2_up, w2_down, final_g):
    yp, ys = x_prompt, x_sample
    sp_list, ss_list, vs_list = [], [], []
    for l in range(DEPTH):
        lw = (w_ada[l], b_ada[l], n1_g[l], w1_gate[l], w1_up[l], w1_down[l],
              nm_g[l], w_in[l], gm_ln_g[l], gm_ln_b[l], gm_ws[l], gm_bs[l],
              w_a[l], w_b[l], w_o[l], n2_g[l], w2_gate[l], w2_up[l], w2_down[l])
        S0_prompt = jnp.zeros((x_prompt.shape[0], RET_HEADS, RET_DK, RET_DV), jnp.float32)
        yp, sp, _ = decoder_layer(yp, c_prompt, 0, S0_prompt, *lw)
        ys, ss, vs = decoder_layer(ys, c_sample, PAST_LEN, state_ret[l].astype(jnp.float32), *lw)
        sp_list.append(sp)
        ss_list.append(ss)
        vs_list.append(vs)
    y_prompt = rms_norm(yp, final_g)
    y_sample = rms_norm(ys, final_g)
    ret_state_prompt = jnp.stack(sp_list, axis=0)
    ret_state_sample = jnp.stack(ss_list, axis=0)
    gmlp_v_sample = jnp.stack(vs_list, axis=0)
    return (y_prompt, y_sample, ret_state_prompt, ret_state_sample, gmlp_v_sample)
```

```python
import functools

import jax
import jax.numpy as jnp
import numpy as np
from jax import lax
from jax.experimental import pallas as pl
from jax.experimental.pallas import tpu as pltpu

F32 = jnp.float32
BF16 = jnp.bfloat16

D_MODEL = 1024
D_FF = 2816
N_MOD = 9
EPS = 1e-6
ROPE_BASE = 10000.0
PAST_LEN = 16384
CHUNK = 128
GROUPS = 8
GROUP_DIM = D_MODEL // GROUPS
HEADS = 8
DK = D_MODEL // HEADS
DV = 2 * DK
RET_V = HEADS * DV
OFF_U, OFF_V, OFF_Q, OFF_K, OFF_RV, OFF_RG, OFF_GA, OFF_GB, OFF_END = (
    0, 1024, 2048, 3072, 4096, 6144, 8192, 9216, 10240)

VMEM_LIMIT_V7X = 56 * 1024 * 1024


def _dot(a, b):
    return jnp.dot(a, b, preferred_element_type=F32)


def _silu(x):
    return x * jax.nn.sigmoid(x)


def _gelu_tanh(x):
    c = np.float32(np.sqrt(2.0 / np.pi))
    return 0.5 * x * (1.0 + jnp.tanh(c * (x + 0.044715 * (x * x * x))))


def _rms_norm(x, gain):
    return x * lax.rsqrt(jnp.mean(x * x, axis=-1, keepdims=True) + EPS) * gain


def _const_spec(shape):
    nd = len(shape)
    return pl.BlockSpec(shape, lambda *_: (0,) * nd, pipeline_mode=pl.Buffered(1))


def _ada_kernel(c_ref, w_ref, b_ref, o_ref):
    s = _silu(c_ref[...]).astype(BF16)
    o_ref[...] = _dot(s, w_ref[...]) + b_ref[...]


def _ada(c_all, w_ada, b_ada):
    rows = c_all.shape[0]
    n = w_ada.shape[1]
    bn = 3 * D_MODEL
    return pl.pallas_call(
        _ada_kernel,
        grid=(n // bn,),
        in_specs=[pl.BlockSpec((rows, D_MODEL), lambda j: (0, 0)),
                  pl.BlockSpec((D_MODEL, bn), lambda j: (0, j)),
                  pl.BlockSpec((1, bn), lambda j: (0, j))],
        out_specs=pl.BlockSpec((rows, bn), lambda j: (0, j)),
        out_shape=jax.ShapeDtypeStruct((rows, n), F32),
        name="ada",
    )(c_all, w_ada, b_ada.reshape(1, n))


def _ffn_kernel(x_ref, sh_ref, sc_ref, g_ref, ng_ref, wg_ref, wu_ref, wd_ref, fg_ref, o_ref, *, final):
    x = x_ref[...]
    bb, r, _ = x.shape
    h = _rms_norm(x, ng_ref[...]) * (1.0 + sc_ref[...]) + sh_ref[...]
    h = h.reshape(bb * r, D_MODEL).astype(BF16)
    act = (_silu(_dot(h, wg_ref[...])) * _dot(h, wu_ref[...])).astype(BF16)
    y = _dot(act, wd_ref[...]).reshape(bb, r, D_MODEL)
    out = x + (0.5 * g_ref[...]) * y
    if final:
        out = _rms_norm(out, fg_ref[...])
    o_ref[...] = out


def _ffn(x, sh, sc, g, norm_g, wg, wu, wd, final_g, *, block, final):
    nb, r, _ = x.shape
    bb, br = block
    xspec = pl.BlockSpec((bb, br, D_MODEL), lambda i, j: (i, j, 0))
    mspec = pl.BlockSpec((bb, 1, D_MODEL), lambda i, j: (i, 0, 0))
    return pl.pallas_call(
        functools.partial(_ffn_kernel, final=final),
        grid=(nb // bb, r // br),
        in_specs=[xspec, mspec, mspec, mspec,
                  _const_spec((1, D_MODEL)),
                  _const_spec((D_MODEL, D_FF)), _const_spec((D_MODEL, D_FF)), _const_spec((D_FF, D_MODEL)),
                  _const_spec((1, D_MODEL))],
        out_specs=xspec,
        out_shape=jax.ShapeDtypeStruct(x.shape, F32),
        compiler_params=pltpu.CompilerParams(
            dimension_semantics=("arbitrary", "arbitrary"), vmem_limit_bytes=VMEM_LIMIT_V7X),
        name="ffn_final" if final else "ffn",
    )(x, sh, sc, g, norm_g.reshape(1, D_MODEL), wg, wu, wd, final_g.reshape(1, D_MODEL))


def _modulated_norm(x, ng_ref, sc_ref, sh_ref):
    return _rms_norm(x, ng_ref[...]) * (1.0 + sc_ref[...]) + sh_ref[...]


def _gmlp_branch(h, win_ref, lng_ref, lnb_ref, wmix_ref, mmask_ref, bst_ref, wa_ref, oa_scr):
    rows = h.shape[0]
    u = _gelu_tanh(_dot(h, win_ref[:, OFF_U:OFF_V]))
    gv = _gelu_tanh(_dot(h, win_ref[:, OFF_V:OFF_Q]))
    mu = jnp.mean(gv, axis=-1, keepdims=True)
    var = jnp.mean(jnp.square(gv - mu), axis=-1, keepdims=True)
    v = (gv - mu) * lax.rsqrt(var + EPS) * lng_ref[...] + lnb_ref[...]
    vb = v.astype(BF16)
    mmask = mmask_ref[...]
    for g in range(GROUPS):
        wg = (wmix_ref[g] * mmask).astype(BF16)
        cols = slice(g * GROUP_DIM, (g + 1) * GROUP_DIM)
        for c in range(rows // CHUNK):
            rws = slice(c * CHUNK, (c + 1) * CHUNK)
            mix = _dot(wg, vb[rws, cols]) + bst_ref[:, g:g + 1]
            oa_scr[rws, cols] = (u[rws, cols] * mix).astype(BF16)
    pa = _dot(oa_scr[...], wa_ref[...])
    ga = _dot(h, win_ref[:, OFF_GA:OFF_GB])
    return jax.nn.sigmoid(ga) * pa, v


def _rotary_heads(z, cos2, sin2):
    out = []
    for hd in range(HEADS):
        zh = z[:, hd * DK:(hd + 1) * DK]
        out.append(zh * cos2 + pltpu.roll(zh, DK // 2, 1) * sin2)
    return out


def _intra(qc, kc, vc, dmask):
    scores = lax.dot_general(qc, kc, (((1,), (1,)), ((), ())), preferred_element_type=F32) * dmask
    return _dot(scores.astype(BF16), vc)


def _head_rms(o):
    return o * lax.rsqrt(jnp.mean(o * o, axis=-1, keepdims=True) + EPS)


def _mix_tail(h, merged, ob_scr, win_ref, wb_ref, wo_ref):
    ob = (ob_scr[...] * _silu(_dot(h, win_ref[:, OFF_RG:OFF_GA]))).astype(BF16)
    pb = _dot(ob, wb_ref[...])
    gb = _dot(h, win_ref[:, OFF_GB:OFF_END])
    merged = merged + jax.nn.sigmoid(gb) * pb
    return _dot(merged.astype(BF16), wo_ref[...])


def _mix_prompt_kernel(x_ref, sh_ref, sc_ref, g_ref, ng_ref, win_ref, lng_ref, lnb_ref, wmix_ref, mmask_ref,
                       bst_ref, cos_ref, sin_ref, dmask_ref, qd_ref, kd_ref, sdec_ref, wa_ref, wb_ref, wo_ref,
                       o_ref, s_ref, oa_scr, ob_scr):
    @pl.when(pl.program_id(1) == 0)
    def _():
        s_ref[...] = jnp.zeros(s_ref.shape, F32)

    x = x_ref[0]
    tm = x.shape[0]
    h = _modulated_norm(x, ng_ref, sc_ref.at[0], sh_ref.at[0]).astype(BF16)
    merged, _ = _gmlp_branch(h, win_ref, lng_ref, lnb_ref, wmix_ref, mmask_ref, bst_ref, wa_ref, oa_scr)

    cos2, sin2 = cos_ref[...], sin_ref[...]
    qs = _rotary_heads(_dot(h, win_ref[:, OFF_Q:OFF_K]), cos2, sin2)
    ks = _rotary_heads(_dot(h, win_ref[:, OFF_K:OFF_RV]), cos2, sin2)
    rv = _dot(h, win_ref[:, OFF_RV:OFF_RG]).astype(BF16)
    kscale = np.float32(DK ** -0.5)
    for hd in range(HEADS):
        q = qs[hd].astype(BF16)
        k = ks[hd] * kscale
        kb = k.astype(BF16)
        for c in range(tm // CHUNK):
            rws = slice(c * CHUNK, (c + 1) * CHUNK)
            qc, vc = q[rws], rv[rws, hd * DV:(hd + 1) * DV]
            state = s_ref[0, hd]
            inter = _dot(qc, state.astype(BF16)) * qd_ref[:, hd:hd + 1]
            o = _intra(qc, kb[rws], vc, dmask_ref[hd]) + inter
            ob_scr[rws, hd * DV:(hd + 1) * DV] = _head_rms(o)
            kd = k[rws] * kd_ref[:, hd:hd + 1]
            upd = _dot(jnp.transpose(kd).astype(BF16), vc)
            s_ref[0, hd] = sdec_ref[hd] * state + upd
    o_ref[0] = x + g_ref[0] * _mix_tail(h, merged, ob_scr, win_ref, wb_ref, wo_ref)


def _mix_sample_kernel(x_ref, sh_ref, sc_ref, g_ref, ng_ref, win_ref, lng_ref, lnb_ref, wmix_ref, mmask_ref,
                       bst_ref, cos_ref, sin_ref, dmask_ref, qd_ref, kd_ref, sdec_ref, wa_ref, wb_ref, wo_ref,
                       s0_ref, o_ref, s_ref, vrow_ref, oa_scr, ob_scr, q_scr, kt_scr, rv_scr, h_scr, mg_scr):
    j = pl.program_id(1)
    nseq, ntok, _ = x_ref.shape

    @pl.when(j == 0)
    def _():
        x = x_ref[...]
        h = _modulated_norm(x, ng_ref, sc_ref, sh_ref).reshape(nseq * ntok, D_MODEL).astype(BF16)
        h_scr[...] = h
        merged, v = _gmlp_branch(h, win_ref, lng_ref, lnb_ref, wmix_ref, mmask_ref, bst_ref, wa_ref, oa_scr)
        mg_scr[...] = merged
        vrow_ref[...] = v.reshape(nseq, ntok, D_MODEL)
        cos2, sin2 = cos_ref[...], sin_ref[...]
        qs = _rotary_heads(_dot(h, win_ref[:, OFF_Q:OFF_K]), cos2, sin2)
        ks = _rotary_heads(_dot(h, win_ref[:, OFF_K:OFF_RV]), cos2, sin2)
        rv = _dot(h, win_ref[:, OFF_RV:OFF_RG]).astype(BF16)
        rv_scr[...] = rv
        kscale = np.float32(DK ** -0.5)
        for hd in range(HEADS):
            k = ks[hd] * kscale
            q_scr[:, hd * DK:(hd + 1) * DK] = qs[hd]
            kt_scr[hd * DK:(hd + 1) * DK, :] = jnp.transpose(k * kd_ref[:, hd:hd + 1])
            ob_scr[:, hd * DV:(hd + 1) * DV] = _intra(
                qs[hd].astype(BF16), k.astype(BF16), rv[:, hd * DV:(hd + 1) * DV], dmask_ref[hd])

    rows = pl.ds(pl.multiple_of(j * ntok, ntok), ntok)
    in_seq = lax.broadcasted_iota(jnp.int32, (DK, nseq * ntok), 1) // ntok == j
    for hd in range(HEADS):
        state = s0_ref[0, hd]
        qj = q_scr[rows, hd * DK:(hd + 1) * DK].astype(BF16)
        inter = _dot(qj, state.astype(BF16)) * qd_ref[0:ntok, hd:hd + 1]
        ob_scr[rows, hd * DV:(hd + 1) * DV] += inter
        kt = jnp.where(in_seq, kt_scr[hd * DK:(hd + 1) * DK, :], 0.0).astype(BF16)
        upd = _dot(kt, rv_scr[:, hd * DV:(hd + 1) * DV])
        s_ref[0, hd] = sdec_ref[hd] * state + upd

    @pl.when(j == nseq - 1)
    def _():
        for hd in range(HEADS):
            cols = slice(hd * DV, (hd + 1) * DV)
            ob_scr[:, cols] = _head_rms(ob_scr[:, cols])
        out = _mix_tail(h_scr[...], mg_scr[...], ob_scr, win_ref, wb_ref, wo_ref)
        o_ref[...] = x_ref[...] + g_ref[...] * out.reshape(nseq, ntok, D_MODEL)


def _retention_tables(seq_len):
    lc = min(seq_len, CHUNK)
    log_gamma = jnp.log1p(-jnp.power(2.0, -5.0 - jnp.arange(HEADS, dtype=F32)))
    idx = jnp.arange(lc, dtype=F32)
    diff = idx[:, None] - idx[None, :]
    causal = diff >= 0
    decay = jnp.where(causal[None], jnp.exp(jnp.maximum(diff, 0.0)[None] * log_gamma[:, None, None]), 0.0)
    q_decay = jnp.exp((idx + 1.0)[:, None] * log_gamma[None, :])
    k_decay = jnp.exp((lc - 1.0 - idx)[:, None] * log_gamma[None, :])
    s_decay = jnp.exp(lc * log_gamma)
    rep = CHUNK // lc
    blockdiag = jnp.kron(jnp.eye(rep, dtype=F32), jnp.ones((lc, lc), F32))
    dmask = jnp.tile(decay, (1, rep, rep)) * blockdiag[None]
    mmask = jnp.tile(jnp.tril(jnp.ones((lc, lc), F32)), (rep, rep)) * blockdiag
    return dmask, jnp.tile(q_decay, (rep, 1)), jnp.tile(k_decay, (rep, 1)), s_decay, mmask


def _rotary_tables(pos):
    half = DK // 2
    inv = ROPE_BASE ** (-jnp.arange(half, dtype=F32) / half)
    ang = pos.astype(F32)[:, None] * inv[None, :]
    cos, sin = jnp.cos(ang), jnp.sin(ang)
    return jnp.concatenate([cos, cos], axis=-1), jnp.concatenate([-sin, sin], axis=-1)


def _mix_common_specs():
    return [_const_spec((1, D_MODEL)),
            _const_spec((D_MODEL, OFF_END)),
            _const_spec((1, D_MODEL)), _const_spec((1, D_MODEL)),
            _const_spec((GROUPS, CHUNK, CHUNK)), _const_spec((CHUNK, CHUNK)),
            _const_spec((CHUNK, GROUPS))]


def _mix_tail_specs():
    return [_const_spec((HEADS, CHUNK, CHUNK)),
            _const_spec((CHUNK, HEADS)), _const_spec((CHUNK, HEADS)),
            pl.BlockSpec(memory_space=pltpu.SMEM),
            _const_spec((D_MODEL, D_MODEL)), _const_spec((RET_V, D_MODEL)), _const_spec((D_MODEL, D_MODEL))]


def _mix_prompt(x, sh, sc, g, nm_g, w_in, ln_g, ln_b, gm_ws, gm_bs, w_a, w_b, w_o, *, tm):
    nb, seq, _ = x.shape
    dmask, qd, kd, sdec, mmask = _retention_tables(seq)
    cos2, sin2 = _rotary_tables(jnp.arange(seq))
    xspec = pl.BlockSpec((1, tm, D_MODEL), lambda b, j: (b, j, 0))
    mspec = pl.BlockSpec((1, 1, D_MODEL), lambda b, j: (b, 0, 0))
    tspec = pl.BlockSpec((tm, DK), lambda b, j: (j, 0))
    return pl.pallas_call(
        _mix_prompt_kernel,
        grid=(nb, seq // tm),
        in_specs=[xspec, mspec, mspec, mspec] + _mix_common_specs() + [tspec, tspec] + _mix_tail_specs(),
        out_specs=[xspec, pl.BlockSpec((1, HEADS, DK, DV), lambda b, j: (b, 0, 0, 0))],
        out_shape=[jax.ShapeDtypeStruct(x.shape, F32), jax.ShapeDtypeStruct((nb, HEADS, DK, DV), F32)],
        scratch_shapes=[pltpu.VMEM((tm, D_MODEL), BF16), pltpu.VMEM((tm, RET_V), F32)],
        compiler_params=pltpu.CompilerParams(
            dimension_semantics=("arbitrary", "arbitrary"), vmem_limit_bytes=VMEM_LIMIT_V7X),
        name="mix_prompt",
    )(x, sh, sc, g, nm_g.reshape(1, D_MODEL), w_in, ln_g.reshape(1, D_MODEL), ln_b.reshape(1, D_MODEL),
      gm_ws, mmask, jnp.transpose(gm_bs), cos2, sin2, dmask, qd, kd, sdec, w_a, w_b, w_o)


def _mix_sample(x, sh, sc, g, nm_g, w_in, ln_g, ln_b, gm_ws, gm_bs, w_a, w_b, w_o, state, *, pos0):
    nb, ntok, _ = x.shape
    nseq = CHUNK // ntok
    dmask, qd, kd, sdec, mmask = _retention_tables(ntok)
    cos2, sin2 = _rotary_tables(pos0 + jnp.arange(ntok))
    cos2, sin2 = jnp.tile(cos2, (nseq, 1)), jnp.tile(sin2, (nseq, 1))
    wmix = jnp.tile(gm_ws[:, :ntok, :ntok], (1, nseq, nseq))
    bst = jnp.tile(jnp.transpose(gm_bs[:, :ntok]), (nseq, 1))
    xspec = pl.BlockSpec((nseq, ntok, D_MODEL), lambda i, j: (i, 0, 0))
    mspec = pl.BlockSpec((nseq, 1, D_MODEL), lambda i, j: (i, 0, 0))
    sspec = pl.BlockSpec((1, HEADS, DK, DV), lambda i, j: (i * nseq + j, 0, 0, 0))
    rows = nseq * ntok
    return pl.pallas_call(
        _mix_sample_kernel,
        grid=(nb // nseq, nseq),
        in_specs=([xspec, mspec, mspec, mspec] + _mix_common_specs()
                  + [_const_spec((rows, DK)), _const_spec((rows, DK))] + _mix_tail_specs() + [sspec]),
        out_specs=[xspec, sspec, xspec],
        out_shape=[jax.ShapeDtypeStruct(x.shape, F32), jax.ShapeDtypeStruct(state.shape, F32),
                   jax.ShapeDtypeStruct(x.shape, F32)],
        scratch_shapes=[pltpu.VMEM((rows, D_MODEL), BF16),
                        pltpu.VMEM((rows, RET_V), F32),
                        pltpu.VMEM((rows, D_MODEL), F32),
                        pltpu.VMEM((D_MODEL, rows), F32),
                        pltpu.VMEM((rows, RET_V), BF16),
                        pltpu.VMEM((rows, D_MODEL), BF16),
                        pltpu.VMEM((rows, D_MODEL), F32)],
        compiler_params=pltpu.CompilerParams(
            dimension_semantics=("arbitrary", "arbitrary"), vmem_limit_bytes=VMEM_LIMIT_V7X),
        name="mix_sample",
    )(x, sh, sc, g, nm_g.reshape(1, D_MODEL), w_in, ln_g.reshape(1, D_MODEL), ln_b.reshape(1, D_MODEL),
      wmix, mmask, bst, cos2, sin2, dmask, qd, kd, sdec, w_a, w_b, w_o, state)


def kernel(x_prompt, x_sample, state_ret, c_prompt, c_sample, w_ada, b_ada, n1_g, w1_gate, w1_up, w1_down,
           nm_g, w_in, gm_ln_g, gm_ln_b, gm_ws, gm_bs, w_a, w_b, w_o, n2_g, w2_gate, w2_up, w2_down, final_g):
    depth = w_ada.shape[0]
    assert depth == 1, "single-layer step"
    nbp, seq, _ = x_prompt.shape
    nbs, ntok, _ = x_sample.shape
    bf =lambda w: w[0].astype(BF16)

    mods = _ada(jnp.concatenate([c_prompt, c_sample], axis=0), bf(w_ada), b_ada[0])
    mods = mods.reshape(nbp + nbs, 1, N_MOD, D_MODEL)
    mp = [mods[:nbp, :, i] for i in range(N_MOD)]
    ms = [mods[nbp:, :, i] for i in range(N_MOD)]

    w1 = (bf(w1_gate), bf(w1_up), bf(w1_down))
    w2 = (bf(w2_gate), bf(w2_up), bf(w2_down))
    wmix = (nm_g[0], bf(w_in), gm_ln_g[0], gm_ln_b[0], gm_ws[0], gm_bs[0], bf(w_a), bf(w_b), bf(w_o))

    yp = _ffn(x_prompt, mp[0], mp[1], mp[2], n1_g[0], *w1, final_g, block=(1, 512), final=False)
    yp, sp = _mix_prompt(yp, mp[3], mp[4], mp[5], *wmix, tm=256)
    yp = _ffn(yp, mp[6], mp[7], mp[8], n2_g[0], *w2, final_g, block=(1, 512), final=True)

    ys = _ffn(x_sample, ms[0], ms[1], ms[2], n1_g[0], *w1, final_g, block=(64, ntok), final=False)
    ys, ss, vs = _mix_sample(ys, ms[3], ms[4], ms[5], *wmix, state_ret[0].astype(F32), pos0=PAST_LEN)
    ys = _ffn(ys, ms[6], ms[7], ms[8], n2_g[0], *w2, final_g, block=(64, ntok), final=True)

    return (yp, ys, sp[None], ss[None], vs[None])
```

```python
import functools

import jax
import jax.numpy as jnp
import numpy as np
from jax import lax
from jax.experimental import pallas as pl
from jax.experimental.pallas import tpu as pltpu

F32 = jnp.float32
BF16 = jnp.bfloat16

D_MODEL = 1024
D_FF = 2816
N_MOD = 9
EPS = 1e-6
ROPE_BASE = 10000.0
PAST_LEN = 16384
CHUNK = 128
GROUPS = 8
GROUP_DIM = D_MODEL // GROUPS
HEADS = 8
DK = D_MODEL // HEADS
DV = 2 * DK
RET_V = HEADS * DV
OFF_U, OFF_V, OFF_Q, OFF_K, OFF_RV, OFF_RG, OFF_GA, OFF_GB, OFF_END = (
    0, 1024, 2048, 3072, 4096, 6144, 8192, 9216, 10240)

VMEM_LIMIT_V7X = 56 * 1024 * 1024
FFN_ROWS = 512
MIX_ROWS = 256


def _dot(a, b):
    return jnp.dot(a, b, preferred_element_type=F32)


def _silu(x):
    return x * jax.nn.sigmoid(x)


def _gelu_tanh(x):
    c = np.float32(np.sqrt(2.0 / np.pi))
    return 0.5 * x * (1.0 + jnp.tanh(c * (x + 0.044715 * (x * x * x))))


def _rms_norm(x, gain):
    return x * lax.rsqrt(jnp.mean(x * x, axis=-1, keepdims=True) + EPS) * gain


def _const_spec(shape):
    nd = len(shape)
    return pl.BlockSpec(shape, lambda *_: (0,) * nd, pipeline_mode=pl.Buffered(1))


def _mod_specs(bb, row0, first):
    assert row0 % bb == 0
    return [pl.BlockSpec((bb, 1, D_MODEL), lambda i, j, k=k: (row0 // bb + i, 0, k))
            for k in range(first, first + 3)]


def _ada_kernel(c_ref, w_ref, b_ref, o_ref):
    s = _silu(c_ref[...]).astype(BF16)
    o_ref[...] = _dot(s, w_ref[...]) + b_ref[...]


def _ada(c_all, w_ada, b_ada):
    rows = c_all.shape[0]
    n = w_ada.shape[1]
    bn = 3 * D_MODEL
    return pl.pallas_call(
        _ada_kernel,
        grid=(n // bn,),
        in_specs=[pl.BlockSpec((rows, D_MODEL), lambda j: (0, 0)),
                  pl.BlockSpec((D_MODEL, bn), lambda j: (0, j)),
                  pl.BlockSpec((1, bn), lambda j: (0, j))],
        out_specs=pl.BlockSpec((rows, bn), lambda j: (0, j)),
        out_shape=jax.ShapeDtypeStruct((rows, n), F32),
        name="ada",
    )(c_all, w_ada, b_ada.reshape(1, n))


def _ffn_kernel(x_ref, sh_ref, sc_ref, g_ref, ng_ref, wg_ref, wu_ref, wd_ref, fg_ref, o_ref, *, final):
    x = x_ref[...]
    bb, r, _ = x.shape
    h = _rms_norm(x, ng_ref[...]) * (1.0 + sc_ref[...]) + sh_ref[...]
    h = h.reshape(bb * r, D_MODEL).astype(BF16)
    act = (_silu(_dot(h, wg_ref[...])) * _dot(h, wu_ref[...])).astype(BF16)
    y = _dot(act, wd_ref[...]).reshape(bb, r, D_MODEL)
    out = x + (0.5 * g_ref[...]) * y
    if final:
        out = _rms_norm(out, fg_ref[...])
    o_ref[...] = out


def _ffn(x, mods, mod_row0, mod_first, norm_g, wg, wu, wd, final_g, *, final):
    nb, r, _ = x.shape
    br = min(r, FFN_ROWS)
    bb = FFN_ROWS // br
    xspec = pl.BlockSpec((bb, br, D_MODEL), lambda i, j: (i, j, 0))
    return pl.pallas_call(
        functools.partial(_ffn_kernel, final=final),
        grid=(nb // bb, r // br),
        in_specs=[xspec] + _mod_specs(bb, mod_row0, mod_first) + [
            _const_spec((1, D_MODEL)),
            _const_spec((D_MODEL, D_FF)), _const_spec((D_MODEL, D_FF)), _const_spec((D_FF, D_MODEL)),
            _const_spec((1, D_MODEL))],
        out_specs=xspec,
        out_shape=jax.ShapeDtypeStruct(x.shape, F32),
        compiler_params=pltpu.CompilerParams(
            dimension_semantics=("arbitrary", "arbitrary"), vmem_limit_bytes=VMEM_LIMIT_V7X),
        name="ffn_final" if final else "ffn",
    )(x, mods, mods, mods, norm_g.reshape(1, D_MODEL), wg, wu, wd, final_g.reshape(1, D_MODEL))


def _modulated_norm(x, ng_ref, sc_ref, sh_ref):
    return _rms_norm(x, ng_ref[...]) * (1.0 + sc_ref[...]) + sh_ref[...]


def _gmlp_branch(h, win_ref, lng_ref, lnb_ref, wmix_ref, mmask_ref, bst_ref, wa_ref, oa_scr):
    rows = h.shape[0]
    u = _gelu_tanh(_dot(h, win_ref[:, OFF_U:OFF_V]))
    gv = _gelu_tanh(_dot(h, win_ref[:, OFF_V:OFF_Q]))
    mu = jnp.mean(gv, axis=-1, keepdims=True)
    var = jnp.mean(jnp.square(gv - mu), axis=-1, keepdims=True)
    v = (gv - mu) * lax.rsqrt(var + EPS) * lng_ref[...] + lnb_ref[...]
    vb = v.astype(BF16)
    mmask = mmask_ref[...]
    for g in range(GROUPS):
        wg = (wmix_ref[g] * mmask).astype(BF16)
        cols = slice(g * GROUP_DIM, (g + 1) * GROUP_DIM)
        for c in range(rows // CHUNK):
            rws = slice(c * CHUNK, (c + 1) * CHUNK)
            mix = _dot(wg, vb[rws, cols]) + bst_ref[:, g:g + 1]
            oa_scr[rws, cols] = (u[rws, cols] * mix).astype(BF16)
    pa = _dot(oa_scr[...], wa_ref[...])
    ga = _dot(h, win_ref[:, OFF_GA:OFF_GB])
    return jax.nn.sigmoid(ga) * pa, v


def _rotary_heads(z, cos2, sin2):
    out = []
    for hd in range(HEADS):
        zh = z[:, hd * DK:(hd + 1) * DK]
        out.append(zh * cos2 + pltpu.roll(zh, DK // 2, 1) * sin2)
    return out


def _intra(qc, kc, vc, dmask):
    scores = lax.dot_general(qc, kc, (((1,), (1,)), ((), ())), preferred_element_type=F32) * dmask
    return _dot(scores.astype(BF16), vc)


def _head_rms(o):
    return o * lax.rsqrt(jnp.mean(o * o, axis=-1, keepdims=True) + EPS)


def _mix_tail(h, merged, ob_scr, win_ref, wb_ref, wo_ref):
    ob = (ob_scr[...] * _silu(_dot(h, win_ref[:, OFF_RG:OFF_GA]))).astype(BF16)
    pb = _dot(ob, wb_ref[...])
    gb = _dot(h, win_ref[:, OFF_GB:OFF_END])
    merged = merged + jax.nn.sigmoid(gb) * pb
    return _dot(merged.astype(BF16), wo_ref[...])


def _mix_prompt_kernel(x_ref, sh_ref, sc_ref, g_ref, ng_ref, win_ref, lng_ref, lnb_ref, wmix_ref, mmask_ref,
                       bst_ref, cos_ref, sin_ref, dmask_ref, qd_ref, kd_ref, wa_ref, wb_ref, wo_ref,
                       o_ref, s_ref, oa_scr, ob_scr, *, s_decay):
    @pl.when(pl.program_id(1) == 0)
    def _():
        s_ref[...] = jnp.zeros(s_ref.shape, F32)

    x = x_ref[0]
    tm = x.shape[0]
    h = _modulated_norm(x, ng_ref, sc_ref.at[0], sh_ref.at[0]).astype(BF16)
    merged, _ = _gmlp_branch(h, win_ref, lng_ref, lnb_ref, wmix_ref, mmask_ref, bst_ref, wa_ref, oa_scr)

    cos2, sin2 = cos_ref[...], sin_ref[...]
    qs = _rotary_heads(_dot(h, win_ref[:, OFF_Q:OFF_K]), cos2, sin2)
    ks = _rotary_heads(_dot(h, win_ref[:, OFF_K:OFF_RV]), cos2, sin2)
    rv = _dot(h, win_ref[:, OFF_RV:OFF_RG]).astype(BF16)
    kscale = np.float32(DK ** -0.5)
    for hd in range(HEADS):
        q = qs[hd].astype(BF16)
        k = ks[hd] * kscale
        kb = k.astype(BF16)
        for c in range(tm // CHUNK):
            rws = slice(c * CHUNK, (c + 1) * CHUNK)
            qc, vc = q[rws], rv[rws, hd * DV:(hd + 1) * DV]
            state = s_ref[0, 0, hd]
            inter = _dot(qc, state.astype(BF16)) * qd_ref[:, hd:hd + 1]
            o = _intra(qc, kb[rws], vc, dmask_ref[hd]) + inter
            ob_scr[rws, hd * DV:(hd + 1) * DV] = _head_rms(o)
            kd = k[rws] * kd_ref[:, hd:hd + 1]
            upd = _dot(jnp.transpose(kd).astype(BF16), vc)
            s_ref[0, 0, hd] = s_decay[hd] * state + upd
    o_ref[0] = x + g_ref[0] * _mix_tail(h, merged, ob_scr, win_ref, wb_ref, wo_ref)


def _mix_sample_kernel(x_ref, sh_ref, sc_ref, g_ref, ng_ref, win_ref, lng_ref, lnb_ref, wmix_ref, mmask_ref,
                       bst_ref, cos_ref, sin_ref, dmask_ref, qd_ref, kd_ref, wa_ref, wb_ref, wo_ref, s0_ref,
                       o_ref, s_ref, vrow_ref, oa_scr, ob_scr, q_scr, kt_scr, rv_scr, h_scr, mg_scr,
                       *, s_decay):
    j = pl.program_id(1)
    nseq, ntok, _ = x_ref.shape
    rows = nseq * ntok
    seq_per_chunk = CHUNK // ntok

    @pl.when(j == 0)
    def _():
        h = _modulated_norm(x_ref[...], ng_ref, sc_ref, sh_ref).reshape(rows, D_MODEL).astype(BF16)
        h_scr[...] = h
        merged, v = _gmlp_branch(h, win_ref, lng_ref, lnb_ref, wmix_ref, mmask_ref, bst_ref, wa_ref, oa_scr)
        mg_scr[...] = merged
        vrow_ref[0] = v.reshape(nseq, ntok, D_MODEL)
        cos2, sin2 = cos_ref[...], sin_ref[...]
        qs = _rotary_heads(_dot(h, win_ref[:, OFF_Q:OFF_K]), cos2, sin2)
        ks = _rotary_heads(_dot(h, win_ref[:, OFF_K:OFF_RV]), cos2, sin2)
        rv = _dot(h, win_ref[:, OFF_RV:OFF_RG]).astype(BF16)
        kscale = np.float32(DK ** -0.5)
        for c in range(rows // CHUNK):
            rws = slice(c * CHUNK, (c + 1) * CHUNK)
            rv_scr[c] = rv[rws]
            for hd in range(HEADS):
                k = ks[hd][rws] * kscale
                kt_scr[c, hd * DK:(hd + 1) * DK, :] = jnp.transpose(k * kd_ref[:, hd:hd + 1])
                ob_scr[rws, hd * DV:(hd + 1) * DV] = _intra(
                    qs[hd][rws].astype(BF16), k.astype(BF16), rv[rws, hd * DV:(hd + 1) * DV], dmask_ref[hd])
        for hd in range(HEADS):
            q_scr[:, hd * DK:(hd + 1) * DK] = qs[hd]

    tok = pl.ds(pl.multiple_of(j * ntok, ntok), ntok)
    c = j // seq_per_chunk
    in_seq = lax.broadcasted_iota(jnp.int32, (DK, CHUNK), 1) // ntok == j % seq_per_chunk
    for hd in range(HEADS):
        state = s0_ref[0, 0, hd]
        qj = q_scr[tok, hd * DK:(hd + 1) * DK].astype(BF16)
        inter = _dot(qj, state.astype(BF16)) * qd_ref[0:ntok, hd:hd + 1]
        ob_scr[tok, hd * DV:(hd + 1) * DV] += inter
        kt = jnp.where(in_seq, kt_scr[c, hd * DK:(hd + 1) * DK, :], 0.0).astype(BF16)
        upd = _dot(kt, rv_scr[c, :, hd * DV:(hd + 1) * DV])
        s_ref[0, 0, hd] = s_decay[hd] * state + upd

    @pl.when(j == nseq - 1)
    def _():
        for hd in range(HEADS):
            cols = slice(hd * DV, (hd + 1) * DV)
            ob_scr[:, cols] = _head_rms(ob_scr[:, cols])
        out = _mix_tail(h_scr[...], mg_scr[...], ob_scr, win_ref, wb_ref, wo_ref)
        o_ref[...] = x_ref[...] + g_ref[...] * out.reshape(nseq, ntok, D_MODEL)


def _retention_tables(seq_len):
    lc = min(seq_len, CHUNK)
    log_gamma = np.log1p(-np.power(2.0, -5.0 - np.arange(HEADS)))
    idx = np.arange(lc, dtype=np.float64)
    diff = idx[:, None] - idx[None, :]
    decay = np.where(diff >= 0, np.exp(np.maximum(diff, 0.0)[None] * log_gamma[:, None, None]), 0.0)
    q_decay = np.exp((idx + 1.0)[:, None] * log_gamma[None, :])
    k_decay = np.exp((lc - 1.0 - idx)[:, None] * log_gamma[None, :])
    s_decay = tuple(float(v) for v in np.exp(lc * log_gamma).astype(np.float32))
    rep = CHUNK // lc
    blockdiag = np.kron(np.eye(rep), np.ones((lc, lc)))
    dmask = np.tile(decay, (1, rep, rep)) * blockdiag[None]
    mmask = np.tile(np.tril(np.ones((lc, lc))), (rep, rep)) * blockdiag
    f32 = lambda a: jnp.asarray(a.astype(np.float32))
    return f32(dmask), f32(np.tile(q_decay, (rep, 1))), f32(np.tile(k_decay, (rep, 1))), s_decay, f32(mmask)


def _rotary_tables(pos, rows):
    half = DK // 2
    inv = ROPE_BASE ** (-np.arange(half, dtype=np.float64) / half)
    ang = np.asarray(pos, np.float64)[:, None] * inv[None, :]
    cos, sin = np.cos(ang), np.sin(ang)
    rep = rows // len(pos)
    cos2 = np.tile(np.concatenate([cos, cos], axis=-1), (rep, 1))
    sin2 = np.tile(np.concatenate([-sin, sin], axis=-1), (rep, 1))
    return jnp.asarray(cos2.astype(np.float32)), jnp.asarray(sin2.astype(np.float32))


def _mix_weight_specs():
    return [_const_spec((1, D_MODEL)),
            _const_spec((D_MODEL, OFF_END)),
            _const_spec((1, D_MODEL)), _const_spec((1, D_MODEL)),
            _const_spec((GROUPS, CHUNK, CHUNK)), _const_spec((CHUNK, CHUNK)),
            _const_spec((CHUNK, GROUPS))]


def _mix_out_weight_specs():
    return [_const_spec((HEADS, CHUNK, CHUNK)),
            _const_spec((CHUNK, HEADS)), _const_spec((CHUNK, HEADS)),
            _const_spec((D_MODEL, D_MODEL)), _const_spec((RET_V, D_MODEL)), _const_spec((D_MODEL, D_MODEL))]


def _mix_prompt(x, mods, mod_row0, nm_g, w_in, ln_g, ln_b, gm_ws, gm_bs, w_a, w_b, w_o):
    nb, seq, _ = x.shape
    tm = MIX_ROWS
    dmask, qd, kd, s_decay, mmask = _retention_tables(seq)
    cos2, sin2 = _rotary_tables(np.arange(seq), seq)
    xspec = pl.BlockSpec((1, tm, D_MODEL), lambda b, j: (b, j, 0))
    tspec = pl.BlockSpec((tm, DK), lambda b, j: (j, 0))
    state_shape = (1, nb, HEADS, DK, DV)
    return pl.pallas_call(
        functools.partial(_mix_prompt_kernel, s_decay=s_decay),
        grid=(nb, seq // tm),
        in_specs=([xspec] + _mod_specs(1, mod_row0, 3) + _mix_weight_specs() + [tspec, tspec]
                  + _mix_out_weight_specs()),
        out_specs=[xspec, pl.BlockSpec((1, 1, HEADS, DK, DV), lambda b, j: (0, b, 0, 0, 0))],
        out_shape=[jax.ShapeDtypeStruct(x.shape, F32), jax.ShapeDtypeStruct(state_shape, F32)],
        scratch_shapes=[pltpu.VMEM((tm, D_MODEL), BF16), pltpu.VMEM((tm, RET_V), F32)],
        compiler_params=pltpu.CompilerParams(
            dimension_semantics=("arbitrary", "arbitrary"), vmem_limit_bytes=VMEM_LIMIT_V7X),
        name="mix_prompt",
    )(x, mods, mods, mods, nm_g.reshape(1, D_MODEL), w_in, ln_g.reshape(1, D_MODEL), ln_b.reshape(1, D_MODEL),
      gm_ws, mmask, jnp.transpose(gm_bs), cos2, sin2, dmask, qd, kd, w_a, w_b, w_o)


def _mix_sample(x, mods, mod_row0, nm_g, w_in, ln_g, ln_b, gm_ws, gm_bs, w_a, w_b, w_o, state):
    nb, ntok, _ = x.shape
    rows = MIX_ROWS
    nseq = rows // ntok
    nsub = rows // CHUNK
    rep = CHUNK // ntok
    dmask, qd, kd, s_decay, mmask = _retention_tables(ntok)
    cos2, sin2 = _rotary_tables(PAST_LEN + np.arange(ntok), rows)
    wmix = jnp.tile(gm_ws[:, :ntok, :ntok], (1, rep, rep))
    bst = jnp.tile(jnp.transpose(gm_bs[:, :ntok]), (rep, 1))
    xspec = pl.BlockSpec((nseq, ntok, D_MODEL), lambda i, j: (i, 0, 0))
    vspec = pl.BlockSpec((1, nseq, ntok, D_MODEL), lambda i, j: (0, i, 0, 0))
    sspec = pl.BlockSpec((1, 1, HEADS, DK, DV), lambda i, j: (0, i * nseq + j, 0, 0, 0))
    return pl.pallas_call(
        functools.partial(_mix_sample_kernel, s_decay=s_decay),
        grid=(nb // nseq, nseq),
        in_specs=([xspec] + _mod_specs(nseq, mod_row0, 3) + _mix_weight_specs()
                  + [_const_spec((rows, DK)), _const_spec((rows, DK))] + _mix_out_weight_specs() + [sspec]),
        out_specs=[xspec, sspec, vspec],
        out_shape=[jax.ShapeDtypeStruct(x.shape, F32), jax.ShapeDtypeStruct(state.shape, F32),
                   jax.ShapeDtypeStruct((1,) + x.shape, F32)],
        scratch_shapes=[pltpu.VMEM((rows, D_MODEL), BF16),
                        pltpu.VMEM((rows, RET_V), F32),
                        pltpu.VMEM((rows, D_MODEL), F32),
                        pltpu.VMEM((nsub, D_MODEL, CHUNK), F32),
                        pltpu.VMEM((nsub, CHUNK, RET_V), BF16),
                        pltpu.VMEM((rows, D_MODEL), BF16),
                        pltpu.VMEM((rows, D_MODEL), F32)],
        compiler_params=pltpu.CompilerParams(
            dimension_semantics=("arbitrary", "arbitrary"), vmem_limit_bytes=VMEM_LIMIT_V7X),
        name="mix_sample",
    )(x, mods, mods, mods, nm_g.reshape(1, D_MODEL), w_in, ln_g.reshape(1, D_MODEL), ln_b.reshape(1, D_MODEL),
      wmix, mmask, bst, cos2, sin2, dmask, qd, kd, w_a, w_b, w_o, state)


def kernel(x_prompt, x_sample, state_ret, c_prompt, c_sample, w_ada, b_ada, n1_g, w1_gate, w1_up, w1_down,
           nm_g, w_in, gm_ln_g, gm_ln_b, gm_ws, gm_bs, w_a, w_b, w_o, n2_g, w2_gate, w2_up, w2_down, final_g):
    assert w_ada.shape[0] == 1, "single-layer step"
    nbs = x_sample.shape[0]
    bf = lambda w: w[0].astype(BF16)

    mods = _ada(jnp.concatenate([c_sample, c_prompt], axis=0), bf(w_ada), b_ada[0])
    mods = mods.reshape(mods.shape[0], 1, N_MOD * D_MODEL)
    row_s, row_p = 0, nbs

    w1 = (bf(w1_gate), bf(w1_up), bf(w1_down))
    w2 = (bf(w2_gate), bf(w2_up), bf(w2_down))
    wmix = (nm_g[0], bf(w_in), gm_ln_g[0], gm_ln_b[0], gm_ws[0], gm_bs[0], bf(w_a), bf(w_b), bf(w_o))

    yp = _ffn(x_prompt, mods, row_p, 0, n1_g[0], *w1, final_g, final=False)
    yp, sp = _mix_prompt(yp, mods, row_p, *wmix)
    yp = _ffn(yp, mods, row_p, 6, n2_g[0], *w2, final_g, final=True)

    ys = _ffn(x_sample, mods, row_s, 0, n1_g[0], *w1, final_g, final=False)
    ys, ss, vs = _mix_sample(ys, mods, row_s, *wmix, state_ret.astype(F32))
    ys = _ffn(ys, mods, row_s, 6, n2_g[0], *w2, final_g, final=True)

    return (yp, ys, sp, ss, vs)
```

```python
import functools

import jax
import jax.numpy as jnp
import numpy as np
from jax import lax
from jax.experimental import pallas as pl
from jax.experimental.pallas import tpu as pltpu

F32 = jnp.float32
BF16 = jnp.bfloat16

D_MODEL = 1024
D_FF = 2816
N_MOD = 9
EPS = 1e-6
ROPE_BASE = 10000.0
PAST_LEN = 16384
CHUNK = 128
GROUPS = 8
GROUP_DIM = D_MODEL // GROUPS
HEADS = 8
DK = D_MODEL // HEADS
DV = 2 * DK
RET_V = HEADS * DV
OFF_U, OFF_V, OFF_Q, OFF_K, OFF_RV, OFF_RG, OFF_GA, OFF_GB, OFF_END = (
    0, 1024, 2048, 3072, 4096, 6144, 8192, 9216, 10240)

VMEM_LIMIT_V7X = 56 * 1024 * 1024
FFN_ROWS = 512
MIX_ROWS = 256
ADA_STEPS = 8
STATE_SEQS = 2
BF16_SUBLANES = 16


def _dot(a, b):
    return jnp.dot(a, b, preferred_element_type=F32)


def _silu(x):
    return x * jax.nn.sigmoid(x)


def _gelu_tanh(x):
    c = np.float32(np.sqrt(2.0 / np.pi))
    return 0.5 * x * (1.0 + jnp.tanh(c * (x + 0.044715 * (x * x * x))))


def _rms_norm(x, gain):
    return x * lax.rsqrt(jnp.mean(x * x, axis=-1, keepdims=True) + EPS) * gain


def _const_spec(shape):
    nd = len(shape)
    return pl.BlockSpec(shape, lambda *_: (0,) * nd, pipeline_mode=pl.Buffered(1))


def _mod_specs(bb, row0, first):
    assert row0 % bb == 0
    return [pl.BlockSpec((bb, 1, D_MODEL), lambda i, j, k=k: (row0 // bb + i, 0, k))
            for k in range(first, first + 3)]


def _cast_plan(weights, grid):
    steps = int(np.prod(grid))
    in_specs, out_specs, out_shapes = [], [], []
    for w in weights:
        rows, cols = w.shape
        nblk = max(n for n in range(1, steps + 1)
                   if steps % n == 0 and rows % n == 0 and (rows // n) % BF16_SUBLANES == 0)
        per_blk = steps // nblk

        def index(*ids, per_blk=per_blk):
            step = 0
            for extent, i in zip(grid, ids):
                step = step * extent + i
            return (step // per_blk, 0)

        spec = pl.BlockSpec((rows // nblk, cols), index)
        in_specs.append(spec)
        out_specs.append(spec)
        out_shapes.append(jax.ShapeDtypeStruct(w.shape, BF16))
    return in_specs, out_specs, out_shapes


def _cast_blocks(src_refs, dst_refs):
    for src, dst in zip(src_refs, dst_refs, strict=True):
        dst[...] = src[...].astype(BF16)


def _ada_kernel(c_ref, w_ref, b_ref, *refs):
    ncast = (len(refs) - 1) // 2
    o_ref = refs[ncast]
    s = _silu(c_ref[...]).astype(BF16)
    m = _dot(s, w_ref[...].astype(BF16)) + b_ref[...]
    for r in range(m.shape[0]):
        o_ref[r] = m[r:r + 1, :]
    _cast_blocks(refs[:ncast], refs[ncast + 1:])


def _ada(c_all, w_ada, b_ada, cast):
    rows = c_all.shape[0]
    n = w_ada.shape[1]
    grid = (ADA_STEPS,)
    bn = n // ADA_STEPS
    cast_in, cast_out, cast_shapes = _cast_plan(cast, grid)
    out = pl.pallas_call(
        _ada_kernel,
        grid=grid,
        in_specs=[pl.BlockSpec((rows, D_MODEL), lambda j: (0, 0)),
                  pl.BlockSpec((D_MODEL, bn), lambda j: (0, j)),
                  pl.BlockSpec((1, bn), lambda j: (0, j))] + cast_in,
        out_specs=[pl.BlockSpec((rows, 1, bn), lambda j: (0, 0, j))] + cast_out,
        out_shape=[jax.ShapeDtypeStruct((rows, 1, n), F32)] + cast_shapes,
        compiler_params=pltpu.CompilerParams(
            dimension_semantics=("arbitrary",), vmem_limit_bytes=VMEM_LIMIT_V7X),
        name="ada",
    )(c_all, w_ada, b_ada.reshape(1, n), *cast)
    return out[0], out[1:]


def _ffn_kernel(x_ref, sh_ref, sc_ref, g_ref, ng_ref, wg_ref, wu_ref, wd_ref, fg_ref, *refs, final):
    ncast = (len(refs) - 1) // 2
    o_ref = refs[ncast]
    x = x_ref[...]
    bb, r, _ = x.shape
    h = _rms_norm(x, ng_ref[...]) * (1.0 + sc_ref[...]) + sh_ref[...]
    h = h.reshape(bb * r, D_MODEL).astype(BF16)
    act = (_silu(_dot(h, wg_ref[...])) * _dot(h, wu_ref[...])).astype(BF16)
    y = _dot(act, wd_ref[...]).reshape(bb, r, D_MODEL)
    out = x + (0.5 * g_ref[...]) * y
    if final:
        out = _rms_norm(out, fg_ref[...])
    o_ref[...] = out
    _cast_blocks(refs[:ncast], refs[ncast + 1:])


def _ffn(x, mods, mod_row0, mod_first, norm_g, wg, wu, wd, final_g, *, final, cast=()):
    nb, r, _ = x.shape
    br = min(r, FFN_ROWS)
    bb = FFN_ROWS // br
    grid = (nb // bb, r // br)
    xspec = pl.BlockSpec((bb, br, D_MODEL), lambda i, j: (i, j, 0))
    cast_in, cast_out, cast_shapes = _cast_plan(cast, grid)
    out = pl.pallas_call(
        functools.partial(_ffn_kernel, final=final),
        grid=grid,
        in_specs=[xspec] + _mod_specs(bb, mod_row0, mod_first) + [
            _const_spec((1, D_MODEL)),
            _const_spec((D_MODEL, D_FF)), _const_spec((D_MODEL, D_FF)), _const_spec((D_FF, D_MODEL)),
            _const_spec((1, D_MODEL))] + cast_in,
        out_specs=[xspec] + cast_out,
        out_shape=[jax.ShapeDtypeStruct(x.shape, F32)] + cast_shapes,
        compiler_params=pltpu.CompilerParams(
            dimension_semantics=("arbitrary", "arbitrary"), vmem_limit_bytes=VMEM_LIMIT_V7X),
        name="ffn_final" if final else "ffn",
    )(x, mods, mods, mods, norm_g.reshape(1, D_MODEL), wg, wu, wd, final_g.reshape(1, D_MODEL), *cast)
    return out[0], out[1:]


def _modulated_norm(x, ng_ref, sc_ref, sh_ref):
    return _rms_norm(x, ng_ref[...]) * (1.0 + sc_ref[...]) + sh_ref[...]


def _gmlp_branch(h, win_ref, lng_ref, lnb_ref, wmix_ref, mmask_ref, bst_ref, wa_ref, oa_scr):
    rows = h.shape[0]
    u = _gelu_tanh(_dot(h, win_ref[:, OFF_U:OFF_V]))
    gv = _gelu_tanh(_dot(h, win_ref[:, OFF_V:OFF_Q]))
    mu = jnp.mean(gv, axis=-1, keepdims=True)
    var = jnp.mean(jnp.square(gv - mu), axis=-1, keepdims=True)
    v = (gv - mu) * lax.rsqrt(var + EPS) * lng_ref[...] + lnb_ref[...]
    vb = v.astype(BF16)
    mmask = mmask_ref[...]
    for g in range(GROUPS):
        wg = (wmix_ref[g] * mmask).astype(BF16)
        cols = slice(g * GROUP_DIM, (g + 1) * GROUP_DIM)
        for c in range(rows // CHUNK):
            rws = slice(c * CHUNK, (c + 1) * CHUNK)
            mix = _dot(wg, vb[rws, cols]) + bst_ref[:, g:g + 1]
            oa_scr[rws, cols] = (u[rws, cols] * mix).astype(BF16)
    pa = _dot(oa_scr[...], wa_ref[...])
    ga = _dot(h, win_ref[:, OFF_GA:OFF_GB])
    return jax.nn.sigmoid(ga) * pa, v


def _rotary_heads(z, cos2, sin2):
    out = []
    for hd in range(HEADS):
        zh = z[:, hd * DK:(hd + 1) * DK]
        out.append(zh * cos2 + pltpu.roll(zh, DK // 2, 1) * sin2)
    return out


def _intra(qc, kc, vc, dmask):
    scores = lax.dot_general(qc, kc, (((1,), (1,)), ((), ())), preferred_element_type=F32) * dmask
    return _dot(scores.astype(BF16), vc)


def _head_rms(o):
    return o * lax.rsqrt(jnp.mean(o * o, axis=-1, keepdims=True) + EPS)


def _mix_tail(h, merged, ob_scr, win_ref, wb_ref, wo_ref):
    ob = (ob_scr[...] * _silu(_dot(h, win_ref[:, OFF_RG:OFF_GA]))).astype(BF16)
    pb = _dot(ob, wb_ref[...])
    gb = _dot(h, win_ref[:, OFF_GB:OFF_END])
    merged = merged + jax.nn.sigmoid(gb) * pb
    return _dot(merged.astype(BF16), wo_ref[...])


def _mix_prompt_kernel(x_ref, sh_ref, sc_ref, g_ref, ng_ref, win_ref, lng_ref, lnb_ref, wmix_ref, mmask_ref,
                       bst_ref, cos_ref, sin_ref, dmask_ref, qd_ref, kd_ref, wa_ref, wb_ref, wo_ref,
                       o_ref, s_ref, oa_scr, ob_scr, *, s_decay):
    @pl.when(pl.program_id(1) == 0)
    def _():
        s_ref[...] = jnp.zeros(s_ref.shape, F32)

    x = x_ref[0]
    tm = x.shape[0]
    h = _modulated_norm(x, ng_ref, sc_ref.at[0], sh_ref.at[0]).astype(BF16)
    merged, _ = _gmlp_branch(h, win_ref, lng_ref, lnb_ref, wmix_ref, mmask_ref, bst_ref, wa_ref, oa_scr)

    cos2, sin2 = cos_ref[...], sin_ref[...]
    qs = _rotary_heads(_dot(h, win_ref[:, OFF_Q:OFF_K]), cos2, sin2)
    ks = _rotary_heads(_dot(h, win_ref[:, OFF_K:OFF_RV]), cos2, sin2)
    rv = _dot(h, win_ref[:, OFF_RV:OFF_RG]).astype(BF16)
    kscale = np.float32(DK ** -0.5)
    for hd in range(HEADS):
        q = qs[hd].astype(BF16)
        k = ks[hd] * kscale
        kb = k.astype(BF16)
        for c in range(tm // CHUNK):
            rws = slice(c * CHUNK, (c + 1) * CHUNK)
            qc, vc = q[rws], rv[rws, hd * DV:(hd + 1) * DV]
            state = s_ref[0, 0, hd]
            inter = _dot(qc, state.astype(BF16)) * qd_ref[:, hd:hd + 1]
            o = _intra(qc, kb[rws], vc, dmask_ref[hd]) + inter
            ob_scr[rws, hd * DV:(hd + 1) * DV] = _head_rms(o)
            kd = k[rws] * kd_ref[:, hd:hd + 1]
            upd = _dot(jnp.transpose(kd).astype(BF16), vc)
            s_ref[0, 0, hd] = s_decay[hd] * state + upd
    o_ref[0] = x + g_ref[0] * _mix_tail(h, merged, ob_scr, win_ref, wb_ref, wo_ref)


def _mix_sample_kernel(x_ref, sh_ref, sc_ref, g_ref, ng_ref, win_ref, lng_ref, lnb_ref, wmix_ref, mmask_ref,
                       bst_ref, cos_ref, sin_ref, dmask_ref, qd_ref, kd_ref, wa_ref, wb_ref, wo_ref, s0_ref,
                       o_ref, s_ref, vrow_ref, oa_scr, ob_scr, q_scr, kt_scr, rv_scr, h_scr, mg_scr,
                       *, s_decay):
    j = pl.program_id(1)
    nseq, ntok, _ = x_ref.shape
    rows = nseq * ntok
    seq_per_chunk = CHUNK // ntok

    @pl.when(j == 0)
    def _():
        h = _modulated_norm(x_ref[...], ng_ref, sc_ref, sh_ref).reshape(rows, D_MODEL).astype(BF16)
        h_scr[...] = h
        merged, v = _gmlp_branch(h, win_ref, lng_ref, lnb_ref, wmix_ref, mmask_ref, bst_ref, wa_ref, oa_scr)
        mg_scr[...] = merged
        vrow_ref[0] = v.reshape(nseq, ntok, D_MODEL)
        cos2, sin2 = cos_ref[...], sin_ref[...]
        qs = _rotary_heads(_dot(h, win_ref[:, OFF_Q:OFF_K]), cos2, sin2)
        ks = _rotary_heads(_dot(h, win_ref[:, OFF_K:OFF_RV]), cos2, sin2)
        rv = _dot(h, win_ref[:, OFF_RV:OFF_RG]).astype(BF16)
        kscale = np.float32(DK ** -0.5)
        for c in range(rows // CHUNK):
            rws = slice(c * CHUNK, (c + 1) * CHUNK)
            rv_scr[c] = rv[rws]
            for hd in range(HEADS):
                k = ks[hd][rws] * kscale
                kt_scr[c, hd * DK:(hd + 1) * DK, :] = jnp.transpose(k * kd_ref[:, hd:hd + 1])
                ob_scr[rws, hd * DV:(hd + 1) * DV] = _intra(
                    qs[hd][rws].astype(BF16), k.astype(BF16), rv[rws, hd * DV:(hd + 1) * DV], dmask_ref[hd])
        for hd in range(HEADS):
            q_scr[:, hd * DK:(hd + 1) * DK] = qs[hd]

    nstep = s0_ref.shape[1]
    for t in range(nstep):
        s = j * nstep + t
        tok = pl.ds(pl.multiple_of(s * ntok, ntok), ntok)
        c = s // seq_per_chunk
        in_seq = lax.broadcasted_iota(jnp.int32, (DK, CHUNK), 1) // ntok == s % seq_per_chunk
        for hd in range(HEADS):
            state = s0_ref[0, t, hd]
            qj = q_scr[tok, hd * DK:(hd + 1) * DK].astype(BF16)
            inter = _dot(qj, state.astype(BF16)) * qd_ref[0:ntok, hd:hd + 1]
            ob_scr[tok, hd * DV:(hd + 1) * DV] += inter
            kt = jnp.where(in_seq, kt_scr[c, hd * DK:(hd + 1) * DK, :], 0.0).astype(BF16)
            upd = _dot(kt, rv_scr[c, :, hd * DV:(hd + 1) * DV])
            s_ref[0, t, hd] = s_decay[hd] * state + upd

    @pl.when(j == pl.num_programs(1) - 1)
    def _():
        for hd in range(HEADS):
            cols = slice(hd * DV, (hd + 1) * DV)
            ob_scr[:, cols] = _head_rms(ob_scr[:, cols])
        out = _mix_tail(h_scr[...], mg_scr[...], ob_scr, win_ref, wb_ref, wo_ref)
        o_ref[...] = x_ref[...] + g_ref[...] * out.reshape(nseq, ntok, D_MODEL)


def _retention_tables(seq_len):
    lc = min(seq_len, CHUNK)
    log_gamma = np.log1p(-np.power(2.0, -5.0 - np.arange(HEADS)))
    idx = np.arange(lc, dtype=np.float64)
    diff = idx[:, None] - idx[None, :]
    decay = np.where(diff >= 0, np.exp(np.maximum(diff, 0.0)[None] * log_gamma[:, None, None]), 0.0)
    q_decay = np.exp((idx + 1.0)[:, None] * log_gamma[None, :])
    k_decay = np.exp((lc - 1.0 - idx)[:, None] * log_gamma[None, :])
    s_decay = tuple(float(v) for v in np.exp(lc * log_gamma).astype(np.float32))
    rep = CHUNK // lc
    blockdiag = np.kron(np.eye(rep), np.ones((lc, lc)))
    dmask = np.tile(decay, (1, rep, rep)) * blockdiag[None]
    mmask = np.tile(np.tril(np.ones((lc, lc))), (rep, rep)) * blockdiag
    f32 = lambda a: jnp.asarray(a.astype(np.float32))
    return f32(dmask), f32(np.tile(q_decay, (rep, 1))), f32(np.tile(k_decay, (rep, 1))), s_decay, f32(mmask)


def _rotary_tables(pos, rows):
    half = DK // 2
    inv = ROPE_BASE ** (-np.arange(half, dtype=np.float64) / half)
    ang = np.asarray(pos, np.float64)[:, None] * inv[None, :]
    cos, sin = np.cos(ang), np.sin(ang)
    rep = rows // len(pos)
    cos2 = np.tile(np.concatenate([cos, cos], axis=-1), (rep, 1))
    sin2 = np.tile(np.concatenate([-sin, sin], axis=-1), (rep, 1))
    return jnp.asarray(cos2.astype(np.float32)), jnp.asarray(sin2.astype(np.float32))


def _mix_weight_specs():
    return [_const_spec((1, D_MODEL)),
            _const_spec((D_MODEL, OFF_END)),
            _const_spec((1, D_MODEL)), _const_spec((1, D_MODEL)),
            _const_spec((GROUPS, CHUNK, CHUNK)), _const_spec((CHUNK, CHUNK)),
            _const_spec((CHUNK, GROUPS))]


def _mix_out_weight_specs():
    return [_const_spec((HEADS, CHUNK, CHUNK)),
            _const_spec((CHUNK, HEADS)), _const_spec((CHUNK, HEADS)),
            _const_spec((D_MODEL, D_MODEL)), _const_spec((RET_V, D_MODEL)), _const_spec((D_MODEL, D_MODEL))]


def _mix_prompt(x, mods, mod_row0, nm_g, w_in, ln_g, ln_b, gm_ws, gm_bs, w_a, w_b, w_o):
    nb, seq, _ = x.shape
    tm = MIX_ROWS
    dmask, qd, kd, s_decay, mmask = _retention_tables(seq)
    cos2, sin2 = _rotary_tables(np.arange(seq), seq)
    xspec = pl.BlockSpec((1, tm, D_MODEL), lambda b, j: (b, j, 0))
    tspec = pl.BlockSpec((tm, DK), lambda b, j: (j, 0))
    state_shape = (1, nb, HEADS, DK, DV)
    return pl.pallas_call(
        functools.partial(_mix_prompt_kernel, s_decay=s_decay),
        grid=(nb, seq // tm),
        in_specs=([xspec] + _mod_specs(1, mod_row0, 3) + _mix_weight_specs() + [tspec, tspec]
                  + _mix_out_weight_specs()),
        out_specs=[xspec, pl.BlockSpec((1, 1, HEADS, DK, DV), lambda b, j: (0, b, 0, 0, 0))],
        out_shape=[jax.ShapeDtypeStruct(x.shape, F32), jax.ShapeDtypeStruct(state_shape, F32)],
        scratch_shapes=[pltpu.VMEM((tm, D_MODEL), BF16), pltpu.VMEM((tm, RET_V), F32)],
        compiler_params=pltpu.CompilerParams(
            dimension_semantics=("arbitrary", "arbitrary"), vmem_limit_bytes=VMEM_LIMIT_V7X),
        name="mix_prompt",
    )(x, mods, mods, mods, nm_g.reshape(1, D_MODEL), w_in, ln_g.reshape(1, D_MODEL), ln_b.reshape(1, D_MODEL),
      gm_ws, mmask, jnp.transpose(gm_bs), cos2, sin2, dmask, qd, kd, w_a, w_b, w_o)


def _mix_sample(x, mods, mod_row0, nm_g, w_in, ln_g, ln_b, gm_ws, gm_bs, w_a, w_b, w_o, state):
    nb, ntok, _ = x.shape
    rows = MIX_ROWS
    nseq = rows // ntok
    nsub = rows // CHUNK
    rep = CHUNK // ntok
    dmask, qd, kd, s_decay, mmask = _retention_tables(ntok)
    cos2, sin2 = _rotary_tables(PAST_LEN + np.arange(ntok), rows)
    onehot = jnp.asarray(np.tile(np.eye(ntok, dtype=np.float32), (rep, 1)))
    wmix = jnp.einsum("ra,gab,cb->grc", onehot, gm_ws[:, :ntok, :ntok], onehot, precision=lax.Precision.HIGHEST)
    bst = jnp.einsum("ra,ga->rg", onehot, gm_bs[:, :ntok], precision=lax.Precision.HIGHEST)
    steps = nseq // STATE_SEQS
    xspec = pl.BlockSpec((nseq, ntok, D_MODEL), lambda i, j: (i, 0, 0))
    vspec = pl.BlockSpec((1, nseq, ntok, D_MODEL), lambda i, j: (0, i, 0, 0))
    sspec = pl.BlockSpec((1, STATE_SEQS, HEADS, DK, DV), lambda i, j: (0, i * steps + j, 0, 0, 0))
    return pl.pallas_call(
        functools.partial(_mix_sample_kernel, s_decay=s_decay),
        grid=(nb // nseq, steps),
        in_specs=([xspec] + _mod_specs(nseq, mod_row0, 3) + _mix_weight_specs()
                  + [_const_spec((rows, DK)), _const_spec((rows, DK))] + _mix_out_weight_specs() + [sspec]),
        out_specs=[xspec, sspec, vspec],
        out_shape=[jax.ShapeDtypeStruct(x.shape, F32), jax.ShapeDtypeStruct(state.shape, F32),
                   jax.ShapeDtypeStruct((1,) + x.shape, F32)],
        scratch_shapes=[pltpu.VMEM((rows, D_MODEL), BF16),
                        pltpu.VMEM((rows, RET_V), F32),
                        pltpu.VMEM((rows, D_MODEL), F32),
                        pltpu.VMEM((nsub, D_MODEL, CHUNK), F32),
                        pltpu.VMEM((nsub, CHUNK, RET_V), BF16),
                        pltpu.VMEM((rows, D_MODEL), BF16),
                        pltpu.VMEM((rows, D_MODEL), F32)],
        compiler_params=pltpu.CompilerParams(
            dimension_semantics=("arbitrary", "arbitrary"), vmem_limit_bytes=VMEM_LIMIT_V7X),
        name="mix_sample",
    )(x, mods, mods, mods, nm_g.reshape(1, D_MODEL), w_in, ln_g.reshape(1, D_MODEL), ln_b.reshape(1, D_MODEL),
      wmix, mmask, bst, cos2, sin2, dmask, qd, kd, w_a, w_b, w_o, state)


def kernel(x_prompt, x_sample, state_ret, c_prompt, c_sample, w_ada, b_ada, n1_g, w1_gate, w1_up, w1_down,
           nm_g, w_in, gm_ln_g, gm_ln_b, gm_ws, gm_bs, w_a, w_b, w_o, n2_g, w2_gate, w2_up, w2_down, final_g):
    assert w_ada.shape[0] == 1, "single-layer step"
    nbs = x_sample.shape[0]

    mods, w1 = _ada(jnp.concatenate([c_sample, c_prompt], axis=0), w_ada[0], b_ada[0],
                    cast=(w1_gate[0], w1_up[0], w1_down[0]))
    row_s, row_p = 0, nbs

    yp, later = _ffn(x_prompt, mods, row_p, 0, n1_g[0], *w1, final_g, final=False,
                     cast=(w_in[0], w_a[0], w_b[0], w_o[0], w2_gate[0], w2_up[0], w2_down[0]))
    w_in_b, w_a_b, w_b_b, w_o_b = later[:4]
    w2 = later[4:]
    wmix = (nm_g[0], w_in_b, gm_ln_g[0], gm_ln_b[0], gm_ws[0], gm_bs[0], w_a_b, w_b_b, w_o_b)
    yp, sp = _mix_prompt(yp, mods, row_p, *wmix)
    yp, _ = _ffn(yp, mods, row_p, 6, n2_g[0], *w2, final_g, final=True)

    ys, _ = _ffn(x_sample, mods, row_s, 0, n1_g[0], *w1, final_g, final=False)
    ys, ss, vs = _mix_sample(ys, mods, row_s, *wmix, state_ret.astype(F32))
    ys, _ = _ffn(ys, mods, row_s, 6, n2_g[0], *w2, final_g, final=True)

    return (yp, ys, sp, ss, vs)
```

```python
import functools

import jax
import jax.numpy as jnp
import numpy as np
from jax import lax
from jax.experimental import pallas as pl
from jax.experimental.pallas import tpu as pltpu

F32 = jnp.float32
BF16 = jnp.bfloat16

D_MODEL = 1024
D_FF = 2816
N_MOD = 9
EPS = 1e-6
ROPE_BASE = 10000.0
PAST_LEN = 16384
CHUNK = 128
GROUPS = 8
GROUP_DIM = D_MODEL // GROUPS
HEADS = 8
DK = D_MODEL // HEADS
DV = 2 * DK
RET_V = HEADS * DV
OFF_U, OFF_V, OFF_Q, OFF_K, OFF_RV, OFF_RG, OFF_GA, OFF_GB, OFF_END = (
    0, 1024, 2048, 3072, 4096, 6144, 8192, 9216, 10240)

VMEM_LIMIT_V7X = 56 * 1024 * 1024
FFN_ROWS = 512
FFN_SPLIT = 2
MIX_ROWS = 256
ADA_STEPS = 8
STATE_SEQS = 2
BF16_SUBLANES = 16


def _dot(a, b):
    return jnp.dot(a, b, preferred_element_type=F32)


def _silu(x):
    return x * jax.nn.sigmoid(x)


def _gelu_tanh(x):
    c = np.float32(np.sqrt(2.0 / np.pi))
    return 0.5 * x * (1.0 + jnp.tanh(c * (x + 0.044715 * (x * x * x))))


def _rms_norm(x, gain):
    return x * lax.rsqrt(jnp.mean(x * x, axis=-1, keepdims=True) + EPS) * gain


def _const_spec(shape):
    nd = len(shape)
    return pl.BlockSpec(shape, lambda *_: (0,) * nd, pipeline_mode=pl.Buffered(1))


def _mod_specs(bb, row0, first):
    assert row0 % bb == 0
    return [pl.BlockSpec((bb, 1, D_MODEL), lambda i, j, k=k: (row0 // bb + i, 0, k))
            for k in range(first, first + 3)]


def _cast_plan(weights, grid):
    steps = int(np.prod(grid))
    in_specs, out_specs, out_shapes = [], [], []
    for w in weights:
        rows, cols = w.shape
        nblk = max(n for n in range(1, steps + 1)
                   if steps % n == 0 and rows % n == 0 and (rows // n) % BF16_SUBLANES == 0)
        per_blk = steps // nblk

        def index(*ids, per_blk=per_blk):
            step = 0
            for extent, i in zip(grid, ids):
                step = step * extent + i
            return (step // per_blk, 0)

        spec = pl.BlockSpec((rows // nblk, cols), index)
        in_specs.append(spec)
        out_specs.append(spec)
        out_shapes.append(jax.ShapeDtypeStruct(w.shape, BF16))
    return in_specs, out_specs, out_shapes


def _cast_blocks(src_refs, dst_refs):
    for src, dst in zip(src_refs, dst_refs, strict=True):
        dst[...] = src[...].astype(BF16)


def _ada_kernel(c_ref, w_ref, b_ref, *refs):
    ncast = (len(refs) - 1) // 2
    o_ref = refs[ncast]
    s = _silu(c_ref[...]).astype(BF16)
    m = _dot(s, w_ref[...].astype(BF16)) + b_ref[...]
    for r in range(m.shape[0]):
        o_ref[r] = m[r:r + 1, :]
    _cast_blocks(refs[:ncast], refs[ncast + 1:])


def _ada(c_all, w_ada, b_ada, cast):
    rows = c_all.shape[0]
    n = w_ada.shape[1]
    grid = (ADA_STEPS,)
    bn = n // ADA_STEPS
    cast_in, cast_out, cast_shapes = _cast_plan(cast, grid)
    out = pl.pallas_call(
        _ada_kernel,
        grid=grid,
        in_specs=[pl.BlockSpec((rows, D_MODEL), lambda j: (0, 0)),
                  pl.BlockSpec((D_MODEL, bn), lambda j: (0, j)),
                  pl.BlockSpec((1, bn), lambda j: (0, j))] + cast_in,
        out_specs=[pl.BlockSpec((rows, 1, bn), lambda j: (0, 0, j))] + cast_out,
        out_shape=[jax.ShapeDtypeStruct((rows, 1, n), F32)] + cast_shapes,
        compiler_params=pltpu.CompilerParams(
            dimension_semantics=("arbitrary",), vmem_limit_bytes=VMEM_LIMIT_V7X),
        name="ada",
    )(c_all, w_ada, b_ada.reshape(1, n), *cast)
    return out[0], out[1:]


def _ffn_kernel(x_ref, sh_ref, sc_ref, g_ref, ng_ref, wg_ref, wu_ref, wd_ref, fg_ref, *refs, final):
    ncast = (len(refs) - 1) // 2
    o_ref = refs[ncast]
    nb, nr, _ = x_ref.shape
    for s in range(FFN_SPLIT):
        if nb > 1:
            seqs, toks = slice(s * nb // FFN_SPLIT, (s + 1) * nb // FFN_SPLIT), slice(None)
        else:
            seqs, toks = slice(None), slice(s * nr // FFN_SPLIT, (s + 1) * nr // FFN_SPLIT)
        x = x_ref[seqs, toks, :]
        bb, r, _ = x.shape
        h = _rms_norm(x, ng_ref[...]) * (1.0 + sc_ref[seqs]) + sh_ref[seqs]
        h = h.reshape(bb * r, D_MODEL).astype(BF16)
        act = (_silu(_dot(h, wg_ref[...])) * _dot(h, wu_ref[...])).astype(BF16)
        y = _dot(act, wd_ref[...]).reshape(bb, r, D_MODEL)
        out = x + (0.5 * g_ref[seqs]) * y
        if final:
            out = _rms_norm(out, fg_ref[...])
        o_ref[seqs, toks, :] = out
    _cast_blocks(refs[:ncast], refs[ncast + 1:])


def _ffn(x, mods, mod_row0, mod_first, norm_g, wg, wu, wd, final_g, *, final, cast=()):
    nb, r, _ = x.shape
    br = min(r, FFN_ROWS)
    bb = FFN_ROWS // br
    grid = (nb // bb, r // br)
    xspec = pl.BlockSpec((bb, br, D_MODEL), lambda i, j: (i, j, 0))
    cast_in, cast_out, cast_shapes = _cast_plan(cast, grid)
    out = pl.pallas_call(
        functools.partial(_ffn_kernel, final=final),
        grid=grid,
        in_specs=[xspec] + _mod_specs(bb, mod_row0, mod_first) + [
            _const_spec((1, D_MODEL)),
            _const_spec((D_MODEL, D_FF)), _const_spec((D_MODEL, D_FF)), _const_spec((D_FF, D_MODEL)),
            _const_spec((1, D_MODEL))] + cast_in,
        out_specs=[xspec] + cast_out,
        out_shape=[jax.ShapeDtypeStruct(x.shape, F32)] + cast_shapes,
        compiler_params=pltpu.CompilerParams(
            dimension_semantics=("arbitrary", "arbitrary"), vmem_limit_bytes=VMEM_LIMIT_V7X),
        name="ffn_final" if final else "ffn",
    )(x, mods, mods, mods, norm_g.reshape(1, D_MODEL), wg, wu, wd, final_g.reshape(1, D_MODEL), *cast)
    return out[0], out[1:]


def _modulated_norm(x, ng_ref, sc_ref, sh_ref):
    return _rms_norm(x, ng_ref[...]) * (1.0 + sc_ref[...]) + sh_ref[...]


def _gmlp_branch(h, win_ref, lng_ref, lnb_ref, wmix_ref, mmask_ref, bst_ref, wa_ref, oa_scr):
    rows = h.shape[0]
    u = _gelu_tanh(_dot(h, win_ref[:, OFF_U:OFF_V]))
    gv = _gelu_tanh(_dot(h, win_ref[:, OFF_V:OFF_Q]))
    mu = jnp.mean(gv, axis=-1, keepdims=True)
    var = jnp.mean(jnp.square(gv - mu), axis=-1, keepdims=True)
    v = (gv - mu) * lax.rsqrt(var + EPS) * lng_ref[...] + lnb_ref[...]
    vb = v.astype(BF16)
    mmask = mmask_ref[...]
    for g in range(GROUPS):
        wg = (wmix_ref[g] * mmask).astype(BF16)
        cols = slice(g * GROUP_DIM, (g + 1) * GROUP_DIM)
        for c in range(rows // CHUNK):
            rws = slice(c * CHUNK, (c + 1) * CHUNK)
            mix = _dot(wg, vb[rws, cols]) + bst_ref[:, g:g + 1]
            oa_scr[rws, cols] = (u[rws, cols] * mix).astype(BF16)
    pa = _dot(oa_scr[...], wa_ref[...])
    ga = _dot(h, win_ref[:, OFF_GA:OFF_GB])
    return jax.nn.sigmoid(ga) * pa, v


def _rotary_heads(z, cos2, sin2):
    out = []
    for hd in range(HEADS):
        zh = z[:, hd * DK:(hd + 1) * DK]
        out.append(zh * cos2 + pltpu.roll(zh, DK // 2, 1) * sin2)
    return out


def _intra(qc, kc, vc, dmask):
    scores = lax.dot_general(qc, kc, (((1,), (1,)), ((), ())), preferred_element_type=F32) * dmask
    return _dot(scores.astype(BF16), vc)


def _head_rms(o):
    return o * lax.rsqrt(jnp.mean(o * o, axis=-1, keepdims=True) + EPS)


def _mix_tail(h, merged, ob_scr, win_ref, wb_ref, wo_ref):
    ob = (ob_scr[...] * _silu(_dot(h, win_ref[:, OFF_RG:OFF_GA]))).astype(BF16)
    pb = _dot(ob, wb_ref[...])
    gb = _dot(h, win_ref[:, OFF_GB:OFF_END])
    merged = merged + jax.nn.sigmoid(gb) * pb
    return _dot(merged.astype(BF16), wo_ref[...])


def _mix_prompt_kernel(x_ref, sh_ref, sc_ref, g_ref, ng_ref, win_ref, lng_ref, lnb_ref, wmix_ref, mmask_ref,
                       bst_ref, cos_ref, sin_ref, dmask_ref, qd_ref, kd_ref, wa_ref, wb_ref, wo_ref,
                       o_ref, s_ref, oa_scr, ob_scr, *, s_decay):
    @pl.when(pl.program_id(1) == 0)
    def _():
        s_ref[...] = jnp.zeros(s_ref.shape, F32)

    x = x_ref[0]
    tm = x.shape[0]
    h = _modulated_norm(x, ng_ref, sc_ref.at[0], sh_ref.at[0]).astype(BF16)
    merged, _ = _gmlp_branch(h, win_ref, lng_ref, lnb_ref, wmix_ref, mmask_ref, bst_ref, wa_ref, oa_scr)

    cos2, sin2 = cos_ref[...], sin_ref[...]
    qs = _rotary_heads(_dot(h, win_ref[:, OFF_Q:OFF_K]), cos2, sin2)
    ks = _rotary_heads(_dot(h, win_ref[:, OFF_K:OFF_RV]), cos2, sin2)
    rv = _dot(h, win_ref[:, OFF_RV:OFF_RG]).astype(BF16)
    kscale = np.float32(DK ** -0.5)
    for hd in range(HEADS):
        q = qs[hd].astype(BF16)
        k = ks[hd] * kscale
        kb = k.astype(BF16)
        for c in range(tm // CHUNK):
            rws = slice(c * CHUNK, (c + 1) * CHUNK)
            qc, vc = q[rws], rv[rws, hd * DV:(hd + 1) * DV]
            state = s_ref[0, 0, hd]
            inter = _dot(qc, state.astype(BF16)) * qd_ref[:, hd:hd + 1]
            o = _intra(qc, kb[rws], vc, dmask_ref[hd]) + inter
            ob_scr[rws, hd * DV:(hd + 1) * DV] = _head_rms(o)
            kd = k[rws] * kd_ref[:, hd:hd + 1]
            upd = _dot(jnp.transpose(kd).astype(BF16), vc)
            s_ref[0, 0, hd] = s_decay[hd] * state + upd
    o_ref[0] = x + g_ref[0] * _mix_tail(h, merged, ob_scr, win_ref, wb_ref, wo_ref)


def _mix_sample_kernel(x_ref, sh_ref, sc_ref, g_ref, ng_ref, win_ref, lng_ref, lnb_ref, wmix_ref, mmask_ref,
                       bst_ref, cos_ref, sin_ref, dmask_ref, qd_ref, kd_ref, wa_ref, wb_ref, wo_ref, s0_ref,
                       o_ref, s_ref, vrow_ref, oa_scr, ob_scr, q_scr, kt_scr, rv_scr, h_scr, mg_scr,
                       *, s_decay):
    j = pl.program_id(1)
    nseq, ntok, _ = x_ref.shape
    rows = nseq * ntok
    seq_per_chunk = CHUNK // ntok

    @pl.when(j == 0)
    def _():
        h = _modulated_norm(x_ref[...], ng_ref, sc_ref, sh_ref).reshape(rows, D_MODEL).astype(BF16)
        h_scr[...] = h
        merged, v = _gmlp_branch(h, win_ref, lng_ref, lnb_ref, wmix_ref, mmask_ref, bst_ref, wa_ref, oa_scr)
        mg_scr[...] = merged
        vrow_ref[0] = v.reshape(nseq, ntok, D_MODEL)
        cos2, sin2 = cos_ref[...], sin_ref[...]
        qs = _rotary_heads(_dot(h, win_ref[:, OFF_Q:OFF_K]), cos2, sin2)
        ks = _rotary_heads(_dot(h, win_ref[:, OFF_K:OFF_RV]), cos2, sin2)
        rv = _dot(h, win_ref[:, OFF_RV:OFF_RG]).astype(BF16)
        kscale = np.float32(DK ** -0.5)
        for c in range(rows // CHUNK):
            rws = slice(c * CHUNK, (c + 1) * CHUNK)
            rv_scr[c] = rv[rws]
            for hd in range(HEADS):
                k = ks[hd][rws] * kscale
                kt_scr[c, hd * DK:(hd + 1) * DK, :] = jnp.transpose(k * kd_ref[:, hd:hd + 1])
                ob_scr[rws, hd * DV:(hd + 1) * DV] = _intra(
                    qs[hd][rws].astype(BF16), k.astype(BF16), rv[rws, hd * DV:(hd + 1) * DV], dmask_ref[hd])
        for hd in range(HEADS):
            q_scr[:, hd * DK:(hd + 1) * DK] = qs[hd]

    nstep = s0_ref.shape[1]
    for t in range(nstep):
        s = j * nstep + t
        tok = pl.ds(pl.multiple_of(s * ntok, ntok), ntok)
        c = s // seq_per_chunk
        in_seq = lax.broadcasted_iota(jnp.int32, (DK, CHUNK), 1) // ntok == s % seq_per_chunk
        for hd in range(HEADS):
            state = s0_ref[0, t, hd]
            qj = q_scr[tok, hd * DK:(hd + 1) * DK].astype(BF16)
            inter = _dot(qj, state.astype(BF16)) * qd_ref[0:ntok, hd:hd + 1]
            ob_scr[tok, hd * DV:(hd + 1) * DV] += inter
            kt = jnp.where(in_seq, kt_scr[c, hd * DK:(hd + 1) * DK, :], 0.0).astype(BF16)
            upd = _dot(kt, rv_scr[c, :, hd * DV:(hd + 1) * DV])
            s_ref[0, t, hd] = s_decay[hd] * state + upd

    @pl.when(j == pl.num_programs(1) - 1)
    def _():
        for hd in range(HEADS):
            cols = slice(hd * DV, (hd + 1) * DV)
            ob_scr[:, cols] = _head_rms(ob_scr[:, cols])
        out = _mix_tail(h_scr[...], mg_scr[...], ob_scr, win_ref, wb_ref, wo_ref)
        o_ref[...] = x_ref[...] + g_ref[...] * out.reshape(nseq, ntok, D_MODEL)


def _retention_tables(seq_len):
    lc = min(seq_len, CHUNK)
    log_gamma = np.log1p(-np.power(2.0, -5.0 - np.arange(HEADS)))
    idx = np.arange(lc, dtype=np.float64)
    diff = idx[:, None] - idx[None, :]
    decay = np.where(diff >= 0, np.exp(np.maximum(diff, 0.0)[None] * log_gamma[:, None, None]), 0.0)
    q_decay = np.exp((idx + 1.0)[:, None] * log_gamma[None, :])
    k_decay = np.exp((lc - 1.0 - idx)[:, None] * log_gamma[None, :])
    s_decay = tuple(float(v) for v in np.exp(lc * log_gamma).astype(np.float32))
    rep = CHUNK // lc
    blockdiag = np.kron(np.eye(rep), np.ones((lc, lc)))
    dmask = np.tile(decay, (1, rep, rep)) * blockdiag[None]
    mmask = np.tile(np.tril(np.ones((lc, lc))), (rep, rep)) * blockdiag
    f32 = lambda a: jnp.asarray(a.astype(np.float32))
    return f32(dmask), f32(np.tile(q_decay, (rep, 1))), f32(np.tile(k_decay, (rep, 1))), s_decay, f32(mmask)


def _rotary_tables(pos, rows):
    half = DK // 2
    inv = ROPE_BASE ** (-np.arange(half, dtype=np.float64) / half)
    ang = np.asarray(pos, np.float64)[:, None] * inv[None, :]
    cos, sin = np.cos(ang), np.sin(ang)
    rep = rows // len(pos)
    cos2 = np.tile(np.concatenate([cos, cos], axis=-1), (rep, 1))
    sin2 = np.tile(np.concatenate([-sin, sin], axis=-1), (rep, 1))
    return jnp.asarray(cos2.astype(np.float32)), jnp.asarray(sin2.astype(np.float32))


def _mix_weight_specs():
    return [_const_spec((1, D_MODEL)),
            _const_spec((D_MODEL, OFF_END)),
            _const_spec((1, D_MODEL)), _const_spec((1, D_MODEL)),
            _const_spec((GROUPS, CHUNK, CHUNK)), _const_spec((CHUNK, CHUNK)),
            _const_spec((CHUNK, GROUPS))]


def _mix_out_weight_specs():
    return [_const_spec((HEADS, CHUNK, CHUNK)),
            _const_spec((CHUNK, HEADS)), _const_spec((CHUNK, HEADS)),
            _const_spec((D_MODEL, D_MODEL)), _const_spec((RET_V, D_MODEL)), _const_spec((D_MODEL, D_MODEL))]


def _mix_prompt(x, mods, mod_row0, nm_g, w_in, ln_g, ln_b, gm_ws, gm_bs, w_a, w_b, w_o):
    nb, seq, _ = x.shape
    tm = MIX_ROWS
    dmask, qd, kd, s_decay, mmask = _retention_tables(seq)
    cos2, sin2 = _rotary_tables(np.arange(seq), seq)
    xspec = pl.BlockSpec((1, tm, D_MODEL), lambda b, j: (b, j, 0))
    tspec = pl.BlockSpec((tm, DK), lambda b, j: (j, 0))
    state_shape = (1, nb, HEADS, DK, DV)
    return pl.pallas_call(
        functools.partial(_mix_prompt_kernel, s_decay=s_decay),
        grid=(nb, seq // tm),
        in_specs=([xspec] + _mod_specs(1, mod_row0, 3) + _mix_weight_specs() + [tspec, tspec]
                  + _mix_out_weight_specs()),
        out_specs=[xspec, pl.BlockSpec((1, 1, HEADS, DK, DV), lambda b, j: (0, b, 0, 0, 0))],
        out_shape=[jax.ShapeDtypeStruct(x.shape, F32), jax.ShapeDtypeStruct(state_shape, F32)],
        scratch_shapes=[pltpu.VMEM((tm, D_MODEL), BF16), pltpu.VMEM((tm, RET_V), F32)],
        compiler_params=pltpu.CompilerParams(
            dimension_semantics=("arbitrary", "arbitrary"), vmem_limit_bytes=VMEM_LIMIT_V7X),
        name="mix_prompt",
    )(x, mods, mods, mods, nm_g.reshape(1, D_MODEL), w_in, ln_g.reshape(1, D_MODEL), ln_b.reshape(1, D_MODEL),
      gm_ws, mmask, jnp.transpose(gm_bs), cos2, sin2, dmask, qd, kd, w_a, w_b, w_o)


def _mix_sample(x, mods, mod_row0, nm_g, w_in, ln_g, ln_b, gm_ws, gm_bs, w_a, w_b, w_o, state):
    nb, ntok, _ = x.shape
    rows = MIX_ROWS
    nseq = rows // ntok
    nsub = rows // CHUNK
    rep = CHUNK // ntok
    dmask, qd, kd, s_decay, mmask = _retention_tables(ntok)
    cos2, sin2 = _rotary_tables(PAST_LEN + np.arange(ntok), rows)
    onehot = jnp.asarray(np.tile(np.eye(ntok, dtype=np.float32), (rep, 1)))
    wmix = jnp.einsum("ra,gab,cb->grc", onehot, gm_ws[:, :ntok, :ntok], onehot, precision=lax.Precision.HIGHEST)
    bst = jnp.einsum("ra,ga->rg", onehot, gm_bs[:, :ntok], precision=lax.Precision.HIGHEST)
    steps = nseq // STATE_SEQS
    xspec = pl.BlockSpec((nseq, ntok, D_MODEL), lambda i, j: (i, 0, 0))
    vspec = pl.BlockSpec((1, nseq, ntok, D_MODEL), lambda i, j: (0, i, 0, 0))
    sspec = pl.BlockSpec((1, STATE_SEQS, HEADS, DK, DV), lambda i, j: (0, i * steps + j, 0, 0, 0))
    return pl.pallas_call(
        functools.partial(_mix_sample_kernel, s_decay=s_decay),
        grid=(nb // nseq, steps),
        in_specs=([xspec] + _mod_specs(nseq, mod_row0, 3) + _mix_weight_specs()
                  + [_const_spec((rows, DK)), _const_spec((rows, DK))] + _mix_out_weight_specs() + [sspec]),
        out_specs=[xspec, sspec, vspec],
        out_shape=[jax.ShapeDtypeStruct(x.shape, F32), jax.ShapeDtypeStruct(state.shape, F32),
                   jax.ShapeDtypeStruct((1,) + x.shape, F32)],
        scratch_shapes=[pltpu.VMEM((rows, D_MODEL), BF16),
                        pltpu.VMEM((rows, RET_V), F32),
                        pltpu.VMEM((rows, D_MODEL), F32),
                        pltpu.VMEM((nsub, D_MODEL, CHUNK), F32),
                        pltpu.VMEM((nsub, CHUNK, RET_V), BF16),
                        pltpu.VMEM((rows, D_MODEL), BF16),
                        pltpu.VMEM((rows, D_MODEL), F32)],
        compiler_params=pltpu.CompilerParams(
            dimension_semantics=("arbitrary", "arbitrary"), vmem_limit_bytes=VMEM_LIMIT_V7X),
        name="mix_sample",
    )(x, mods, mods, mods, nm_g.reshape(1, D_MODEL), w_in, ln_g.reshape(1, D_MODEL), ln_b.reshape(1, D_MODEL),
      wmix, mmask, bst, cos2, sin2, dmask, qd, kd, w_a, w_b, w_o, state)


def kernel(x_prompt, x_sample, state_ret, c_prompt, c_sample, w_ada, b_ada, n1_g, w1_gate, w1_up, w1_down,
           nm_g, w_in, gm_ln_g, gm_ln_b, gm_ws, gm_bs, w_a, w_b, w_o, n2_g, w2_gate, w2_up, w2_down, final_g):
    assert w_ada.shape[0] == 1, "single-layer step"
    nbs = x_sample.shape[0]

    mods, w1 = _ada(jnp.concatenate([c_sample, c_prompt], axis=0), w_ada[0], b_ada[0],
                    cast=(w1_gate[0], w1_up[0], w1_down[0]))
    row_s, row_p = 0, nbs

    yp, later = _ffn(x_prompt, mods, row_p, 0, n1_g[0], *w1, final_g, final=False,
                     cast=(w_in[0], w_a[0], w_b[0], w_o[0], w2_gate[0], w2_up[0], w2_down[0]))
    w_in_b, w_a_b, w_b_b, w_o_b = later[:4]
    w2 = later[4:]
    wmix = (nm_g[0], w_in_b, gm_ln_g[0], gm_ln_b[0], gm_ws[0], gm_bs[0], w_a_b, w_b_b, w_o_b)
    yp, sp = _mix_prompt(yp, mods, row_p, *wmix)
    yp, _ = _ffn(yp, mods, row_p, 6, n2_g[0], *w2, final_g, final=True)

    ys, _ = _ffn(x_sample, mods, row_s, 0, n1_g[0], *w1, final_g, final=False)
    ys, ss, vs = _mix_sample(ys, mods, row_s, *wmix, state_ret.astype(F32))
    ys, _ = _ffn(ys, mods, row_s, 6, n2_g[0], *w2, final_g, final=True)

    return (yp, ys, sp, ss, vs)
```

```python
import functools

import jax
import jax.numpy as jnp
import numpy as np
from jax import lax
from jax.experimental import pallas as pl
from jax.experimental.pallas import tpu as pltpu

F32 = jnp.float32
BF16 = jnp.bfloat16

D_MODEL = 1024
D_FF = 2816
N_MOD = 9
EPS = 1e-6
ROPE_BASE = 10000.0
PAST_LEN = 16384
CHUNK = 128
GROUPS = 8
GROUP_DIM = D_MODEL // GROUPS
HEADS = 8
DK = D_MODEL // HEADS
DV = 2 * DK
RET_V = HEADS * DV
OFF_U, OFF_V, OFF_Q, OFF_K, OFF_RV, OFF_RG, OFF_GA, OFF_GB, OFF_END = (
    0, 1024, 2048, 3072, 4096, 6144, 8192, 9216, 10240)

VMEM_LIMIT_V7X = 56 * 1024 * 1024
FFN_ROWS = 512
FFN_SPLIT = 2
MIX_ROWS = 256
ADA_STEPS = 8
BF16_SUBLANES = 16


def _dot(a, b):
    return jnp.dot(a, b, preferred_element_type=F32)


def _silu(x):
    return x * jax.nn.sigmoid(x)


def _gelu_tanh(x):
    c = np.float32(np.sqrt(2.0 / np.pi))
    return 0.5 * x * (1.0 + jnp.tanh(c * (x + 0.044715 * (x * x * x))))


def _rms_norm(x, gain):
    return x * lax.rsqrt(jnp.mean(x * x, axis=-1, keepdims=True) + EPS) * gain


def _const_spec(shape):
    nd = len(shape)
    return pl.BlockSpec(shape, lambda *_: (0,) * nd, pipeline_mode=pl.Buffered(1))


def _mod_specs(bb, row0, first):
    assert row0 % bb == 0
    return [pl.BlockSpec((bb, 1, D_MODEL), lambda i, *_, k=k: (row0 // bb + i, 0, k))
            for k in range(first, first + 3)]


def _cast_plan(weights, grid):
    steps = int(np.prod(grid))
    in_specs, out_specs, out_shapes = [], [], []
    for w in weights:
        rows, cols = w.shape
        nblk = max(n for n in range(1, steps + 1)
                   if steps % n == 0 and rows % n == 0 and (rows // n) % BF16_SUBLANES == 0)
        per_blk = steps // nblk

        def index(*ids, per_blk=per_blk):
            step = 0
            for extent, i in zip(grid, ids):
                step = step * extent + i
            return (step // per_blk, 0)

        spec = pl.BlockSpec((rows // nblk, cols), index)
        in_specs.append(spec)
        out_specs.append(spec)
        out_shapes.append(jax.ShapeDtypeStruct(w.shape, BF16))
    return in_specs, out_specs, out_shapes


def _cast_blocks(src_refs, dst_refs):
    for src, dst in zip(src_refs, dst_refs, strict=True):
        dst[...] = src[...].astype(BF16)


def _ada_kernel(c_ref, w_ref, b_ref, *refs):
    ncast = (len(refs) - 1) // 2
    o_ref = refs[ncast]
    s = _silu(c_ref[...]).astype(BF16)
    m = _dot(s, w_ref[...].astype(BF16)) + b_ref[...]
    for r in range(m.shape[0]):
        o_ref[r] = m[r:r + 1, :]
    _cast_blocks(refs[:ncast], refs[ncast + 1:])


def _ada(c_all, w_ada, b_ada, cast):
    rows = c_all.shape[0]
    n = w_ada.shape[1]
    grid = (ADA_STEPS,)
    bn = n // ADA_STEPS
    cast_in, cast_out, cast_shapes = _cast_plan(cast, grid)
    out = pl.pallas_call(
        _ada_kernel,
        grid=grid,
        in_specs=[pl.BlockSpec((rows, D_MODEL), lambda j: (0, 0)),
                  pl.BlockSpec((D_MODEL, bn), lambda j: (0, j)),
                  pl.BlockSpec((1, bn), lambda j: (0, j))] + cast_in,
        out_specs=[pl.BlockSpec((rows, 1, bn), lambda j: (0, 0, j))] + cast_out,
        out_shape=[jax.ShapeDtypeStruct((rows, 1, n), F32)] + cast_shapes,
        compiler_params=pltpu.CompilerParams(
            dimension_semantics=("arbitrary",), vmem_limit_bytes=VMEM_LIMIT_V7X),
        name="ada",
    )(c_all, w_ada, b_ada.reshape(1, n), *cast)
    return out[0], out[1:]


def _ffn_kernel(x_ref, sh_ref, sc_ref, g_ref, ng_ref, wg_ref, wu_ref, wd_ref, fg_ref, *refs, final):
    ncast = (len(refs) - 1) // 2
    o_ref = refs[ncast]
    nb, nr, _ = x_ref.shape
    for s in range(FFN_SPLIT):
        if nb > 1:
            seqs, toks = slice(s * nb // FFN_SPLIT, (s + 1) * nb // FFN_SPLIT), slice(None)
        else:
            seqs, toks = slice(None), slice(s * nr // FFN_SPLIT, (s + 1) * nr // FFN_SPLIT)
        x = x_ref[seqs, toks, :]
        bb, r, _ = x.shape
        h = _rms_norm(x, ng_ref[...]) * (1.0 + sc_ref[seqs]) + sh_ref[seqs]
        h = h.reshape(bb * r, D_MODEL).astype(BF16)
        act = (_silu(_dot(h, wg_ref[...])) * _dot(h, wu_ref[...])).astype(BF16)
        y = _dot(act, wd_ref[...]).reshape(bb, r, D_MODEL)
        out = x + (0.5 * g_ref[seqs]) * y
        if final:
            out = _rms_norm(out, fg_ref[...])
        o_ref[seqs, toks, :] = out
    _cast_blocks(refs[:ncast], refs[ncast + 1:])


def _ffn(x, mods, mod_row0, mod_first, norm_g, wg, wu, wd, final_g, *, final, cast=()):
    nb, r, _ = x.shape
    br = min(r, FFN_ROWS)
    bb = FFN_ROWS // br
    grid = (nb // bb, r // br)
    xspec = pl.BlockSpec((bb, br, D_MODEL), lambda i, j: (i, j, 0))
    cast_in, cast_out, cast_shapes = _cast_plan(cast, grid)
    out = pl.pallas_call(
        functools.partial(_ffn_kernel, final=final),
        grid=grid,
        in_specs=[xspec] + _mod_specs(bb, mod_row0, mod_first) + [
            _const_spec((1, D_MODEL)),
            _const_spec((D_MODEL, D_FF)), _const_spec((D_MODEL, D_FF)), _const_spec((D_FF, D_MODEL)),
            _const_spec((1, D_MODEL))] + cast_in,
        out_specs=[xspec] + cast_out,
        out_shape=[jax.ShapeDtypeStruct(x.shape, F32)] + cast_shapes,
        compiler_params=pltpu.CompilerParams(
            dimension_semantics=("arbitrary", "arbitrary"), vmem_limit_bytes=VMEM_LIMIT_V7X),
        name="ffn_final" if final else "ffn",
    )(x, mods, mods, mods, norm_g.reshape(1, D_MODEL), wg, wu, wd, final_g.reshape(1, D_MODEL), *cast)
    return out[0], out[1:]


def _modulated_norm(x, ng_ref, sc_ref, sh_ref):
    return _rms_norm(x, ng_ref[...]) * (1.0 + sc_ref[...]) + sh_ref[...]


def _gmlp_branch(h, win_ref, lng_ref, lnb_ref, wmix_ref, mmask_ref, bst_ref, wa_ref, oa_scr):
    rows = h.shape[0]
    u = _gelu_tanh(_dot(h, win_ref[:, OFF_U:OFF_V]))
    gv = _gelu_tanh(_dot(h, win_ref[:, OFF_V:OFF_Q]))
    mu = jnp.mean(gv, axis=-1, keepdims=True)
    var = jnp.mean(jnp.square(gv - mu), axis=-1, keepdims=True)
    v = (gv - mu) * lax.rsqrt(var + EPS) * lng_ref[...] + lnb_ref[...]
    vb = v.astype(BF16)
    mmask = mmask_ref[...]
    for g in range(GROUPS):
        wg = (wmix_ref[g] * mmask).astype(BF16)
        cols = slice(g * GROUP_DIM, (g + 1) * GROUP_DIM)
        for c in range(rows // CHUNK):
            rws = slice(c * CHUNK, (c + 1) * CHUNK)
            mix = _dot(wg, vb[rws, cols]) + bst_ref[:, g:g + 1]
            oa_scr[rws, cols] = (u[rws, cols] * mix).astype(BF16)
    pa = _dot(oa_scr[...], wa_ref[...])
    ga = _dot(h, win_ref[:, OFF_GA:OFF_GB])
    return jax.nn.sigmoid(ga) * pa, v


def _rotary_heads(z, cos2, sin2):
    out = []
    for hd in range(HEADS):
        zh = z[:, hd * DK:(hd + 1) * DK]
        out.append(zh * cos2 + pltpu.roll(zh, DK // 2, 1) * sin2)
    return out


def _intra(qc, kc, vc, dmask):
    scores = lax.dot_general(qc, kc, (((1,), (1,)), ((), ())), preferred_element_type=F32) * dmask
    return _dot(scores.astype(BF16), vc)


def _head_rms(o):
    return o * lax.rsqrt(jnp.mean(o * o, axis=-1, keepdims=True) + EPS)


def _mix_tail(h, merged, ob_scr, win_ref, wb_ref, wo_ref):
    ob = (ob_scr[...] * _silu(_dot(h, win_ref[:, OFF_RG:OFF_GA]))).astype(BF16)
    pb = _dot(ob, wb_ref[...])
    gb = _dot(h, win_ref[:, OFF_GB:OFF_END])
    merged = merged + jax.nn.sigmoid(gb) * pb
    return _dot(merged.astype(BF16), wo_ref[...])


def _state_step(q_ref, kt_ref, rv_ref, qd_ref, s0_ref, inter_ref, snew_ref, first_seq, s_decay):
    nstep = s0_ref.shape[1]
    ntok = qd_ref.shape[0]
    seq_per_chunk = CHUNK // ntok
    for t in range(nstep):
        tok = slice(t * ntok, (t + 1) * ntok)
        in_seq = lax.broadcasted_iota(jnp.int32, (DK, CHUNK), 1) // ntok == (first_seq + t) % seq_per_chunk
        for hd in range(HEADS):
            state = s0_ref[0, t, hd]
            qj = q_ref[tok, hd * DK:(hd + 1) * DK].astype(BF16)
            inter_ref[tok, hd * DV:(hd + 1) * DV] = _dot(qj, state.astype(BF16)) * qd_ref[:, hd:hd + 1]
            kt = jnp.where(in_seq, kt_ref[0, hd * DK:(hd + 1) * DK, :], 0.0).astype(BF16)
            upd = _dot(kt, rv_ref[0, :, hd * DV:(hd + 1) * DV])
            snew_ref[0, t, hd] = s_decay[hd] * state + upd


N_MIX_IN = 19
N_JOB_IN = 5


def _mix_prompt_kernel(*refs, s_decay, job_decay):
    (x_ref, sh_ref, sc_ref, g_ref, ng_ref, win_ref, lng_ref, lnb_ref, wmix_ref, mmask_ref,
     bst_ref, cos_ref, sin_ref, dmask_ref, qd_ref, kd_ref, wa_ref, wb_ref, wo_ref) = refs[:N_MIX_IN]
    rest = refs[N_MIX_IN:]
    if job_decay is not None:
        jq_ref, jkt_ref, jrv_ref, jqd_ref, js0_ref = rest[:N_JOB_IN]
        o_ref, s_ref, jinter_ref, jsnew_ref, oa_scr, ob_scr = rest[N_JOB_IN:]
        step = pl.program_id(0) * pl.num_programs(1) + pl.program_id(1)
        _state_step(jq_ref, jkt_ref, jrv_ref, jqd_ref, js0_ref, jinter_ref, jsnew_ref,
                    step * js0_ref.shape[1], job_decay)
    else:
        o_ref, s_ref, oa_scr, ob_scr = rest

    @pl.when(pl.program_id(1) == 0)
    def _():
        s_ref[...] = jnp.zeros(s_ref.shape, F32)

    x = x_ref[0]
    tm = x.shape[0]
    h = _modulated_norm(x, ng_ref, sc_ref.at[0], sh_ref.at[0]).astype(BF16)
    merged, _ = _gmlp_branch(h, win_ref, lng_ref, lnb_ref, wmix_ref, mmask_ref, bst_ref, wa_ref, oa_scr)

    cos2, sin2 = cos_ref[...], sin_ref[...]
    qs = _rotary_heads(_dot(h, win_ref[:, OFF_Q:OFF_K]), cos2, sin2)
    ks = _rotary_heads(_dot(h, win_ref[:, OFF_K:OFF_RV]), cos2, sin2)
    rv = _dot(h, win_ref[:, OFF_RV:OFF_RG]).astype(BF16)
    kscale = np.float32(DK ** -0.5)
    for hd in range(HEADS):
        q = qs[hd].astype(BF16)
        k = ks[hd] * kscale
        kb = k.astype(BF16)
        for c in range(tm // CHUNK):
            rws = slice(c * CHUNK, (c + 1) * CHUNK)
            qc, vc = q[rws], rv[rws, hd * DV:(hd + 1) * DV]
            state = s_ref[0, 0, hd]
            inter = _dot(qc, state.astype(BF16)) * qd_ref[:, hd:hd + 1]
            o = _intra(qc, kb[rws], vc, dmask_ref[hd]) + inter
            ob_scr[rws, hd * DV:(hd + 1) * DV] = _head_rms(o)
            kd = k[rws] * kd_ref[:, hd:hd + 1]
            upd = _dot(jnp.transpose(kd).astype(BF16), vc)
            s_ref[0, 0, hd] = s_decay[hd] * state + upd
    o_ref[0] = x + g_ref[0] * _mix_tail(h, merged, ob_scr, win_ref, wb_ref, wo_ref)


def _mix_front_kernel(x_ref, sh_ref, sc_ref, ng_ref, win_ref, lng_ref, lnb_ref, wmix_ref, mmask_ref, bst_ref,
                      cos_ref, sin_ref, dmask_ref, kd_ref, wa_ref,
                      vrow_ref, mg_ref, h_ref, q_ref, kt_ref, rv_ref, ob_ref, oa_scr):
    nseq, ntok, _ = x_ref.shape
    rows = nseq * ntok
    h = _modulated_norm(x_ref[...], ng_ref, sc_ref, sh_ref).reshape(rows, D_MODEL).astype(BF16)
    h_ref[...] = h
    merged, v = _gmlp_branch(h, win_ref, lng_ref, lnb_ref, wmix_ref, mmask_ref, bst_ref, wa_ref, oa_scr)
    mg_ref[...] = merged
    vrow_ref[0] = v.reshape(nseq, ntok, D_MODEL)
    cos2, sin2 = cos_ref[...], sin_ref[...]
    qs = _rotary_heads(_dot(h, win_ref[:, OFF_Q:OFF_K]), cos2, sin2)
    ks = _rotary_heads(_dot(h, win_ref[:, OFF_K:OFF_RV]), cos2, sin2)
    rv = _dot(h, win_ref[:, OFF_RV:OFF_RG]).astype(BF16)
    kscale = np.float32(DK ** -0.5)
    for c in range(rows // CHUNK):
        rws = slice(c * CHUNK, (c + 1) * CHUNK)
        rv_ref[c] = rv[rws]
        for hd in range(HEADS):
            k = ks[hd][rws] * kscale
            kt_ref[c, hd * DK:(hd + 1) * DK, :] = jnp.transpose(k * kd_ref[:, hd:hd + 1])
            ob_ref[rws, hd * DV:(hd + 1) * DV] = _intra(
                qs[hd][rws].astype(BF16), k.astype(BF16), rv[rws, hd * DV:(hd + 1) * DV], dmask_ref[hd])
    for hd in range(HEADS):
        q_ref[:, hd * DK:(hd + 1) * DK] = qs[hd]


def _mix_back_kernel(x_ref, g_ref, h_ref, mg_ref, ob_ref, inter_ref, win_ref, wb_ref, wo_ref, o_ref, on_scr):
    nseq, ntok, _ = x_ref.shape
    for hd in range(HEADS):
        cols = slice(hd * DV, (hd + 1) * DV)
        on_scr[:, cols] = _head_rms(ob_ref[:, cols] + inter_ref[:, cols])
    out = _mix_tail(h_ref[...], mg_ref[...], on_scr, win_ref, wb_ref, wo_ref)
    o_ref[...] = x_ref[...] + g_ref[...] * out.reshape(nseq, ntok, D_MODEL)


def _retention_tables(seq_len):
    lc = min(seq_len, CHUNK)
    log_gamma = np.log1p(-np.power(2.0, -5.0 - np.arange(HEADS)))
    idx = np.arange(lc, dtype=np.float64)
    diff = idx[:, None] - idx[None, :]
    decay = np.where(diff >= 0, np.exp(np.maximum(diff, 0.0)[None] * log_gamma[:, None, None]), 0.0)
    q_decay = np.exp((idx + 1.0)[:, None] * log_gamma[None, :])
    k_decay = np.exp((lc - 1.0 - idx)[:, None] * log_gamma[None, :])
    s_decay = tuple(float(v) for v in np.exp(lc * log_gamma).astype(np.float32))
    rep = CHUNK // lc
    blockdiag = np.kron(np.eye(rep), np.ones((lc, lc)))
    dmask = np.tile(decay, (1, rep, rep)) * blockdiag[None]
    mmask = np.tile(np.tril(np.ones((lc, lc))), (rep, rep)) * blockdiag
    f32 = lambda a: jnp.asarray(a.astype(np.float32))
    return f32(dmask), f32(np.tile(q_decay, (rep, 1))), f32(np.tile(k_decay, (rep, 1))), s_decay, f32(mmask)


def _rotary_tables(pos, rows):
    half = DK // 2
    inv = ROPE_BASE ** (-np.arange(half, dtype=np.float64) / half)
    ang = np.asarray(pos, np.float64)[:, None] * inv[None, :]
    cos, sin = np.cos(ang), np.sin(ang)
    rep = rows // len(pos)
    cos2 = np.tile(np.concatenate([cos, cos], axis=-1), (rep, 1))
    sin2 = np.tile(np.concatenate([-sin, sin], axis=-1), (rep, 1))
    return jnp.asarray(cos2.astype(np.float32)), jnp.asarray(sin2.astype(np.float32))


def _mix_weight_specs():
    return [_const_spec((1, D_MODEL)),
            _const_spec((D_MODEL, OFF_END)),
            _const_spec((1, D_MODEL)), _const_spec((1, D_MODEL)),
            _const_spec((GROUPS, CHUNK, CHUNK)), _const_spec((CHUNK, CHUNK)),
            _const_spec((CHUNK, GROUPS))]


def _mix_out_weight_specs():
    return [_const_spec((HEADS, CHUNK, CHUNK)),
            _const_spec((CHUNK, HEADS)), _const_spec((CHUNK, HEADS)),
            _const_spec((D_MODEL, D_MODEL)), _const_spec((RET_V, D_MODEL)), _const_spec((D_MODEL, D_MODEL))]


def _mix_prompt(x, mods, mod_row0, nm_g, w_in, ln_g, ln_b, gm_ws, gm_bs, w_a, w_b, w_o, job):
    nb, seq, _ = x.shape
    tm = MIX_ROWS
    nj = seq // tm
    dmask, qd, kd, s_decay, mmask = _retention_tables(seq)
    cos2, sin2 = _rotary_tables(np.arange(seq), seq)
    xspec = pl.BlockSpec((1, tm, D_MODEL), lambda b, j: (b, j, 0))
    tspec = pl.BlockSpec((tm, DK), lambda b, j: (j, 0))
    state_shape = (1, nb, HEADS, DK, DV)

    jq, jkt, jrv, jstate, ntok = job
    nseq_total = jstate.shape[1]
    nstep = nseq_total // (nb * nj)
    seq_per_chunk = CHUNK // ntok
    assert nstep * nb * nj == nseq_total and seq_per_chunk % nstep == 0 and (nstep * ntok) % 8 == 0
    _, jqd, _, job_decay, _ = _retention_tables(ntok)
    step = lambda b, j: b * nj + j
    chunk = lambda b, j: step(b, j) * nstep // seq_per_chunk
    jsspec = pl.BlockSpec((1, nstep, HEADS, DK, DV), lambda b, j: (0, step(b, j), 0, 0, 0))
    jispec = pl.BlockSpec((nstep * ntok, RET_V), lambda b, j: (step(b, j), 0))
    job_in = [pl.BlockSpec((nstep * ntok, D_MODEL), lambda b, j: (step(b, j), 0)),
              pl.BlockSpec((1, D_MODEL, CHUNK), lambda b, j: (chunk(b, j), 0, 0)),
              pl.BlockSpec((1, CHUNK, RET_V), lambda b, j: (chunk(b, j), 0, 0)),
              _const_spec((ntok, HEADS)), jsspec]

    return pl.pallas_call(
        functools.partial(_mix_prompt_kernel, s_decay=s_decay, job_decay=job_decay),
        grid=(nb, nj),
        in_specs=([xspec] + _mod_specs(1, mod_row0, 3) + _mix_weight_specs() + [tspec, tspec]
                  + _mix_out_weight_specs() + job_in),
        out_specs=[xspec, pl.BlockSpec((1, 1, HEADS, DK, DV), lambda b, j: (0, b, 0, 0, 0)), jispec, jsspec],
        out_shape=[jax.ShapeDtypeStruct(x.shape, F32), jax.ShapeDtypeStruct(state_shape, F32),
                   jax.ShapeDtypeStruct((nseq_total * ntok, RET_V), F32), jax.ShapeDtypeStruct(jstate.shape, F32)],
        scratch_shapes=[pltpu.VMEM((tm, D_MODEL), BF16), pltpu.VMEM((tm, RET_V), F32)],
        compiler_params=pltpu.CompilerParams(
            dimension_semantics=("arbitrary", "arbitrary"), vmem_limit_bytes=VMEM_LIMIT_V7X),
        name="mix_prompt",
    )(x, mods, mods, mods, nm_g.reshape(1, D_MODEL), w_in, ln_g.reshape(1, D_MODEL), ln_b.reshape(1, D_MODEL),
      gm_ws, mmask, jnp.transpose(gm_bs), cos2, sin2, dmask, qd, kd, w_a, w_b, w_o,
      jq, jkt, jrv, jqd[:ntok], jstate)


def _mix_sample_front(x, mods, mod_row0, nm_g, w_in, ln_g, ln_b, gm_ws, gm_bs, w_a):
    nb, ntok, _ = x.shape
    rows = MIX_ROWS
    nseq = rows // ntok
    nsub = rows // CHUNK
    nrow, nchunk = nb * ntok, nb * ntok // CHUNK
    rep = CHUNK // ntok
    dmask, _, kd, _, mmask = _retention_tables(ntok)
    cos2, sin2 = _rotary_tables(PAST_LEN + np.arange(ntok), rows)
    onehot = jnp.asarray(np.tile(np.eye(ntok, dtype=np.float32), (rep, 1)))
    wmix = jnp.einsum("ra,gab,cb->grc", onehot, gm_ws[:, :ntok, :ntok], onehot, precision=lax.Precision.HIGHEST)
    bst = jnp.einsum("ra,ga->rg", onehot, gm_bs[:, :ntok], precision=lax.Precision.HIGHEST)
    xspec = pl.BlockSpec((nseq, ntok, D_MODEL), lambda i: (i, 0, 0))
    row_spec = lambda width: pl.BlockSpec((rows, width), lambda i: (i, 0))
    return pl.pallas_call(
        _mix_front_kernel,
        grid=(nb // nseq,),
        in_specs=([xspec] + _mod_specs(nseq, mod_row0, 3)[:2] + _mix_weight_specs()
                  + [_const_spec((rows, DK)), _const_spec((rows, DK)),
                     _const_spec((HEADS, CHUNK, CHUNK)), _const_spec((CHUNK, HEADS)),
                     _const_spec((D_MODEL, D_MODEL))]),
        out_specs=[pl.BlockSpec((1, nseq, ntok, D_MODEL), lambda i: (0, i, 0, 0)),
                   row_spec(D_MODEL), row_spec(D_MODEL), row_spec(D_MODEL),
                   pl.BlockSpec((nsub, D_MODEL, CHUNK), lambda i: (i, 0, 0)),
                   pl.BlockSpec((nsub, CHUNK, RET_V), lambda i: (i, 0, 0)),
                   row_spec(RET_V)],
        out_shape=[jax.ShapeDtypeStruct((1,) + x.shape, F32),
                   jax.ShapeDtypeStruct((nrow, D_MODEL), F32), jax.ShapeDtypeStruct((nrow, D_MODEL), BF16),
                   jax.ShapeDtypeStruct((nrow, D_MODEL), F32),
                   jax.ShapeDtypeStruct((nchunk, D_MODEL, CHUNK), F32),
                   jax.ShapeDtypeStruct((nchunk, CHUNK, RET_V), BF16),
                   jax.ShapeDtypeStruct((nrow, RET_V), F32)],
        scratch_shapes=[pltpu.VMEM((rows, D_MODEL), BF16)],
        compiler_params=pltpu.CompilerParams(
            dimension_semantics=("arbitrary",), vmem_limit_bytes=VMEM_LIMIT_V7X),
        name="mix_sample_front",
    )(x, mods, mods, nm_g.reshape(1, D_MODEL), w_in, ln_g.reshape(1, D_MODEL), ln_b.reshape(1, D_MODEL),
      wmix, mmask, bst, cos2, sin2, dmask, kd, w_a)


def _mix_sample_back(x, mods, mod_row0, h, merged, ob, inter, w_in, w_b, w_o):
    nb, ntok, _ = x.shape
    rows = MIX_ROWS
    nseq = rows // ntok
    xspec = pl.BlockSpec((nseq, ntok, D_MODEL), lambda i: (i, 0, 0))
    row_spec = lambda width: pl.BlockSpec((rows, width), lambda i: (i, 0))
    return pl.pallas_call(
        _mix_back_kernel,
        grid=(nb // nseq,),
        in_specs=[xspec, _mod_specs(nseq, mod_row0, 3)[2], row_spec(D_MODEL), row_spec(D_MODEL),
                  row_spec(RET_V), row_spec(RET_V),
                  _const_spec((D_MODEL, OFF_END)), _const_spec((RET_V, D_MODEL)), _const_spec((D_MODEL, D_MODEL))],
        out_specs=xspec,
        out_shape=jax.ShapeDtypeStruct(x.shape, F32),
        scratch_shapes=[pltpu.VMEM((rows, RET_V), F32)],
        compiler_params=pltpu.CompilerParams(
            dimension_semantics=("arbitrary",), vmem_limit_bytes=VMEM_LIMIT_V7X),
        name="mix_sample_back",
    )(x, mods, h, merged, ob, inter, w_in, w_b, w_o)


def kernel(x_prompt, x_sample, state_ret, c_prompt, c_sample, w_ada, b_ada, n1_g, w1_gate, w1_up, w1_down,
           nm_g, w_in, gm_ln_g, gm_ln_b, gm_ws, gm_bs, w_a, w_b, w_o, n2_g, w2_gate, w2_up, w2_down, final_g):
    assert w_ada.shape[0] == 1, "single-layer step"
    nbs = x_sample.shape[0]

    mods, w1 = _ada(jnp.concatenate([c_sample, c_prompt], axis=0), w_ada[0], b_ada[0],
                    cast=(w1_gate[0], w1_up[0], w1_down[0]))
    row_s, row_p = 0, nbs

    yp, later = _ffn(x_prompt, mods, row_p, 0, n1_g[0], *w1, final_g, final=False,
                     cast=(w_in[0], w_a[0], w_b[0], w_o[0], w2_gate[0], w2_up[0], w2_down[0]))
    w_in_b, w_a_b, w_b_b, w_o_b = later[:4]
    w2 = later[4:]
    gmlp = (nm_g[0], w_in_b, gm_ln_g[0], gm_ln_b[0], gm_ws[0], gm_bs[0], w_a_b)

    ys, _ = _ffn(x_sample, mods, row_s, 0, n1_g[0], *w1, final_g, final=False)
    vs, merged_s, h_s, q_s, kt_s, rv_s, ob_s = _mix_sample_front(ys, mods, row_s, *gmlp)

    yp, sp, inter_s, ss = _mix_prompt(yp, mods, row_p, *gmlp, w_b_b, w_o_b,
                                      job=(q_s, kt_s, rv_s, state_ret.astype(F32), x_sample.shape[1]))
    yp, _ = _ffn(yp, mods, row_p, 6, n2_g[0], *w2, final_g, final=True)

    ys = _mix_sample_back(ys, mods, row_s, h_s, merged_s, ob_s, inter_s, w_in_b, w_b_b, w_o_b)
    ys, _ = _ffn(ys, mods, row_s, 6, n2_g[0], *w2, final_g, final=True)

    return (yp, ys, sp, ss, vs)
```

```python
import functools

import jax
import jax.numpy as jnp
import numpy as np
from jax import lax
from jax.experimental import pallas as pl
from jax.experimental.pallas import tpu as pltpu

F32 = jnp.float32
BF16 = jnp.bfloat16

D_MODEL = 1024
D_FF = 2816
N_MOD = 9
EPS = 1e-6
ROPE_BASE = 10000.0
PAST_LEN = 16384
CHUNK = 128
GROUPS = 8
GROUP_DIM = D_MODEL // GROUPS
HEADS = 8
DK = D_MODEL // HEADS
DV = 2 * DK
RET_V = HEADS * DV
OFF_U, OFF_V, OFF_Q, OFF_K, OFF_RV, OFF_RG, OFF_GA, OFF_GB, OFF_END = (
    0, 1024, 2048, 3072, 4096, 6144, 8192, 9216, 10240)

VMEM_LIMIT_V7X = 56 * 1024 * 1024
FFN_ROWS = 512
FFN_SPLIT = 2
MIX_ROWS = 256
ADA_STEPS = 8
BF16_SUBLANES = 16


def _dot(a, b):
    return jnp.dot(a, b, preferred_element_type=F32)


def _silu(x):
    return x * jax.nn.sigmoid(x)


def _gelu_tanh(x):
    c = np.float32(np.sqrt(2.0 / np.pi))
    return 0.5 * x * (1.0 + jnp.tanh(c * (x + 0.044715 * (x * x * x))))


def _rms_norm(x, gain):
    return x * lax.rsqrt(jnp.mean(x * x, axis=-1, keepdims=True) + EPS) * gain


def _const_spec(shape):
    nd = len(shape)
    return pl.BlockSpec(shape, lambda *_: (0,) * nd, pipeline_mode=pl.Buffered(1))


def _mod_specs(bb, row0, first):
    assert row0 % bb == 0
    return [pl.BlockSpec((bb, 1, D_MODEL), lambda i, *_, k=k: (row0 // bb + i, 0, k))
            for k in range(first, first + 3)]


def _cast_plan(weights, grid):
    steps = int(np.prod(grid))
    in_specs, out_specs, out_shapes = [], [], []
    for w in weights:
        rows, cols = w.shape
        nblk = max(n for n in range(1, steps + 1)
                   if steps % n == 0 and rows % n == 0 and (rows // n) % BF16_SUBLANES == 0)
        per_blk = steps // nblk

        def index(*ids, per_blk=per_blk):
            step = 0
            for extent, i in zip(grid, ids):
                step = step * extent + i
            return (step // per_blk, 0)

        spec = pl.BlockSpec((rows // nblk, cols), index)
        in_specs.append(spec)
        out_specs.append(spec)
        out_shapes.append(jax.ShapeDtypeStruct(w.shape, BF16))
    return in_specs, out_specs, out_shapes


def _cast_blocks(src_refs, dst_refs):
    for src, dst in zip(src_refs, dst_refs, strict=True):
        dst[...] = src[...].astype(BF16)


def _ada_kernel(c_ref, w_ref, b_ref, *refs):
    ncast = (len(refs) - 1) // 2
    o_ref = refs[ncast]
    s = _silu(c_ref[...]).astype(BF16)
    m = _dot(s, w_ref[...].astype(BF16)) + b_ref[...]
    for r in range(m.shape[0]):
        o_ref[r] = m[r:r + 1, :]
    _cast_blocks(refs[:ncast], refs[ncast + 1:])


def _ada(c_all, w_ada, b_ada, cast):
    rows = c_all.shape[0]
    n = w_ada.shape[1]
    grid = (ADA_STEPS,)
    bn = n // ADA_STEPS
    cast_in, cast_out, cast_shapes = _cast_plan(cast, grid)
    out = pl.pallas_call(
        _ada_kernel,
        grid=grid,
        in_specs=[pl.BlockSpec((rows, D_MODEL), lambda j: (0, 0)),
                  pl.BlockSpec((D_MODEL, bn), lambda j: (0, j)),
                  pl.BlockSpec((1, bn), lambda j: (0, j))] + cast_in,
        out_specs=[pl.BlockSpec((rows, 1, bn), lambda j: (0, 0, j))] + cast_out,
        out_shape=[jax.ShapeDtypeStruct((rows, 1, n), F32)] + cast_shapes,
        compiler_params=pltpu.CompilerParams(
            dimension_semantics=("arbitrary",), vmem_limit_bytes=VMEM_LIMIT_V7X),
        name="ada",
    )(c_all, w_ada, b_ada.reshape(1, n), *cast)
    return out[0], out[1:]


def _ffn_kernel(x_ref, sh_ref, sc_ref, g_ref, ng_ref, wg_ref, wu_ref, wd_ref, fg_ref, *refs, final):
    ncast = (len(refs) - 1) // 2
    o_ref = refs[ncast]
    nb, nr, _ = x_ref.shape
    for s in range(FFN_SPLIT):
        if nb > 1:
            seqs, toks = slice(s * nb // FFN_SPLIT, (s + 1) * nb // FFN_SPLIT), slice(None)
        else:
            seqs, toks = slice(None), slice(s * nr // FFN_SPLIT, (s + 1) * nr // FFN_SPLIT)
        x = x_ref[seqs, toks, :]
        bb, r, _ = x.shape
        h = _rms_norm(x, ng_ref[...]) * (1.0 + sc_ref[seqs]) + sh_ref[seqs]
        h = h.reshape(bb * r, D_MODEL).astype(BF16)
        act = (_silu(_dot(h, wg_ref[...])) * _dot(h, wu_ref[...])).astype(BF16)
        y = _dot(act, wd_ref[...]).reshape(bb, r, D_MODEL)
        out = x + (0.5 * g_ref[seqs]) * y
        if final:
            out = _rms_norm(out, fg_ref[...])
        o_ref[seqs, toks, :] = out
    _cast_blocks(refs[:ncast], refs[ncast + 1:])


def _ffn(x, mods, mod_row0, mod_first, norm_g, wg, wu, wd, final_g, *, final, cast=()):
    nb, r, _ = x.shape
    br = min(r, FFN_ROWS)
    bb = FFN_ROWS // br
    grid = (nb // bb, r // br)
    xspec = pl.BlockSpec((bb, br, D_MODEL), lambda i, j: (i, j, 0))
    cast_in, cast_out, cast_shapes = _cast_plan(cast, grid)
    out = pl.pallas_call(
        functools.partial(_ffn_kernel, final=final),
        grid=grid,
        in_specs=[xspec] + _mod_specs(bb, mod_row0, mod_first) + [
            _const_spec((1, D_MODEL)),
            _const_spec((D_MODEL, D_FF)), _const_spec((D_MODEL, D_FF)), _const_spec((D_FF, D_MODEL)),
            _const_spec((1, D_MODEL))] + cast_in,
        out_specs=[xspec] + cast_out,
        out_shape=[jax.ShapeDtypeStruct(x.shape, F32)] + cast_shapes,
        compiler_params=pltpu.CompilerParams(
            dimension_semantics=("arbitrary", "arbitrary"), vmem_limit_bytes=VMEM_LIMIT_V7X),
        name="ffn_final" if final else "ffn",
    )(x, mods, mods, mods, norm_g.reshape(1, D_MODEL), wg, wu, wd, final_g.reshape(1, D_MODEL), *cast)
    return out[0], out[1:]


def _modulated_norm(x, ng_ref, sc_ref, sh_ref):
    return _rms_norm(x, ng_ref[...]) * (1.0 + sc_ref[...]) + sh_ref[...]


def _layer_norm(x, gain, bias):
    mu = jnp.mean(x, axis=-1, keepdims=True)
    var = jnp.mean(jnp.square(x - mu), axis=-1, keepdims=True)
    return (x - mu) * lax.rsqrt(var + EPS) * gain + bias


def _spatial_mix(u, vb, wmix_ref, mmask_ref, bst_ref, oa_scr):
    mmask = mmask_ref[...]
    for g in range(GROUPS):
        wg = (wmix_ref[g] * mmask).astype(BF16)
        cols = slice(g * GROUP_DIM, (g + 1) * GROUP_DIM)
        for c in range(u.shape[0] // CHUNK):
            rws = slice(c * CHUNK, (c + 1) * CHUNK)
            mix = _dot(wg, vb[rws, cols]) + bst_ref[:, g:g + 1]
            oa_scr[rws, cols] = (u[rws, cols] * mix).astype(BF16)


def _gmlp_branch(h, win_ref, lng_ref, lnb_ref, wmix_ref, mmask_ref, bst_ref, wa_ref, oa_scr):
    u = _gelu_tanh(_dot(h, win_ref[:, OFF_U:OFF_V]))
    v = _layer_norm(_gelu_tanh(_dot(h, win_ref[:, OFF_V:OFF_Q])), lng_ref[...], lnb_ref[...])
    _spatial_mix(u, v.astype(BF16), wmix_ref, mmask_ref, bst_ref, oa_scr)
    pa = _dot(oa_scr[...], wa_ref[...])
    ga = _dot(h, win_ref[:, OFF_GA:OFF_GB])
    return jax.nn.sigmoid(ga) * pa, v


def _rotary_heads(z, cos2, sin2):
    out = []
    for hd in range(HEADS):
        zh = z[:, hd * DK:(hd + 1) * DK]
        out.append(zh * cos2 + pltpu.roll(zh, DK // 2, 1) * sin2)
    return out


def _head_rms(o):
    return o * lax.rsqrt(jnp.mean(o * o, axis=-1, keepdims=True) + EPS)


def _mix_tail(h, merged, ob_scr, win_ref, wb_ref, wo_ref):
    ob = (ob_scr[...] * _silu(_dot(h, win_ref[:, OFF_RG:OFF_GA]))).astype(BF16)
    pb = _dot(ob, wb_ref[...])
    gb = _dot(h, win_ref[:, OFF_GB:OFF_END])
    merged = merged + jax.nn.sigmoid(gb) * pb
    return _dot(merged.astype(BF16), wo_ref[...])


def _state_step(q_ref, kt_ref, rv_ref, qd_ref, s0_ref, inter_ref, snew_ref, first_seq, s_decay):
    nstep = s0_ref.shape[1]
    ntok = qd_ref.shape[0]
    seq_per_chunk = CHUNK // ntok
    for t in range(nstep):
        tok = slice(t * ntok, (t + 1) * ntok)
        in_seq = lax.broadcasted_iota(jnp.int32, (DK, CHUNK), 1) // ntok == (first_seq + t) % seq_per_chunk
        for hd in range(HEADS):
            state = s0_ref[0, t, hd]
            qj = q_ref[tok, hd * DK:(hd + 1) * DK].astype(BF16)
            inter_ref[tok, hd * DV:(hd + 1) * DV] = _dot(qj, state.astype(BF16)) * qd_ref[:, hd:hd + 1]
            kt = jnp.where(in_seq, kt_ref[0, hd * DK:(hd + 1) * DK, :], 0.0).astype(BF16)
            upd = _dot(kt, rv_ref[0, :, hd * DV:(hd + 1) * DV])
            snew_ref[0, t, hd] = s_decay[hd] * state + upd


N_MIX_IN = 19
N_JOB_IN = 5


def _mix_prompt_kernel(*refs, s_decay, job_decay):
    (x_ref, sh_ref, sc_ref, g_ref, ng_ref, win_ref, lng_ref, lnb_ref, wmix_ref, mmask_ref,
     bst_ref, cos_ref, sin_ref, dmask_ref, qd_ref, kd_ref, wa_ref, wb_ref, wo_ref) = refs[:N_MIX_IN]
    rest = refs[N_MIX_IN:]
    if job_decay is not None:
        jq_ref, jkt_ref, jrv_ref, jqd_ref, js0_ref = rest[:N_JOB_IN]
        o_ref, s_ref, jinter_ref, jsnew_ref, oa_scr, ob_scr = rest[N_JOB_IN:]
    else:
        o_ref, s_ref, oa_scr, ob_scr = rest

    @pl.when(pl.program_id(1) == 0)
    def _():
        s_ref[...] = jnp.zeros(s_ref.shape, F32)

    x = x_ref[0]
    tm = x.shape[0]
    h = _modulated_norm(x, ng_ref, sc_ref.at[0], sh_ref.at[0]).astype(BF16)
    proj = lambda lo, hi: _dot(h, win_ref[:, lo:hi])
    zv, zu = proj(OFF_V, OFF_Q), proj(OFF_U, OFF_V)
    zq, zk = proj(OFF_Q, OFF_K), proj(OFF_K, OFF_RV)
    u = _gelu_tanh(zu)
    v = _layer_norm(_gelu_tanh(zv), lng_ref[...], lnb_ref[...])
    cos2, sin2 = cos_ref[...], sin_ref[...]
    qs = _rotary_heads(zq, cos2, sin2)
    ks = _rotary_heads(zk, cos2, sin2)
    rv = proj(OFF_RV, OFF_RG).astype(BF16)
    _spatial_mix(u, v.astype(BF16), wmix_ref, mmask_ref, bst_ref, oa_scr)
    kscale = np.float32(DK ** -0.5)
    gate_a = jax.nn.sigmoid(proj(OFF_GA, OFF_GB))
    heads = range(HEADS)
    qb = [qs[hd].astype(BF16) for hd in heads]
    kf = [ks[hd] * kscale for hd in heads]
    kb = [kf[hd].astype(BF16) for hd in heads]
    nt = (((1,), (1,)), ((), ()))
    for c in range(tm // CHUNK):
        rws = slice(c * CHUNK, (c + 1) * CHUNK)
        vc = [rv[rws, hd * DV:(hd + 1) * DV] for hd in heads]
        state = [s_ref[0, 0, hd] for hd in heads]
        scores = [lax.dot_general(qb[hd][rws], kb[hd][rws], nt, preferred_element_type=F32) for hd in heads]
        inter = [_dot(qb[hd][rws], state[hd].astype(BF16)) for hd in heads]
        kdt = [jnp.transpose(kf[hd][rws] * kd_ref[:, hd:hd + 1]).astype(BF16) for hd in heads]
        upd = [_dot(kdt[hd], vc[hd]) for hd in heads]
        intra = [_dot((scores[hd] * dmask_ref[hd]).astype(BF16), vc[hd]) for hd in heads]
        for hd in heads:
            s_ref[0, 0, hd] = s_decay[hd] * state[hd] + upd[hd]
            ob_scr[rws, hd * DV:(hd + 1) * DV] = _head_rms(intra[hd] + inter[hd] * qd_ref[:, hd:hd + 1])
        if c == 0:
            gate_r = _silu(proj(OFF_RG, OFF_GA))
    gate_b = jax.nn.sigmoid(proj(OFF_GB, OFF_END))
    pa = _dot(oa_scr[...], wa_ref[...])
    merged = gate_a * pa + gate_b * _dot((ob_scr[...] * gate_r).astype(BF16), wb_ref[...])
    o_ref[0] = x + g_ref[0] * _dot(merged.astype(BF16), wo_ref[...])
    if job_decay is not None:
        step = pl.program_id(0) * pl.num_programs(1) + pl.program_id(1)
        _state_step(jq_ref, jkt_ref, jrv_ref, jqd_ref, js0_ref, jinter_ref, jsnew_ref,
                    step * js0_ref.shape[1], job_decay)


def _mix_front_kernel(x_ref, sh_ref, sc_ref, ng_ref, win_ref, lng_ref, lnb_ref, wmix_ref, mmask_ref, bst_ref,
                      cos_ref, sin_ref, dmask_ref, kd_ref, wa_ref,
                      vrow_ref, mg_ref, h_ref, q_ref, kt_ref, rv_ref, ob_ref, oa_scr):
    nseq, ntok, _ = x_ref.shape
    rows = nseq * ntok
    h = _modulated_norm(x_ref[...], ng_ref, sc_ref, sh_ref).reshape(rows, D_MODEL).astype(BF16)
    h_ref[...] = h
    merged, v = _gmlp_branch(h, win_ref, lng_ref, lnb_ref, wmix_ref, mmask_ref, bst_ref, wa_ref, oa_scr)
    mg_ref[...] = merged
    vrow_ref[0] = v.reshape(nseq, ntok, D_MODEL)
    cos2, sin2 = cos_ref[...], sin_ref[...]
    qs = _rotary_heads(_dot(h, win_ref[:, OFF_Q:OFF_K]), cos2, sin2)
    ks = _rotary_heads(_dot(h, win_ref[:, OFF_K:OFF_RV]), cos2, sin2)
    rv = _dot(h, win_ref[:, OFF_RV:OFF_RG]).astype(BF16)
    kscale = np.float32(DK ** -0.5)
    heads = range(HEADS)
    kf = [ks[hd] * kscale for hd in heads]
    nt = (((1,), (1,)), ((), ()))
    for c in range(rows // CHUNK):
        rws = slice(c * CHUNK, (c + 1) * CHUNK)
        rv_ref[c] = rv[rws]
        scores = [lax.dot_general(qs[hd][rws].astype(BF16), kf[hd][rws].astype(BF16), nt,
                                  preferred_element_type=F32) for hd in heads]
        for hd in heads:
            kt_ref[c, hd * DK:(hd + 1) * DK, :] = jnp.transpose(kf[hd][rws] * kd_ref[:, hd:hd + 1])
        for hd in heads:
            ob_ref[rws, hd * DV:(hd + 1) * DV] = _dot(
                (scores[hd] * dmask_ref[hd]).astype(BF16), rv[rws, hd * DV:(hd + 1) * DV])
    for hd in heads:
        q_ref[:, hd * DK:(hd + 1) * DK] = qs[hd]


def _mix_back_kernel(x_ref, g_ref, h_ref, mg_ref, ob_ref, inter_ref, win_ref, wb_ref, wo_ref, o_ref, on_scr):
    nseq, ntok, _ = x_ref.shape
    for hd in range(HEADS):
        cols = slice(hd * DV, (hd + 1) * DV)
        on_scr[:, cols] = _head_rms(ob_ref[:, cols] + inter_ref[:, cols])
    out = _mix_tail(h_ref[...], mg_ref[...], on_scr, win_ref, wb_ref, wo_ref)
    o_ref[...] = x_ref[...] + g_ref[...] * out.reshape(nseq, ntok, D_MODEL)


def _retention_tables(seq_len):
    lc = min(seq_len, CHUNK)
    log_gamma = np.log1p(-np.power(2.0, -5.0 - np.arange(HEADS)))
    idx = np.arange(lc, dtype=np.float64)
    diff = idx[:, None] - idx[None, :]
    decay = np.where(diff >= 0, np.exp(np.maximum(diff, 0.0)[None] * log_gamma[:, None, None]), 0.0)
    q_decay = np.exp((idx + 1.0)[:, None] * log_gamma[None, :])
    k_decay = np.exp((lc - 1.0 - idx)[:, None] * log_gamma[None, :])
    s_decay = tuple(float(v) for v in np.exp(lc * log_gamma).astype(np.float32))
    rep = CHUNK // lc
    blockdiag = np.kron(np.eye(rep), np.ones((lc, lc)))
    dmask = np.tile(decay, (1, rep, rep)) * blockdiag[None]
    mmask = np.tile(np.tril(np.ones((lc, lc))), (rep, rep)) * blockdiag
    f32 = lambda a: jnp.asarray(a.astype(np.float32))
    return f32(dmask), f32(np.tile(q_decay, (rep, 1))), f32(np.tile(k_decay, (rep, 1))), s_decay, f32(mmask)


def _rotary_tables(pos, rows):
    half = DK // 2
    inv = ROPE_BASE ** (-np.arange(half, dtype=np.float64) / half)
    ang = np.asarray(pos, np.float64)[:, None] * inv[None, :]
    cos, sin = np.cos(ang), np.sin(ang)
    rep = rows // len(pos)
    cos2 = np.tile(np.concatenate([cos, cos], axis=-1), (rep, 1))
    sin2 = np.tile(np.concatenate([-sin, sin], axis=-1), (rep, 1))
    return jnp.asarray(cos2.astype(np.float32)), jnp.asarray(sin2.astype(np.float32))


def _mix_weight_specs():
    return [_const_spec((1, D_MODEL)),
            _const_spec((D_MODEL, OFF_END)),
            _const_spec((1, D_MODEL)), _const_spec((1, D_MODEL)),
            _const_spec((GROUPS, CHUNK, CHUNK)), _const_spec((CHUNK, CHUNK)),
            _const_spec((CHUNK, GROUPS))]


def _mix_out_weight_specs():
    return [_const_spec((HEADS, CHUNK, CHUNK)),
            _const_spec((CHUNK, HEADS)), _const_spec((CHUNK, HEADS)),
            _const_spec((D_MODEL, D_MODEL)), _const_spec((RET_V, D_MODEL)), _const_spec((D_MODEL, D_MODEL))]


def _mix_prompt(x, mods, mod_row0, nm_g, w_in, ln_g, ln_b, gm_ws, gm_bs, w_a, w_b, w_o, job):
    nb, seq, _ = x.shape
    tm = MIX_ROWS
    nj = seq // tm
    dmask, qd, kd, s_decay, mmask = _retention_tables(seq)
    cos2, sin2 = _rotary_tables(np.arange(seq), seq)
    xspec = pl.BlockSpec((1, tm, D_MODEL), lambda b, j: (b, j, 0))
    tspec = pl.BlockSpec((tm, DK), lambda b, j: (j, 0))
    state_shape = (1, nb, HEADS, DK, DV)

    jq, jkt, jrv, jstate, ntok = job
    nseq_total = jstate.shape[1]
    nstep = nseq_total // (nb * nj)
    seq_per_chunk = CHUNK // ntok
    assert nstep * nb * nj == nseq_total and seq_per_chunk % nstep == 0 and (nstep * ntok) % 8 == 0
    _, jqd, _, job_decay, _ = _retention_tables(ntok)
    step = lambda b, j: b * nj + j
    chunk = lambda b, j: step(b, j) * nstep // seq_per_chunk
    jsspec = pl.BlockSpec((1, nstep, HEADS, DK, DV), lambda b, j: (0, step(b, j), 0, 0, 0))
    jispec = pl.BlockSpec((nstep * ntok, RET_V), lambda b, j: (step(b, j), 0))
    job_in = [pl.BlockSpec((nstep * ntok, D_MODEL), lambda b, j: (step(b, j), 0)),
              pl.BlockSpec((1, D_MODEL, CHUNK), lambda b, j: (chunk(b, j), 0, 0)),
              pl.BlockSpec((1, CHUNK, RET_V), lambda b, j: (chunk(b, j), 0, 0)),
              _const_spec((ntok, HEADS)), jsspec]

    return pl.pallas_call(
        functools.partial(_mix_prompt_kernel, s_decay=s_decay, job_decay=job_decay),
        grid=(nb, nj),
        in_specs=([xspec] + _mod_specs(1, mod_row0, 3) + _mix_weight_specs() + [tspec, tspec]
                  + _mix_out_weight_specs() + job_in),
        out_specs=[xspec, pl.BlockSpec((1, 1, HEADS, DK, DV), lambda b, j: (0, b, 0, 0, 0)), jispec, jsspec],
        out_shape=[jax.ShapeDtypeStruct(x.shape, F32), jax.ShapeDtypeStruct(state_shape, F32),
                   jax.ShapeDtypeStruct((nseq_total * ntok, RET_V), F32), jax.ShapeDtypeStruct(jstate.shape, F32)],
        scratch_shapes=[pltpu.VMEM((tm, D_MODEL), BF16), pltpu.VMEM((tm, RET_V), F32)],
        compiler_params=pltpu.CompilerParams(
            dimension_semantics=("arbitrary", "arbitrary"), vmem_limit_bytes=VMEM_LIMIT_V7X),
        name="mix_prompt",
    )(x, mods, mods, mods, nm_g.reshape(1, D_MODEL), w_in, ln_g.reshape(1, D_MODEL), ln_b.reshape(1, D_MODEL),
      gm_ws, mmask, jnp.transpose(gm_bs), cos2, sin2, dmask, qd, kd, w_a, w_b, w_o,
      jq, jkt, jrv, jqd[:ntok], jstate)


def _mix_sample_front(x, mods, mod_row0, nm_g, w_in, ln_g, ln_b, gm_ws, gm_bs, w_a):
    nb, ntok, _ = x.shape
    rows = MIX_ROWS
    nseq = rows // ntok
    nsub = rows // CHUNK
    nrow, nchunk = nb * ntok, nb * ntok // CHUNK
    rep = CHUNK // ntok
    dmask, _, kd, _, mmask = _retention_tables(ntok)
    cos2, sin2 = _rotary_tables(PAST_LEN + np.arange(ntok), rows)
    onehot = jnp.asarray(np.tile(np.eye(ntok, dtype=np.float32), (rep, 1)))
    wmix = jnp.einsum("ra,gab,cb->grc", onehot, gm_ws[:, :ntok, :ntok], onehot, precision=lax.Precision.HIGHEST)
    bst = jnp.einsum("ra,ga->rg", onehot, gm_bs[:, :ntok], precision=lax.Precision.HIGHEST)
    xspec = pl.BlockSpec((nseq, ntok, D_MODEL), lambda i: (i, 0, 0))
    row_spec = lambda width: pl.BlockSpec((rows, width), lambda i: (i, 0))
    return pl.pallas_call(
        _mix_front_kernel,
        grid=(nb // nseq,),
        in_specs=([xspec] + _mod_specs(nseq, mod_row0, 3)[:2] + _mix_weight_specs()
                  + [_const_spec((rows, DK)), _const_spec((rows, DK)),
                     _const_spec((HEADS, CHUNK, CHUNK)), _const_spec((CHUNK, HEADS)),
                     _const_spec((D_MODEL, D_MODEL))]),
        out_specs=[pl.BlockSpec((1, nseq, ntok, D_MODEL), lambda i: (0, i, 0, 0)),
                   row_spec(D_MODEL), row_spec(D_MODEL), row_spec(D_MODEL),
                   pl.BlockSpec((nsub, D_MODEL, CHUNK), lambda i: (i, 0, 0)),
                   pl.BlockSpec((nsub, CHUNK, RET_V), lambda i: (i, 0, 0)),
                   row_spec(RET_V)],
        out_shape=[jax.ShapeDtypeStruct((1,) + x.shape, F32),
                   jax.ShapeDtypeStruct((nrow, D_MODEL), F32), jax.ShapeDtypeStruct((nrow, D_MODEL), BF16),
                   jax.ShapeDtypeStruct((nrow, D_MODEL), F32),
                   jax.ShapeDtypeStruct((nchunk, D_MODEL, CHUNK), F32),
                   jax.ShapeDtypeStruct((nchunk, CHUNK, RET_V), BF16),
                   jax.ShapeDtypeStruct((nrow, RET_V), F32)],
        scratch_shapes=[pltpu.VMEM((rows, D_MODEL), BF16)],
        compiler_params=pltpu.CompilerParams(
            dimension_semantics=("arbitrary",), vmem_limit_bytes=VMEM_LIMIT_V7X),
        name="mix_sample_front",
    )(x, mods, mods, nm_g.reshape(1, D_MODEL), w_in, ln_g.reshape(1, D_MODEL), ln_b.reshape(1, D_MODEL),
      wmix, mmask, bst, cos2, sin2, dmask, kd, w_a)


def _mix_sample_back(x, mods, mod_row0, h, merged, ob, inter, w_in, w_b, w_o):
    nb, ntok, _ = x.shape
    rows = MIX_ROWS
    nseq = rows // ntok
    xspec = pl.BlockSpec((nseq, ntok, D_MODEL), lambda i: (i, 0, 0))
    row_spec = lambda width: pl.BlockSpec((rows, width), lambda i: (i, 0))
    return pl.pallas_call(
        _mix_back_kernel,
        grid=(nb // nseq,),
        in_specs=[xspec, _mod_specs(nseq, mod_row0, 3)[2], row_spec(D_MODEL), row_spec(D_MODEL),
                  row_spec(RET_V), row_spec(RET_V),
                  _const_spec((D_MODEL, OFF_END)), _const_spec((RET_V, D_MODEL)), _const_spec((D_MODEL, D_MODEL))],
        out_specs=xspec,
        out_shape=jax.ShapeDtypeStruct(x.shape, F32),
        scratch_shapes=[pltpu.VMEM((rows, RET_V), F32)],
        compiler_params=pltpu.CompilerParams(
            dimension_semantics=("arbitrary",), vmem_limit_bytes=VMEM_LIMIT_V7X),
        name="mix_sample_back",
    )(x, mods, h, merged, ob, inter, w_in, w_b, w_o)


def kernel(x_prompt, x_sample, state_ret, c_prompt, c_sample, w_ada, b_ada, n1_g, w1_gate, w1_up, w1_down,
           nm_g, w_in, gm_ln_g, gm_ln_b, gm_ws, gm_bs, w_a, w_b, w_o, n2_g, w2_gate, w2_up, w2_down, final_g):
    assert w_ada.shape[0] == 1, "single-layer step"
    nbs = x_sample.shape[0]

    mods, w1 = _ada(jnp.concatenate([c_sample, c_prompt], axis=0), w_ada[0], b_ada[0],
                    cast=(w1_gate[0], w1_up[0], w1_down[0]))
    row_s, row_p = 0, nbs

    yp, later = _ffn(x_prompt, mods, row_p, 0, n1_g[0], *w1, final_g, final=False,
                     cast=(w_in[0], w_a[0], w_b[0], w_o[0], w2_gate[0], w2_up[0], w2_down[0]))
    w_in_b, w_a_b, w_b_b, w_o_b = later[:4]
    w2 = later[4:]
    gmlp = (nm_g[0], w_in_b, gm_ln_g[0], gm_ln_b[0], gm_ws[0], gm_bs[0], w_a_b)

    ys, _ = _ffn(x_sample, mods, row_s, 0, n1_g[0], *w1, final_g, final=False)
    vs, merged_s, h_s, q_s, kt_s, rv_s, ob_s = _mix_sample_front(ys, mods, row_s, *gmlp)

    yp, sp, inter_s, ss = _mix_prompt(yp, mods, row_p, *gmlp, w_b_b, w_o_b,
                                      job=(q_s, kt_s, rv_s, state_ret.astype(F32), x_sample.shape[1]))
    yp, _ = _ffn(yp, mods, row_p, 6, n2_g[0], *w2, final_g, final=True)

    ys = _mix_sample_back(ys, mods, row_s, h_s, merged_s, ob_s, inter_s, w_in_b, w_b_b, w_o_b)
    ys, _ = _ffn(ys, mods, row_s, 6, n2_g[0], *w2, final_g, final=True)

    return (yp, ys, sp, ss, vs)
```

```python
import functools

import jax
import jax.numpy as jnp
import numpy as np
from jax import lax
from jax.experimental import pallas as pl
from jax.experimental.pallas import tpu as pltpu

F32 = jnp.float32
BF16 = jnp.bfloat16

D_MODEL = 1024
D_FF = 2816
N_MOD = 9
EPS = 1e-6
ROPE_BASE = 10000.0
PAST_LEN = 16384
CHUNK = 128
GROUPS = 8
GROUP_DIM = D_MODEL // GROUPS
HEADS = 8
DK = D_MODEL // HEADS
DV = 2 * DK
RET_V = HEADS * DV
OFF_U, OFF_V, OFF_Q, OFF_K, OFF_RV, OFF_RG, OFF_GA, OFF_GB, OFF_END = (
    0, 1024, 2048, 3072, 4096, 6144, 8192, 9216, 10240)

VMEM_LIMIT_V7X = 56 * 1024 * 1024
FFN_ROWS = 512
FFN_SPLIT = 2
MIX_ROWS = 256
ADA_STEPS = 8
BF16_SUBLANES = 16


def _dot(a, b):
    return jnp.dot(a, b, preferred_element_type=F32)


def _silu(x):
    return x * jax.nn.sigmoid(x)


def _gelu_tanh(x):
    c = np.float32(np.sqrt(2.0 / np.pi))
    return 0.5 * x * (1.0 + jnp.tanh(c * (x + 0.044715 * (x * x * x))))


def _rms_norm(x, gain):
    return x * lax.rsqrt(jnp.mean(x * x, axis=-1, keepdims=True) + EPS) * gain


def _const_spec(shape):
    nd = len(shape)
    return pl.BlockSpec(shape, lambda *_: (0,) * nd, pipeline_mode=pl.Buffered(1))


def _mod_specs(bb, row0, first):
    assert row0 % bb == 0
    return [pl.BlockSpec((bb, 1, D_MODEL), lambda i, *_, k=k: (row0 // bb + i, 0, k))
            for k in range(first, first + 3)]


def _cast_plan(weights, nsteps):
    in_specs, out_specs, out_shapes = [], [], []
    for w in weights:
        rows, cols = w.shape
        nblk = max(n for n in range(1, nsteps + 1) if rows % n == 0 and (rows // n) % BF16_SUBLANES == 0)
        spec = pl.BlockSpec((rows // nblk, cols), lambda step, nblk=nblk: (step * nblk // nsteps, 0))
        in_specs.append(spec)
        out_specs.append(spec)
        out_shapes.append(jax.ShapeDtypeStruct(w.shape, BF16))
    return in_specs, out_specs, out_shapes


def _cast_blocks(src_refs, dst_refs):
    for src, dst in zip(src_refs, dst_refs, strict=True):
        dst[...] = src[...].astype(BF16)


def _ada_kernel(c_ref, w_ref, b_ref, *refs):
    ncast = (len(refs) - 1) // 2
    o_ref = refs[ncast]
    s = _silu(c_ref[...]).astype(BF16)
    m = _dot(s, w_ref[...].astype(BF16)) + b_ref[...]
    for r in range(m.shape[0]):
        o_ref[r] = m[r:r + 1, :]
    _cast_blocks(refs[:ncast], refs[ncast + 1:])


def _ada(c_all, w_ada, b_ada, cast):
    rows = c_all.shape[0]
    n = w_ada.shape[1]
    grid = (ADA_STEPS,)
    bn = n // ADA_STEPS
    cast_in, cast_out, cast_shapes = _cast_plan(cast, ADA_STEPS)
    out = pl.pallas_call(
        _ada_kernel,
        grid=grid,
        in_specs=[pl.BlockSpec((rows, D_MODEL), lambda j: (0, 0)),
                  pl.BlockSpec((D_MODEL, bn), lambda j: (0, j)),
                  pl.BlockSpec((1, bn), lambda j: (0, j))] + cast_in,
        out_specs=[pl.BlockSpec((rows, 1, bn), lambda j: (0, 0, j))] + cast_out,
        out_shape=[jax.ShapeDtypeStruct((rows, 1, n), F32)] + cast_shapes,
        compiler_params=pltpu.CompilerParams(
            dimension_semantics=("arbitrary",), vmem_limit_bytes=VMEM_LIMIT_V7X),
        name="ada",
    )(c_all, w_ada, b_ada.reshape(1, n), *cast)
    return out[0], out[1:]


def _ffn_rows(x_ref, sh_ref, sc_ref, g_ref, ng_ref, wg_ref, wu_ref, wd_ref, fg_ref, o_ref, final):
    nb, nr, _ = x_ref.shape
    if nb > 1:
        groups = [(slice(s * nb // FFN_SPLIT, (s + 1) * nb // FFN_SPLIT), slice(None)) for s in range(FFN_SPLIT)]
    else:
        groups = [(slice(None), slice(s * nr // FFN_SPLIT, (s + 1) * nr // FFN_SPLIT)) for s in range(FFN_SPLIT)]
    xs = [x_ref[seqs, toks, :] for seqs, toks in groups]
    bb, r, _ = xs[0].shape
    hs = [(_rms_norm(x, ng_ref[...]) * (1.0 + sc_ref[seqs]) + sh_ref[seqs]).reshape(bb * r, D_MODEL).astype(BF16)
          for x, (seqs, _) in zip(xs, groups)]
    gates = [_dot(h, wg_ref[...]) for h in hs]
    ups = [_dot(h, wu_ref[...]) for h in hs]
    acts = [(_silu(gt) * up).astype(BF16) for gt, up in zip(gates, ups)]
    ys = [_dot(act, wd_ref[...]).reshape(bb, r, D_MODEL) for act in acts]
    for x, y, (seqs, toks) in zip(xs, ys, groups):
        out = x + (0.5 * g_ref[seqs]) * y
        if final:
            out = _rms_norm(out, fg_ref[...])
        o_ref[seqs, toks, :] = out


N_FFN_BATCH_IN = 4
N_FFN_WEIGHT_IN = 5


def _ffn_kernel(*refs, final, batch_steps):
    nbatch = len(batch_steps)
    n_in = nbatch * N_FFN_BATCH_IN + N_FFN_WEIGHT_IN
    ncast = (len(refs) - n_in - nbatch) // 2
    weights = refs[nbatch * N_FFN_BATCH_IN:n_in]
    outs = refs[n_in + ncast:]
    step = pl.program_id(0)
    first = 0
    for b, nsteps in enumerate(batch_steps):
        x_ref, sh_ref, sc_ref, g_ref = refs[b * N_FFN_BATCH_IN:(b + 1) * N_FFN_BATCH_IN]

        @pl.when((step >= first) & (step < first + nsteps))
        def _(x_ref=x_ref, sh_ref=sh_ref, sc_ref=sc_ref, g_ref=g_ref, o_ref=outs[b]):
            _ffn_rows(x_ref, sh_ref, sc_ref, g_ref, *weights, o_ref, final)

        first += nsteps
    _cast_blocks(refs[n_in:n_in + ncast], outs[nbatch:])


def _ffn(batches, mods, mod_first, norm_g, wg, wu, wd, final_g, *, final, cast=()):
    operands, in_specs, out_specs, out_shapes, batch_steps = [], [], [], [], []
    first = 0
    for x, mod_row0 in batches:
        nb, r, _ = x.shape
        br = min(r, FFN_ROWS)
        bb = FFN_ROWS // br
        nj = r // br
        nsteps = (nb // bb) * nj
        assert mod_row0 % bb == 0
        local = lambda step, first=first, nsteps=nsteps: jnp.clip(step - first, 0, nsteps - 1)
        xspec = pl.BlockSpec((bb, br, D_MODEL), lambda step, local=local, nj=nj: (local(step) // nj, local(step) % nj, 0))
        mspecs = [pl.BlockSpec((bb, 1, D_MODEL),
                               lambda step, local=local, nj=nj, row=mod_row0 // bb, k=k: (row + local(step) // nj, 0, k))
                  for k in range(mod_first, mod_first + 3)]
        operands += [x, mods, mods, mods]
        in_specs += [xspec] + mspecs
        out_specs.append(xspec)
        out_shapes.append(jax.ShapeDtypeStruct(x.shape, F32))
        batch_steps.append(nsteps)
        first += nsteps
    cast_in, cast_out, cast_shapes = _cast_plan(cast, first)
    out = pl.pallas_call(
        functools.partial(_ffn_kernel, final=final, batch_steps=tuple(batch_steps)),
        grid=(first,),
        in_specs=in_specs + [
            _const_spec((1, D_MODEL)),
            _const_spec((D_MODEL, D_FF)), _const_spec((D_MODEL, D_FF)), _const_spec((D_FF, D_MODEL)),
            _const_spec((1, D_MODEL))] + cast_in,
        out_specs=out_specs + cast_out,
        out_shape=out_shapes + cast_shapes,
        compiler_params=pltpu.CompilerParams(
            dimension_semantics=("arbitrary",), vmem_limit_bytes=VMEM_LIMIT_V7X),
        name="ffn_final" if final else "ffn",
    )(*operands, norm_g.reshape(1, D_MODEL), wg, wu, wd, final_g.reshape(1, D_MODEL), *cast)
    return out[:len(batches)], out[len(batches):]


def _modulated_norm(x, ng_ref, sc_ref, sh_ref):
    return _rms_norm(x, ng_ref[...]) * (1.0 + sc_ref[...]) + sh_ref[...]


def _layer_norm(x, gain, bias):
    mu = jnp.mean(x, axis=-1, keepdims=True)
    var = jnp.mean(jnp.square(x - mu), axis=-1, keepdims=True)
    return (x - mu) * lax.rsqrt(var + EPS) * gain + bias


def _spatial_mix(u, vb, wmix_ref, mmask_ref, bst_ref, oa_scr):
    mmask = mmask_ref[...]
    for g in range(GROUPS):
        wg = (wmix_ref[g] * mmask).astype(BF16)
        cols = slice(g * GROUP_DIM, (g + 1) * GROUP_DIM)
        for c in range(u.shape[0] // CHUNK):
            rws = slice(c * CHUNK, (c + 1) * CHUNK)
            mix = _dot(wg, vb[rws, cols]) + bst_ref[:, g:g + 1]
            oa_scr[rws, cols] = (u[rws, cols] * mix).astype(BF16)


def _gmlp_branch(h, win_ref, lng_ref, lnb_ref, wmix_ref, mmask_ref, bst_ref, wa_ref, oa_scr):
    u = _gelu_tanh(_dot(h, win_ref[:, OFF_U:OFF_V]))
    v = _layer_norm(_gelu_tanh(_dot(h, win_ref[:, OFF_V:OFF_Q])), lng_ref[...], lnb_ref[...])
    _spatial_mix(u, v.astype(BF16), wmix_ref, mmask_ref, bst_ref, oa_scr)
    pa = _dot(oa_scr[...], wa_ref[...])
    ga = _dot(h, win_ref[:, OFF_GA:OFF_GB])
    return jax.nn.sigmoid(ga) * pa, v


def _rotary_heads(z, cos2, sin2):
    out = []
    for hd in range(HEADS):
        zh = z[:, hd * DK:(hd + 1) * DK]
        out.append(zh * cos2 + pltpu.roll(zh, DK // 2, 1) * sin2)
    return out


def _head_rms(o):
    return o * lax.rsqrt(jnp.mean(o * o, axis=-1, keepdims=True) + EPS)


def _state_step(q_ref, kt_ref, rv_ref, qd_ref, s0_ref, inter_ref, snew_ref, first_seq, s_decay):
    nstep = s0_ref.shape[1]
    ntok = qd_ref.shape[0]
    seq_per_chunk = CHUNK // ntok
    for t in range(nstep):
        tok = slice(t * ntok, (t + 1) * ntok)
        in_seq = lax.broadcasted_iota(jnp.int32, (DK, CHUNK), 1) // ntok == (first_seq + t) % seq_per_chunk
        for hd in range(HEADS):
            state = s0_ref[0, t, hd]
            qj = q_ref[tok, hd * DK:(hd + 1) * DK].astype(BF16)
            inter_ref[tok, hd * DV:(hd + 1) * DV] = _dot(qj, state.astype(BF16)) * qd_ref[:, hd:hd + 1]
            kt = jnp.where(in_seq, kt_ref[0, hd * DK:(hd + 1) * DK, :], 0.0).astype(BF16)
            upd = _dot(kt, rv_ref[0, :, hd * DV:(hd + 1) * DV])
            snew_ref[0, t, hd] = s_decay[hd] * state + upd


N_MIX_IN = 19
N_JOB_IN = 5


def _mix_prompt_kernel(*refs, s_decay, job_decay):
    (x_ref, sh_ref, sc_ref, g_ref, ng_ref, win_ref, lng_ref, lnb_ref, wmix_ref, mmask_ref,
     bst_ref, cos_ref, sin_ref, dmask_ref, qd_ref, kd_ref, wa_ref, wb_ref, wo_ref) = refs[:N_MIX_IN]
    rest = refs[N_MIX_IN:]
    if job_decay is not None:
        jq_ref, jkt_ref, jrv_ref, jqd_ref, js0_ref = rest[:N_JOB_IN]
        o_ref, s_ref, jinter_ref, jsnew_ref, oa_scr, ob_scr = rest[N_JOB_IN:]
    else:
        o_ref, s_ref, oa_scr, ob_scr = rest

    @pl.when(pl.program_id(1) == 0)
    def _():
        s_ref[...] = jnp.zeros(s_ref.shape, F32)

    x = x_ref[0]
    tm = x.shape[0]
    h = _modulated_norm(x, ng_ref, sc_ref.at[0], sh_ref.at[0]).astype(BF16)
    proj = lambda lo, hi: _dot(h, win_ref[:, lo:hi])
    cos2, sin2 = cos_ref[...], sin_ref[...]
    zv = proj(OFF_V, OFF_Q)
    zu = proj(OFF_U, OFF_V)
    gv = _gelu_tanh(zv)
    zq = proj(OFF_Q, OFF_K)
    v = _layer_norm(gv, lng_ref[...], lnb_ref[...])
    u = _gelu_tanh(zu)
    zk = proj(OFF_K, OFF_RV)
    qs = _rotary_heads(zq, cos2, sin2)
    rv = proj(OFF_RV, OFF_RG).astype(BF16)
    ks = _rotary_heads(zk, cos2, sin2)
    _spatial_mix(u, v.astype(BF16), wmix_ref, mmask_ref, bst_ref, oa_scr)
    kscale = np.float32(DK ** -0.5)
    gate_a = jax.nn.sigmoid(proj(OFF_GA, OFF_GB))
    heads = range(HEADS)
    qb = [qs[hd].astype(BF16) for hd in heads]
    kf = [ks[hd] * kscale for hd in heads]
    kb = [kf[hd].astype(BF16) for hd in heads]
    nt = (((1,), (1,)), ((), ()))
    for c in range(tm // CHUNK):
        rws = slice(c * CHUNK, (c + 1) * CHUNK)
        vc = [rv[rws, hd * DV:(hd + 1) * DV] for hd in heads]
        state = [s_ref[0, 0, hd] for hd in heads]
        scores = [lax.dot_general(qb[hd][rws], kb[hd][rws], nt, preferred_element_type=F32) for hd in heads]
        inter = [_dot(qb[hd][rws], state[hd].astype(BF16)) for hd in heads]
        kdt = [jnp.transpose(kf[hd][rws] * kd_ref[:, hd:hd + 1]).astype(BF16) for hd in heads]
        upd = [_dot(kdt[hd], vc[hd]) for hd in heads]
        intra = [_dot((scores[hd] * dmask_ref[hd]).astype(BF16), vc[hd]) for hd in heads]
        for hd in heads:
            s_ref[0, 0, hd] = s_decay[hd] * state[hd] + upd[hd]
            ob_scr[rws, hd * DV:(hd + 1) * DV] = _head_rms(intra[hd] + inter[hd] * qd_ref[:, hd:hd + 1])
        if c == 0:
            gate_r = _silu(proj(OFF_RG, OFF_GA))
    gate_b = jax.nn.sigmoid(proj(OFF_GB, OFF_END))
    pa = _dot(oa_scr[...], wa_ref[...])
    merged = gate_a * pa + gate_b * _dot((ob_scr[...] * gate_r).astype(BF16), wb_ref[...])
    o_ref[0] = x + g_ref[0] * _dot(merged.astype(BF16), wo_ref[...])
    if job_decay is not None:
        step = pl.program_id(0) * pl.num_programs(1) + pl.program_id(1)
        _state_step(jq_ref, jkt_ref, jrv_ref, jqd_ref, js0_ref, jinter_ref, jsnew_ref,
                    step * js0_ref.shape[1], job_decay)


def _mix_front_kernel(x_ref, sh_ref, sc_ref, ng_ref, win_ref, lng_ref, lnb_ref, wmix_ref, mmask_ref, bst_ref,
                      cos_ref, sin_ref, dmask_ref, kd_ref, wa_ref,
                      vrow_ref, mg_ref, h_ref, q_ref, kt_ref, rv_ref, ob_ref, oa_scr):
    nseq, ntok, _ = x_ref.shape
    rows = nseq * ntok
    h = _modulated_norm(x_ref[...], ng_ref, sc_ref, sh_ref).reshape(rows, D_MODEL).astype(BF16)
    h_ref[...] = h
    merged, v = _gmlp_branch(h, win_ref, lng_ref, lnb_ref, wmix_ref, mmask_ref, bst_ref, wa_ref, oa_scr)
    mg_ref[...] = merged
    vrow_ref[0] = v.reshape(nseq, ntok, D_MODEL)
    cos2, sin2 = cos_ref[...], sin_ref[...]
    qs = _rotary_heads(_dot(h, win_ref[:, OFF_Q:OFF_K]), cos2, sin2)
    ks = _rotary_heads(_dot(h, win_ref[:, OFF_K:OFF_RV]), cos2, sin2)
    rv = _dot(h, win_ref[:, OFF_RV:OFF_RG]).astype(BF16)
    kscale = np.float32(DK ** -0.5)
    heads = range(HEADS)
    kf = [ks[hd] * kscale for hd in heads]
    nt = (((1,), (1,)), ((), ()))
    for c in range(rows // CHUNK):
        rws = slice(c * CHUNK, (c + 1) * CHUNK)
        rv_ref[c] = rv[rws]
        scores = [lax.dot_general(qs[hd][rws].astype(BF16), kf[hd][rws].astype(BF16), nt,
                                  preferred_element_type=F32) for hd in heads]
        for hd in heads:
            kt_ref[c, hd * DK:(hd + 1) * DK, :] = jnp.transpose(kf[hd][rws] * kd_ref[:, hd:hd + 1])
        for hd in heads:
            ob_ref[rws, hd * DV:(hd + 1) * DV] = _dot(
                (scores[hd] * dmask_ref[hd]).astype(BF16), rv[rws, hd * DV:(hd + 1) * DV])
    for hd in heads:
        q_ref[:, hd * DK:(hd + 1) * DK] = qs[hd]


def _mix_back_kernel(x_ref, g_ref, h_ref, mg_ref, ob_ref, inter_ref, wrg_ref, wgb_ref, wb_ref, wo_ref,
                     o_ref, on_scr):
    nseq, ntok, _ = x_ref.shape
    h = h_ref[...]
    for hd in range(HEADS):
        cols = slice(hd * DV, (hd + 1) * DV)
        on_scr[:, cols] = _head_rms(ob_ref[:, cols] + inter_ref[:, cols])
    ob = (on_scr[...] * _silu(_dot(h, wrg_ref[...]))).astype(BF16)
    merged = mg_ref[...] + jax.nn.sigmoid(_dot(h, wgb_ref[...])) * _dot(ob, wb_ref[...])
    out = _dot(merged.astype(BF16), wo_ref[...])
    o_ref[...] = x_ref[...] + g_ref[...] * out.reshape(nseq, ntok, D_MODEL)


def _retention_tables(seq_len):
    lc = min(seq_len, CHUNK)
    log_gamma = np.log1p(-np.power(2.0, -5.0 - np.arange(HEADS)))
    idx = np.arange(lc, dtype=np.float64)
    diff = idx[:, None] - idx[None, :]
    decay = np.where(diff >= 0, np.exp(np.maximum(diff, 0.0)[None] * log_gamma[:, None, None]), 0.0)
    q_decay = np.exp((idx + 1.0)[:, None] * log_gamma[None, :])
    k_decay = np.exp((lc - 1.0 - idx)[:, None] * log_gamma[None, :])
    s_decay = tuple(float(v) for v in np.exp(lc * log_gamma).astype(np.float32))
    rep = CHUNK // lc
    blockdiag = np.kron(np.eye(rep), np.ones((lc, lc)))
    dmask = np.tile(decay, (1, rep, rep)) * blockdiag[None]
    mmask = np.tile(np.tril(np.ones((lc, lc))), (rep, rep)) * blockdiag
    f32 = lambda a: jnp.asarray(a.astype(np.float32))
    return f32(dmask), f32(np.tile(q_decay, (rep, 1))), f32(np.tile(k_decay, (rep, 1))), s_decay, f32(mmask)


def _rotary_tables(pos, rows):
    half = DK // 2
    inv = ROPE_BASE ** (-np.arange(half, dtype=np.float64) / half)
    ang = np.asarray(pos, np.float64)[:, None] * inv[None, :]
    cos, sin = np.cos(ang), np.sin(ang)
    rep = rows // len(pos)
    cos2 = np.tile(np.concatenate([cos, cos], axis=-1), (rep, 1))
    sin2 = np.tile(np.concatenate([-sin, sin], axis=-1), (rep, 1))
    return jnp.asarray(cos2.astype(np.float32)), jnp.asarray(sin2.astype(np.float32))


def _mix_weight_specs():
    return [_const_spec((1, D_MODEL)),
            _const_spec((D_MODEL, OFF_END)),
            _const_spec((1, D_MODEL)), _const_spec((1, D_MODEL)),
            _const_spec((GROUPS, CHUNK, CHUNK)), _const_spec((CHUNK, CHUNK)),
            _const_spec((CHUNK, GROUPS))]


def _mix_out_weight_specs():
    return [_const_spec((HEADS, CHUNK, CHUNK)),
            _const_spec((CHUNK, HEADS)), _const_spec((CHUNK, HEADS)),
            _const_spec((D_MODEL, D_MODEL)), _const_spec((RET_V, D_MODEL)), _const_spec((D_MODEL, D_MODEL))]


def _mix_prompt(x, mods, mod_row0, nm_g, w_in, ln_g, ln_b, gm_ws, gm_bs, w_a, w_b, w_o, job):
    nb, seq, _ = x.shape
    tm = MIX_ROWS
    nj = seq // tm
    dmask, qd, kd, s_decay, mmask = _retention_tables(seq)
    cos2, sin2 = _rotary_tables(np.arange(seq), seq)
    xspec = pl.BlockSpec((1, tm, D_MODEL), lambda b, j: (b, j, 0))
    tspec = pl.BlockSpec((tm, DK), lambda b, j: (j, 0))
    state_shape = (1, nb, HEADS, DK, DV)

    jq, jkt, jrv, jstate, ntok = job
    nseq_total = jstate.shape[1]
    nstep = nseq_total // (nb * nj)
    seq_per_chunk = CHUNK // ntok
    assert nstep * nb * nj == nseq_total and seq_per_chunk % nstep == 0 and (nstep * ntok) % 8 == 0
    _, jqd, _, job_decay, _ = _retention_tables(ntok)
    step = lambda b, j: b * nj + j
    chunk = lambda b, j: step(b, j) * nstep // seq_per_chunk
    jsspec = pl.BlockSpec((1, nstep, HEADS, DK, DV), lambda b, j: (0, step(b, j), 0, 0, 0))
    jispec = pl.BlockSpec((nstep * ntok, RET_V), lambda b, j: (step(b, j), 0))
    job_in = [pl.BlockSpec((nstep * ntok, D_MODEL), lambda b, j: (step(b, j), 0)),
              pl.BlockSpec((1, D_MODEL, CHUNK), lambda b, j: (chunk(b, j), 0, 0)),
              pl.BlockSpec((1, CHUNK, RET_V), lambda b, j: (chunk(b, j), 0, 0)),
              _const_spec((ntok, HEADS)), jsspec]

    return pl.pallas_call(
        functools.partial(_mix_prompt_kernel, s_decay=s_decay, job_decay=job_decay),
        grid=(nb, nj),
        in_specs=([xspec] + _mod_specs(1, mod_row0, 3) + _mix_weight_specs() + [tspec, tspec]
                  + _mix_out_weight_specs() + job_in),
        out_specs=[xspec, pl.BlockSpec((1, 1, HEADS, DK, DV), lambda b, j: (0, b, 0, 0, 0)), jispec, jsspec],
        out_shape=[jax.ShapeDtypeStruct(x.shape, F32), jax.ShapeDtypeStruct(state_shape, F32),
                   jax.ShapeDtypeStruct((nseq_total * ntok, RET_V), F32), jax.ShapeDtypeStruct(jstate.shape, F32)],
        scratch_shapes=[pltpu.VMEM((tm, D_MODEL), BF16), pltpu.VMEM((tm, RET_V), F32)],
        compiler_params=pltpu.CompilerParams(
            dimension_semantics=("arbitrary", "arbitrary"), vmem_limit_bytes=VMEM_LIMIT_V7X),
        name="mix_prompt",
    )(x, mods, mods, mods, nm_g.reshape(1, D_MODEL), w_in, ln_g.reshape(1, D_MODEL), ln_b.reshape(1, D_MODEL),
      gm_ws, mmask, jnp.transpose(gm_bs), cos2, sin2, dmask, qd, kd, w_a, w_b, w_o,
      jq, jkt, jrv, jqd[:ntok], jstate)


def _mix_sample_front(x, mods, mod_row0, nm_g, w_in, ln_g, ln_b, gm_ws, gm_bs, w_a):
    nb, ntok, _ = x.shape
    rows = MIX_ROWS
    nseq = rows // ntok
    nsub = rows // CHUNK
    nrow, nchunk = nb * ntok, nb * ntok // CHUNK
    rep = CHUNK // ntok
    dmask, _, kd, _, mmask = _retention_tables(ntok)
    cos2, sin2 = _rotary_tables(PAST_LEN + np.arange(ntok), rows)
    onehot = jnp.asarray(np.tile(np.eye(ntok, dtype=np.float32), (rep, 1)))
    wmix = jnp.einsum("ra,gab,cb->grc", onehot, gm_ws[:, :ntok, :ntok], onehot, precision=lax.Precision.HIGHEST)
    bst = jnp.einsum("ra,ga->rg", onehot, gm_bs[:, :ntok], precision=lax.Precision.HIGHEST)
    xspec = pl.BlockSpec((nseq, ntok, D_MODEL), lambda i: (i, 0, 0))
    row_spec = lambda width: pl.BlockSpec((rows, width), lambda i: (i, 0))
    return pl.pallas_call(
        _mix_front_kernel,
        grid=(nb // nseq,),
        in_specs=([xspec] + _mod_specs(nseq, mod_row0, 3)[:2] + _mix_weight_specs()
                  + [_const_spec((rows, DK)), _const_spec((rows, DK)),
                     _const_spec((HEADS, CHUNK, CHUNK)), _const_spec((CHUNK, HEADS)),
                     _const_spec((D_MODEL, D_MODEL))]),
        out_specs=[pl.BlockSpec((1, nseq, ntok, D_MODEL), lambda i: (0, i, 0, 0)),
                   row_spec(D_MODEL), row_spec(D_MODEL), row_spec(D_MODEL),
                   pl.BlockSpec((nsub, D_MODEL, CHUNK), lambda i: (i, 0, 0)),
                   pl.BlockSpec((nsub, CHUNK, RET_V), lambda i: (i, 0, 0)),
                   row_spec(RET_V)],
        out_shape=[jax.ShapeDtypeStruct((1,) + x.shape, F32),
                   jax.ShapeDtypeStruct((nrow, D_MODEL), F32), jax.ShapeDtypeStruct((nrow, D_MODEL), BF16),
                   jax.ShapeDtypeStruct((nrow, D_MODEL), F32),
                   jax.ShapeDtypeStruct((nchunk, D_MODEL, CHUNK), F32),
                   jax.ShapeDtypeStruct((nchunk, CHUNK, RET_V), BF16),
                   jax.ShapeDtypeStruct((nrow, RET_V), F32)],
        scratch_shapes=[pltpu.VMEM((rows, D_MODEL), BF16)],
        compiler_params=pltpu.CompilerParams(
            dimension_semantics=("arbitrary",), vmem_limit_bytes=VMEM_LIMIT_V7X),
        name="mix_sample_front",
    )(x, mods, mods, nm_g.reshape(1, D_MODEL), w_in, ln_g.reshape(1, D_MODEL), ln_b.reshape(1, D_MODEL),
      wmix, mmask, bst, cos2, sin2, dmask, kd, w_a)


def _mix_sample_back(x, mods, mod_row0, h, merged, ob, inter, w_in, w_b, w_o):
    nb, ntok, _ = x.shape
    rows = MIX_ROWS
    nseq = rows // ntok
    xspec = pl.BlockSpec((nseq, ntok, D_MODEL), lambda i: (i, 0, 0))
    row_spec = lambda width: pl.BlockSpec((rows, width), lambda i: (i, 0))
    col_spec = lambda lo, hi: pl.BlockSpec((D_MODEL, hi - lo), lambda i: (0, lo // (hi - lo)),
                                           pipeline_mode=pl.Buffered(1))
    assert OFF_RG % (OFF_GA - OFF_RG) == 0 and OFF_GB % (OFF_END - OFF_GB) == 0
    return pl.pallas_call(
        _mix_back_kernel,
        grid=(nb // nseq,),
        in_specs=[xspec, _mod_specs(nseq, mod_row0, 3)[2], row_spec(D_MODEL), row_spec(D_MODEL),
                  row_spec(RET_V), row_spec(RET_V),
                  col_spec(OFF_RG, OFF_GA), col_spec(OFF_GB, OFF_END),
                  _const_spec((RET_V, D_MODEL)), _const_spec((D_MODEL, D_MODEL))],
        out_specs=xspec,
        out_shape=jax.ShapeDtypeStruct(x.shape, F32),
        scratch_shapes=[pltpu.VMEM((rows, RET_V), F32)],
        compiler_params=pltpu.CompilerParams(
            dimension_semantics=("arbitrary",), vmem_limit_bytes=VMEM_LIMIT_V7X),
        name="mix_sample_back",
    )(x, mods, h, merged, ob, inter, w_in, w_in, w_b, w_o)


def kernel(x_prompt, x_sample, state_ret, c_prompt, c_sample, w_ada, b_ada, n1_g, w1_gate, w1_up, w1_down,
           nm_g, w_in, gm_ln_g, gm_ln_b, gm_ws, gm_bs, w_a, w_b, w_o, n2_g, w2_gate, w2_up, w2_down, final_g):
    assert w_ada.shape[0] == 1, "single-layer step"
    nbs = x_sample.shape[0]

    mods, w1 = _ada(jnp.concatenate([c_sample, c_prompt], axis=0), w_ada[0], b_ada[0],
                    cast=(w1_gate[0], w1_up[0], w1_down[0]))
    row_s, row_p = 0, nbs

    (yp, ys), later = _ffn([(x_prompt, row_p), (x_sample, row_s)], mods, 0, n1_g[0], *w1, final_g, final=False,
                           cast=(w_in[0], w_a[0], w_b[0], w_o[0], w2_gate[0], w2_up[0], w2_down[0]))
    w_in_b, w_a_b, w_b_b, w_o_b = later[:4]
    w2 = later[4:]
    gmlp = (nm_g[0], w_in_b, gm_ln_g[0], gm_ln_b[0], gm_ws[0], gm_bs[0], w_a_b)

    vs, merged_s, h_s, q_s, kt_s, rv_s, ob_s = _mix_sample_front(ys, mods, row_s, *gmlp)
    yp, sp, inter_s, ss = _mix_prompt(yp, mods, row_p, *gmlp, w_b_b, w_o_b,
                                      job=(q_s, kt_s, rv_s, state_ret.astype(F32), x_sample.shape[1]))
    ys = _mix_sample_back(ys, mods, row_s, h_s, merged_s, ob_s, inter_s, w_in_b, w_b_b, w_o_b)

    (yp, ys), _ = _ffn([(yp, row_p), (ys, row_s)], mods, 6, n2_g[0], *w2, final_g, final=True)
    return (yp, ys, sp, ss, vs)
```

```python
import functools

import jax
import jax.numpy as jnp
import numpy as np
from jax import lax
from jax.experimental import pallas as pl
from jax.experimental.pallas import tpu as pltpu

F32 = jnp.float32
BF16 = jnp.bfloat16

D_MODEL = 1024
D_FF = 2816
N_MOD = 9
EPS = 1e-6
ROPE_BASE = 10000.0
PAST_LEN = 16384
CHUNK = 128
GROUPS = 8
GROUP_DIM = D_MODEL // GROUPS
HEADS = 8
DK = D_MODEL // HEADS
DV = 2 * DK
RET_V = HEADS * DV
OFF_U, OFF_V, OFF_Q, OFF_K, OFF_RV, OFF_RG, OFF_GA, OFF_GB, OFF_END = (
    0, 1024, 2048, 3072, 4096, 6144, 8192, 9216, 10240)

VMEM_LIMIT_V7X = 56 * 1024 * 1024
FFN_ROWS = 512
FFN_SPLIT = 2
MIX_ROWS = 256
ADA_STEPS = 8
BF16_SUBLANES = 16


def _dot(a, b):
    return jnp.dot(a, b, preferred_element_type=F32)


def _silu(x):
    return x * jax.nn.sigmoid(x)


def _gelu_tanh(x):
    c = np.float32(np.sqrt(2.0 / np.pi))
    return 0.5 * x * (1.0 + jnp.tanh(c * (x + 0.044715 * (x * x * x))))


def _rms_norm(x, gain):
    return x * lax.rsqrt(jnp.mean(x * x, axis=-1, keepdims=True) + EPS) * gain


def _const_spec(shape):
    nd = len(shape)
    return pl.BlockSpec(shape, lambda *_: (0,) * nd, pipeline_mode=pl.Buffered(1))


def _mod_specs(bb, row0, first):
    assert row0 % bb == 0
    return [pl.BlockSpec((bb, 1, D_MODEL), lambda i, *_, k=k: (row0 // bb + i, 0, k))
            for k in range(first, first + 3)]


def _cast_plan(weights, nsteps):
    in_specs, out_specs, out_shapes = [], [], []
    for w in weights:
        rows, cols = w.shape
        nblk = max(n for n in range(1, nsteps + 1) if rows % n == 0 and (rows // n) % BF16_SUBLANES == 0)
        spec = pl.BlockSpec((rows // nblk, cols), lambda step, nblk=nblk: (step * nblk // nsteps, 0))
        in_specs.append(spec)
        out_specs.append(spec)
        out_shapes.append(jax.ShapeDtypeStruct(w.shape, BF16))
    return in_specs, out_specs, out_shapes


def _cast_blocks(src_refs, dst_refs):
    for src, dst in zip(src_refs, dst_refs, strict=True):
        dst[...] = src[...].astype(BF16)


def _ada_kernel(c_ref, w_ref, b_ref, *refs):
    ncast = (len(refs) - 1) // 2
    o_ref = refs[ncast]
    s = _silu(c_ref[...]).astype(BF16)
    m = _dot(s, w_ref[...].astype(BF16)) + b_ref[...]
    for r in range(m.shape[0]):
        o_ref[r] = m[r:r + 1, :]
    _cast_blocks(refs[:ncast], refs[ncast + 1:])


def _ada(c_all, w_ada, b_ada, cast):
    rows = c_all.shape[0]
    n = w_ada.shape[1]
    grid = (ADA_STEPS,)
    bn = n // ADA_STEPS
    cast_in, cast_out, cast_shapes = _cast_plan(cast, ADA_STEPS)
    out = pl.pallas_call(
        _ada_kernel,
        grid=grid,
        in_specs=[pl.BlockSpec((rows, D_MODEL), lambda j: (0, 0)),
                  pl.BlockSpec((D_MODEL, bn), lambda j: (0, j)),
                  pl.BlockSpec((1, bn), lambda j: (0, j))] + cast_in,
        out_specs=[pl.BlockSpec((rows, 1, bn), lambda j: (0, 0, j))] + cast_out,
        out_shape=[jax.ShapeDtypeStruct((rows, 1, n), F32)] + cast_shapes,
        compiler_params=pltpu.CompilerParams(
            dimension_semantics=("arbitrary",), vmem_limit_bytes=VMEM_LIMIT_V7X),
        name="ada",
    )(c_all, w_ada, b_ada.reshape(1, n), *cast)
    return out[0], out[1:]


def _ffn_rows(x_ref, sh_ref, sc_ref, g_ref, nxt_refs, ng_ref, wg_ref, wu_ref, wd_ref, eg_ref, o_ref, h_ref):
    nb, nr, _ = x_ref.shape
    if nb > 1:
        groups = [(slice(s * nb // FFN_SPLIT, (s + 1) * nb // FFN_SPLIT), slice(None)) for s in range(FFN_SPLIT)]
    else:
        groups = [(slice(None), slice(s * nr // FFN_SPLIT, (s + 1) * nr // FFN_SPLIT)) for s in range(FFN_SPLIT)]
    xs = [x_ref[seqs, toks, :] for seqs, toks in groups]
    bb, r, _ = xs[0].shape
    hs = [(_rms_norm(x, ng_ref[...]) * (1.0 + sc_ref[seqs]) + sh_ref[seqs]).reshape(bb * r, D_MODEL).astype(BF16)
          for x, (seqs, _) in zip(xs, groups)]
    gates = [_dot(h, wg_ref[...]) for h in hs]
    ups = [_dot(h, wu_ref[...]) for h in hs]
    acts = [(_silu(gt) * up).astype(BF16) for gt, up in zip(gates, ups)]
    ys = [_dot(act, wd_ref[...]).reshape(bb, r, D_MODEL) for act in acts]
    for s, (x, y, (seqs, toks)) in enumerate(zip(xs, ys, groups)):
        out = x + (0.5 * g_ref[seqs]) * y
        if nxt_refs is None:
            o_ref[seqs, toks, :] = _rms_norm(out, eg_ref[...])
        else:
            shn_ref, scn_ref = nxt_refs
            o_ref[seqs, toks, :] = out
            hn = _rms_norm(out, eg_ref[...]) * (1.0 + scn_ref[seqs]) + shn_ref[seqs]
            h_ref[s * bb * r:(s + 1) * bb * r, :] = hn.reshape(bb * r, D_MODEL).astype(BF16)


N_FFN_WEIGHT_IN = 5


def _ffn_kernel(*refs, emit_next, batch_steps):
    nbatch = len(batch_steps)
    per_in = 6 if emit_next else 4
    per_out = 2 if emit_next else 1
    n_in = nbatch * per_in + N_FFN_WEIGHT_IN
    ncast = (len(refs) - n_in - nbatch * per_out) // 2
    weights = refs[nbatch * per_in:n_in]
    outs = refs[n_in + ncast:]
    step = pl.program_id(0)
    first = 0
    for b, nsteps in enumerate(batch_steps):
        ins = refs[b * per_in:(b + 1) * per_in]
        bouts = outs[b * per_out:(b + 1) * per_out]

        @pl.when((step >= first) & (step < first + nsteps))
        def _(ins=ins, bouts=bouts):
            _ffn_rows(*ins[:4], ins[4:] if emit_next else None, *weights,
                      bouts[0], bouts[1] if emit_next else None)

        first += nsteps
    _cast_blocks(refs[n_in:n_in + ncast], outs[nbatch * per_out:])


def _ffn(batches, mods, mod_first, norm_g, wg, wu, wd, out_gain, *, emit_next, cast=()):
    operands, in_specs, out_specs, out_shapes, batch_steps = [], [], [], [], []
    first = 0
    for x, mod_row0 in batches:
        nb, r, _ = x.shape
        br = min(r, FFN_ROWS)
        bb = FFN_ROWS // br
        nj = r // br
        nsteps = (nb // bb) * nj
        assert mod_row0 % bb == 0
        local = lambda step, first=first, nsteps=nsteps: jnp.clip(step - first, 0, nsteps - 1)
        xspec = pl.BlockSpec((bb, br, D_MODEL), lambda step, local=local, nj=nj: (local(step) // nj, local(step) % nj, 0))
        nmod = 5 if emit_next else 3
        mspecs = [pl.BlockSpec((bb, 1, D_MODEL),
                               lambda step, local=local, nj=nj, row=mod_row0 // bb, k=k: (row + local(step) // nj, 0, k))
                  for k in range(mod_first, mod_first + nmod)]
        operands += [x] + [mods] * nmod
        in_specs += [xspec] + mspecs
        out_specs.append(xspec)
        out_shapes.append(jax.ShapeDtypeStruct(x.shape, F32))
        if emit_next:
            out_specs.append(pl.BlockSpec((bb * br, D_MODEL), lambda step, local=local: (local(step), 0)))
            out_shapes.append(jax.ShapeDtypeStruct((nb * r, D_MODEL), BF16))
        batch_steps.append(nsteps)
        first += nsteps
    cast_in, cast_out, cast_shapes = _cast_plan(cast, first)
    out = pl.pallas_call(
        functools.partial(_ffn_kernel, emit_next=emit_next, batch_steps=tuple(batch_steps)),
        grid=(first,),
        in_specs=in_specs + [
            _const_spec((1, D_MODEL)),
            _const_spec((D_MODEL, D_FF)), _const_spec((D_MODEL, D_FF)), _const_spec((D_FF, D_MODEL)),
            _const_spec((1, D_MODEL))] + cast_in,
        out_specs=out_specs + cast_out,
        out_shape=out_shapes + cast_shapes,
        compiler_params=pltpu.CompilerParams(
            dimension_semantics=("arbitrary",), vmem_limit_bytes=VMEM_LIMIT_V7X),
        name="ffn" if emit_next else "ffn_final",
    )(*operands, norm_g.reshape(1, D_MODEL), wg, wu, wd, out_gain.reshape(1, D_MODEL), *cast)
    nout = len(batches) * (2 if emit_next else 1)
    ys = [tuple(out[2 * b:2 * b + 2]) for b in range(len(batches))] if emit_next else list(out[:nout])
    return ys, out[nout:]


def _layer_norm(x, gain, bias):
    mu = jnp.mean(x, axis=-1, keepdims=True)
    var = jnp.mean(jnp.square(x - mu), axis=-1, keepdims=True)
    return (x - mu) * lax.rsqrt(var + EPS) * gain + bias


def _spatial_mix(u, vb, wmix_ref, mmask_ref, bst_ref, oa_scr):
    mmask = mmask_ref[...]
    for g in range(GROUPS):
        wg = (wmix_ref[g] * mmask).astype(BF16)
        cols = slice(g * GROUP_DIM, (g + 1) * GROUP_DIM)
        for c in range(u.shape[0] // CHUNK):
            rws = slice(c * CHUNK, (c + 1) * CHUNK)
            mix = _dot(wg, vb[rws, cols]) + bst_ref[:, g:g + 1]
            oa_scr[rws, cols] = (u[rws, cols] * mix).astype(BF16)


def _gmlp_branch(h, win_ref, lng_ref, lnb_ref, wmix_ref, mmask_ref, bst_ref, wa_ref, oa_scr):
    u = _gelu_tanh(_dot(h, win_ref[:, OFF_U:OFF_V]))
    v = _layer_norm(_gelu_tanh(_dot(h, win_ref[:, OFF_V:OFF_Q])), lng_ref[...], lnb_ref[...])
    _spatial_mix(u, v.astype(BF16), wmix_ref, mmask_ref, bst_ref, oa_scr)
    pa = _dot(oa_scr[...], wa_ref[...])
    ga = _dot(h, win_ref[:, OFF_GA:OFF_GB])
    return jax.nn.sigmoid(ga) * pa, v


def _rotary_heads(z, cos2, sin2):
    out = []
    for hd in range(HEADS):
        zh = z[:, hd * DK:(hd + 1) * DK]
        out.append(zh * cos2 + pltpu.roll(zh, DK // 2, 1) * sin2)
    return out


def _head_rms(o):
    return o * lax.rsqrt(jnp.mean(o * o, axis=-1, keepdims=True) + EPS)


def _state_step(q_ref, kt_ref, rv_ref, qd_ref, s0_ref, inter_ref, snew_ref, first_seq, s_decay):
    nstep = s0_ref.shape[1]
    ntok = qd_ref.shape[0]
    seq_per_chunk = CHUNK // ntok
    for t in range(nstep):
        tok = slice(t * ntok, (t + 1) * ntok)
        in_seq = lax.broadcasted_iota(jnp.int32, (DK, CHUNK), 1) // ntok == (first_seq + t) % seq_per_chunk
        for hd in range(HEADS):
            state = s0_ref[0, t, hd]
            qj = q_ref[tok, hd * DK:(hd + 1) * DK].astype(BF16)
            inter_ref[tok, hd * DV:(hd + 1) * DV] = _dot(qj, state.astype(BF16)) * qd_ref[:, hd:hd + 1]
            kt = jnp.where(in_seq, kt_ref[0, hd * DK:(hd + 1) * DK, :], 0.0).astype(BF16)
            upd = _dot(kt, rv_ref[0, :, hd * DV:(hd + 1) * DV])
            snew_ref[0, t, hd] = s_decay[hd] * state + upd


N_MIX_IN = 17
N_JOB_IN = 5


def _mix_prompt_kernel(*refs, s_decay, job_decay):
    (x_ref, h_ref, g_ref, win_ref, lng_ref, lnb_ref, wmix_ref, mmask_ref,
     bst_ref, cos_ref, sin_ref, dmask_ref, qd_ref, kd_ref, wa_ref, wb_ref, wo_ref) = refs[:N_MIX_IN]
    rest = refs[N_MIX_IN:]
    if job_decay is not None:
        jq_ref, jkt_ref, jrv_ref, jqd_ref, js0_ref = rest[:N_JOB_IN]
        o_ref, s_ref, jinter_ref, jsnew_ref, oa_scr, ob_scr = rest[N_JOB_IN:]
    else:
        o_ref, s_ref, oa_scr, ob_scr = rest

    @pl.when(pl.program_id(1) == 0)
    def _():
        s_ref[...] = jnp.zeros(s_ref.shape, F32)

    x = x_ref[0]
    tm = x.shape[0]
    h = h_ref[...]
    proj = lambda lo, hi: _dot(h, win_ref[:, lo:hi])
    cos2, sin2 = cos_ref[...], sin_ref[...]
    zv = proj(OFF_V, OFF_Q)
    zu = proj(OFF_U, OFF_V)
    gv = _gelu_tanh(zv)
    zq = proj(OFF_Q, OFF_K)
    v = _layer_norm(gv, lng_ref[...], lnb_ref[...])
    u = _gelu_tanh(zu)
    zk = proj(OFF_K, OFF_RV)
    qs = _rotary_heads(zq, cos2, sin2)
    rv = proj(OFF_RV, OFF_RG).astype(BF16)
    ks = _rotary_heads(zk, cos2, sin2)
    _spatial_mix(u, v.astype(BF16), wmix_ref, mmask_ref, bst_ref, oa_scr)
    kscale = np.float32(DK ** -0.5)
    gate_a = jax.nn.sigmoid(proj(OFF_GA, OFF_GB))
    heads = range(HEADS)
    qb = [qs[hd].astype(BF16) for hd in heads]
    kf = [ks[hd] * kscale for hd in heads]
    kb = [kf[hd].astype(BF16) for hd in heads]
    nt = (((1,), (1,)), ((), ()))
    for c in range(tm // CHUNK):
        rws = slice(c * CHUNK, (c + 1) * CHUNK)
        vc = [rv[rws, hd * DV:(hd + 1) * DV] for hd in heads]
        state = [s_ref[0, 0, hd] for hd in heads]
        scores = [lax.dot_general(qb[hd][rws], kb[hd][rws], nt, preferred_element_type=F32) for hd in heads]
        inter = [_dot(qb[hd][rws], state[hd].astype(BF16)) for hd in heads]
        kdt = [jnp.transpose(kf[hd][rws] * kd_ref[:, hd:hd + 1]).astype(BF16) for hd in heads]
        upd = [_dot(kdt[hd], vc[hd]) for hd in heads]
        intra = [_dot((scores[hd] * dmask_ref[hd]).astype(BF16), vc[hd]) for hd in heads]
        for hd in heads:
            s_ref[0, 0, hd] = s_decay[hd] * state[hd] + upd[hd]
            ob_scr[rws, hd * DV:(hd + 1) * DV] = _head_rms(intra[hd] + inter[hd] * qd_ref[:, hd:hd + 1])
        if c == 0:
            gate_r = _silu(proj(OFF_RG, OFF_GA))
    gate_b = jax.nn.sigmoid(proj(OFF_GB, OFF_END))
    pa = _dot(oa_scr[...], wa_ref[...])
    merged = gate_a * pa + gate_b * _dot((ob_scr[...] * gate_r).astype(BF16), wb_ref[...])
    o_ref[0] = x + g_ref[0] * _dot(merged.astype(BF16), wo_ref[...])
    if job_decay is not None:
        step = pl.program_id(0) * pl.num_programs(1) + pl.program_id(1)
        _state_step(jq_ref, jkt_ref, jrv_ref, jqd_ref, js0_ref, jinter_ref, jsnew_ref,
                    step * js0_ref.shape[1], job_decay)


def _mix_front_kernel(h_ref, win_ref, lng_ref, lnb_ref, wmix_ref, mmask_ref, bst_ref,
                      cos_ref, sin_ref, dmask_ref, kd_ref, wa_ref,
                      vrow_ref, mg_ref, q_ref, kt_ref, rv_ref, ob_ref, oa_scr):
    _, nseq, ntok, _ = vrow_ref.shape
    rows = nseq * ntok
    h = h_ref[...]
    merged, v = _gmlp_branch(h, win_ref, lng_ref, lnb_ref, wmix_ref, mmask_ref, bst_ref, wa_ref, oa_scr)
    mg_ref[...] = merged
    vrow_ref[0] = v.reshape(nseq, ntok, D_MODEL)
    cos2, sin2 = cos_ref[...], sin_ref[...]
    qs = _rotary_heads(_dot(h, win_ref[:, OFF_Q:OFF_K]), cos2, sin2)
    ks = _rotary_heads(_dot(h, win_ref[:, OFF_K:OFF_RV]), cos2, sin2)
    rv = _dot(h, win_ref[:, OFF_RV:OFF_RG]).astype(BF16)
    kscale = np.float32(DK ** -0.5)
    heads = range(HEADS)
    kf = [ks[hd] * kscale for hd in heads]
    nt = (((1,), (1,)), ((), ()))
    for c in range(rows // CHUNK):
        rws = slice(c * CHUNK, (c + 1) * CHUNK)
        rv_ref[c] = rv[rws]
        scores = [lax.dot_general(qs[hd][rws].astype(BF16), kf[hd][rws].astype(BF16), nt,
                                  preferred_element_type=F32) for hd in heads]
        for hd in heads:
            kt_ref[c, hd * DK:(hd + 1) * DK, :] = jnp.transpose(kf[hd][rws] * kd_ref[:, hd:hd + 1])
        for hd in heads:
            ob_ref[rws, hd * DV:(hd + 1) * DV] = _dot(
                (scores[hd] * dmask_ref[hd]).astype(BF16), rv[rws, hd * DV:(hd + 1) * DV])
    for hd in heads:
        q_ref[:, hd * DK:(hd + 1) * DK] = qs[hd]


def _mix_back_kernel(x_ref, g_ref, h_ref, mg_ref, ob_ref, inter_ref, wrg_ref, wgb_ref, wb_ref, wo_ref,
                     o_ref, on_scr):
    nseq, ntok, _ = x_ref.shape
    h = h_ref[...]
    for hd in range(HEADS):
        cols = slice(hd * DV, (hd + 1) * DV)
        on_scr[:, cols] = _head_rms(ob_ref[:, cols] + inter_ref[:, cols])
    ob = (on_scr[...] * _silu(_dot(h, wrg_ref[...]))).astype(BF16)
    merged = mg_ref[...] + jax.nn.sigmoid(_dot(h, wgb_ref[...])) * _dot(ob, wb_ref[...])
    out = _dot(merged.astype(BF16), wo_ref[...])
    o_ref[...] = x_ref[...] + g_ref[...] * out.reshape(nseq, ntok, D_MODEL)


def _retention_tables(seq_len):
    lc = min(seq_len, CHUNK)
    log_gamma = np.log1p(-np.power(2.0, -5.0 - np.arange(HEADS)))
    idx = np.arange(lc, dtype=np.float64)
    diff = idx[:, None] - idx[None, :]
    decay = np.where(diff >= 0, np.exp(np.maximum(diff, 0.0)[None] * log_gamma[:, None, None]), 0.0)
    q_decay = np.exp((idx + 1.0)[:, None] * log_gamma[None, :])
    k_decay = np.exp((lc - 1.0 - idx)[:, None] * log_gamma[None, :])
    s_decay = tuple(float(v) for v in np.exp(lc * log_gamma).astype(np.float32))
    rep = CHUNK // lc
    blockdiag = np.kron(np.eye(rep), np.ones((lc, lc)))
    dmask = np.tile(decay, (1, rep, rep)) * blockdiag[None]
    mmask = np.tile(np.tril(np.ones((lc, lc))), (rep, rep)) * blockdiag
    f32 = lambda a: jnp.asarray(a.astype(np.float32))
    return f32(dmask), f32(np.tile(q_decay, (rep, 1))), f32(np.tile(k_decay, (rep, 1))), s_decay, f32(mmask)


def _rotary_tables(pos, rows):
    half = DK // 2
    inv = ROPE_BASE ** (-np.arange(half, dtype=np.float64) / half)
    ang = np.asarray(pos, np.float64)[:, None] * inv[None, :]
    cos, sin = np.cos(ang), np.sin(ang)
    rep = rows // len(pos)
    cos2 = np.tile(np.concatenate([cos, cos], axis=-1), (rep, 1))
    sin2 = np.tile(np.concatenate([-sin, sin], axis=-1), (rep, 1))
    return jnp.asarray(cos2.astype(np.float32)), jnp.asarray(sin2.astype(np.float32))


def _mix_weight_specs():
    return [_const_spec((D_MODEL, OFF_END)),
            _const_spec((1, D_MODEL)), _const_spec((1, D_MODEL)),
            _const_spec((GROUPS, CHUNK, CHUNK)), _const_spec((CHUNK, CHUNK)),
            _const_spec((CHUNK, GROUPS))]


def _mix_out_weight_specs():
    return [_const_spec((HEADS, CHUNK, CHUNK)),
            _const_spec((CHUNK, HEADS)), _const_spec((CHUNK, HEADS)),
            _const_spec((D_MODEL, D_MODEL)), _const_spec((RET_V, D_MODEL)), _const_spec((D_MODEL, D_MODEL))]


def _mix_prompt(x, h, mods, mod_row0, w_in, ln_g, ln_b, gm_ws, gm_bs, w_a, w_b, w_o, job):
    nb, seq, _ = x.shape
    tm = MIX_ROWS
    nj = seq // tm
    dmask, qd, kd, s_decay, mmask = _retention_tables(seq)
    cos2, sin2 = _rotary_tables(np.arange(seq), seq)
    xspec = pl.BlockSpec((1, tm, D_MODEL), lambda b, j: (b, j, 0))
    hspec = pl.BlockSpec((tm, D_MODEL), lambda b, j: (b * nj + j, 0))
    tspec = pl.BlockSpec((tm, DK), lambda b, j: (j, 0))
    state_shape = (1, nb, HEADS, DK, DV)

    jq, jkt, jrv, jstate, ntok = job
    nseq_total = jstate.shape[1]
    nstep = nseq_total // (nb * nj)
    seq_per_chunk = CHUNK // ntok
    assert nstep * nb * nj == nseq_total and seq_per_chunk % nstep == 0 and (nstep * ntok) % 8 == 0
    _, jqd, _, job_decay, _ = _retention_tables(ntok)
    step = lambda b, j: b * nj + j
    chunk = lambda b, j: step(b, j) * nstep // seq_per_chunk
    jsspec = pl.BlockSpec((1, nstep, HEADS, DK, DV), lambda b, j: (0, step(b, j), 0, 0, 0))
    jispec = pl.BlockSpec((nstep * ntok, RET_V), lambda b, j: (step(b, j), 0))
    job_in = [pl.BlockSpec((nstep * ntok, D_MODEL), lambda b, j: (step(b, j), 0)),
              pl.BlockSpec((1, D_MODEL, CHUNK), lambda b, j: (chunk(b, j), 0, 0)),
              pl.BlockSpec((1, CHUNK, RET_V), lambda b, j: (chunk(b, j), 0, 0)),
              _const_spec((ntok, HEADS)), jsspec]

    return pl.pallas_call(
        functools.partial(_mix_prompt_kernel, s_decay=s_decay, job_decay=job_decay),
        grid=(nb, nj),
        in_specs=([xspec, hspec, _mod_specs(1, mod_row0, 3)[2]] + _mix_weight_specs() + [tspec, tspec]
                  + _mix_out_weight_specs() + job_in),
        out_specs=[xspec, pl.BlockSpec((1, 1, HEADS, DK, DV), lambda b, j: (0, b, 0, 0, 0)), jispec, jsspec],
        out_shape=[jax.ShapeDtypeStruct(x.shape, F32), jax.ShapeDtypeStruct(state_shape, F32),
                   jax.ShapeDtypeStruct((nseq_total * ntok, RET_V), F32), jax.ShapeDtypeStruct(jstate.shape, F32)],
        scratch_shapes=[pltpu.VMEM((tm, D_MODEL), BF16), pltpu.VMEM((tm, RET_V), F32)],
        compiler_params=pltpu.CompilerParams(
            dimension_semantics=("arbitrary", "arbitrary"), vmem_limit_bytes=VMEM_LIMIT_V7X),
        name="mix_prompt",
    )(x, h, mods, w_in, ln_g.reshape(1, D_MODEL), ln_b.reshape(1, D_MODEL),
      gm_ws, mmask, jnp.transpose(gm_bs), cos2, sin2, dmask, qd, kd, w_a, w_b, w_o,
      jq, jkt, jrv, jqd[:ntok], jstate)


def _mix_sample_front(h, nb, ntok, w_in, ln_g, ln_b, gm_ws, gm_bs, w_a):
    rows = MIX_ROWS
    nseq = rows // ntok
    nsub = rows // CHUNK
    nrow, nchunk = nb * ntok, nb * ntok // CHUNK
    rep = CHUNK // ntok
    dmask, _, kd, _, mmask = _retention_tables(ntok)
    cos2, sin2 = _rotary_tables(PAST_LEN + np.arange(ntok), rows)
    onehot = jnp.asarray(np.tile(np.eye(ntok, dtype=np.float32), (rep, 1)))
    wmix = jnp.einsum("ra,gab,cb->grc", onehot, gm_ws[:, :ntok, :ntok], onehot, precision=lax.Precision.HIGHEST)
    bst = jnp.einsum("ra,ga->rg", onehot, gm_bs[:, :ntok], precision=lax.Precision.HIGHEST)
    row_spec = lambda width: pl.BlockSpec((rows, width), lambda i: (i, 0))
    return pl.pallas_call(
        _mix_front_kernel,
        grid=(nb // nseq,),
        in_specs=([row_spec(D_MODEL)] + _mix_weight_specs()
                  + [_const_spec((rows, DK)), _const_spec((rows, DK)),
                     _const_spec((HEADS, CHUNK, CHUNK)), _const_spec((CHUNK, HEADS)),
                     _const_spec((D_MODEL, D_MODEL))]),
        out_specs=[pl.BlockSpec((1, nseq, ntok, D_MODEL), lambda i: (0, i, 0, 0)),
                   row_spec(D_MODEL), row_spec(D_MODEL),
                   pl.BlockSpec((nsub, D_MODEL, CHUNK), lambda i: (i, 0, 0)),
                   pl.BlockSpec((nsub, CHUNK, RET_V), lambda i: (i, 0, 0)),
                   row_spec(RET_V)],
        out_shape=[jax.ShapeDtypeStruct((1, nb, ntok, D_MODEL), F32),
                   jax.ShapeDtypeStruct((nrow, D_MODEL), F32),
                   jax.ShapeDtypeStruct((nrow, D_MODEL), F32),
                   jax.ShapeDtypeStruct((nchunk, D_MODEL, CHUNK), F32),
                   jax.ShapeDtypeStruct((nchunk, CHUNK, RET_V), BF16),
                   jax.ShapeDtypeStruct((nrow, RET_V), F32)],
        scratch_shapes=[pltpu.VMEM((rows, D_MODEL), BF16)],
        compiler_params=pltpu.CompilerParams(
            dimension_semantics=("arbitrary",), vmem_limit_bytes=VMEM_LIMIT_V7X),
        name="mix_sample_front",
    )(h, w_in, ln_g.reshape(1, D_MODEL), ln_b.reshape(1, D_MODEL),
      wmix, mmask, bst, cos2, sin2, dmask, kd, w_a)


def _mix_sample_back(x, mods, mod_row0, h, merged, ob, inter, w_in, w_b, w_o):
    nb, ntok, _ = x.shape
    rows = MIX_ROWS
    nseq = rows // ntok
    xspec = pl.BlockSpec((nseq, ntok, D_MODEL), lambda i: (i, 0, 0))
    row_spec = lambda width: pl.BlockSpec((rows, width), lambda i: (i, 0))
    col_spec = lambda lo, hi: pl.BlockSpec((D_MODEL, hi - lo), lambda i: (0, lo // (hi - lo)),
                                           pipeline_mode=pl.Buffered(1))
    assert OFF_RG % (OFF_GA - OFF_RG) == 0 and OFF_GB % (OFF_END - OFF_GB) == 0
    return pl.pallas_call(
        _mix_back_kernel,
        grid=(nb // nseq,),
        in_specs=[xspec, _mod_specs(nseq, mod_row0, 3)[2], row_spec(D_MODEL), row_spec(D_MODEL),
                  row_spec(RET_V), row_spec(RET_V),
                  col_spec(OFF_RG, OFF_GA), col_spec(OFF_GB, OFF_END),
                  _const_spec((RET_V, D_MODEL)), _const_spec((D_MODEL, D_MODEL))],
        out_specs=xspec,
        out_shape=jax.ShapeDtypeStruct(x.shape, F32),
        scratch_shapes=[pltpu.VMEM((rows, RET_V), F32)],
        compiler_params=pltpu.CompilerParams(
            dimension_semantics=("arbitrary",), vmem_limit_bytes=VMEM_LIMIT_V7X),
        name="mix_sample_back",
    )(x, mods, h, merged, ob, inter, w_in, w_in, w_b, w_o)


def kernel(x_prompt, x_sample, state_ret, c_prompt, c_sample, w_ada, b_ada, n1_g, w1_gate, w1_up, w1_down,
           nm_g, w_in, gm_ln_g, gm_ln_b, gm_ws, gm_bs, w_a, w_b, w_o, n2_g, w2_gate, w2_up, w2_down, final_g):
    assert w_ada.shape[0] == 1, "single-layer step"
    nbs = x_sample.shape[0]

    mods, w1 = _ada(jnp.concatenate([c_sample, c_prompt], axis=0), w_ada[0], b_ada[0],
                    cast=(w1_gate[0], w1_up[0], w1_down[0]))
    row_s, row_p = 0, nbs

    ((yp, hp), (ys, hs)), later = _ffn(
        [(x_prompt, row_p), (x_sample, row_s)], mods, 0, n1_g[0], *w1, nm_g[0], emit_next=True,
        cast=(w_in[0], w_a[0], w_b[0], w_o[0], w2_gate[0], w2_up[0], w2_down[0]))
    w_in_b, w_a_b, w_b_b, w_o_b = later[:4]
    w2 = later[4:]
    gmlp = (w_in_b, gm_ln_g[0], gm_ln_b[0], gm_ws[0], gm_bs[0], w_a_b)

    nbs, ntok, _ = x_sample.shape
    vs, merged_s, q_s, kt_s, rv_s, ob_s = _mix_sample_front(hs, nbs, ntok, *gmlp)
    yp, sp, inter_s, ss = _mix_prompt(yp, hp, mods, row_p, *gmlp, w_b_b, w_o_b,
                                      job=(q_s, kt_s, rv_s, state_ret.astype(F32), ntok))
    ys = _mix_sample_back(ys, mods, row_s, hs, merged_s, ob_s, inter_s, w_in_b, w_b_b, w_o_b)

    (yp, ys), _ = _ffn([(yp, row_p), (ys, row_s)], mods, 6, n2_g[0], *w2, final_g, emit_next=False)
    return (yp, ys, sp, ss, vs)
```

```python
import functools

import jax
import jax.numpy as jnp
import numpy as np
from jax import lax
from jax.experimental import pallas as pl
from jax.experimental.pallas import tpu as pltpu

F32 = jnp.float32
BF16 = jnp.bfloat16

D_MODEL = 1024
D_FF = 2816
N_MOD = 9
EPS = 1e-6
ROPE_BASE = 10000.0
PAST_LEN = 16384
CHUNK = 128
GROUPS = 8
GROUP_DIM = D_MODEL // GROUPS
HEADS = 8
DK = D_MODEL // HEADS
DV = 2 * DK
RET_V = HEADS * DV
OFF_U, OFF_V, OFF_Q, OFF_K, OFF_RV, OFF_RG, OFF_GA, OFF_GB, OFF_END = (
    0, 1024, 2048, 3072, 4096, 6144, 8192, 9216, 10240)

VMEM_LIMIT_V7X = 56 * 1024 * 1024
FFN_ROWS = 512
FFN_FINAL_ROWS = 1024
FFN_GROUP_ROWS = 256
MIX_ROWS = 256
ADA_STEPS = 8
BF16_SUBLANES = 16


def _dot(a, b):
    return jnp.dot(a, b, preferred_element_type=F32)


def _silu(x):
    return x * jax.nn.sigmoid(x)


def _gelu_tanh(x):
    c = np.float32(np.sqrt(2.0 / np.pi))
    return 0.5 * x * (1.0 + jnp.tanh(c * (x + 0.044715 * (x * x * x))))


def _rms_norm(x, gain):
    return x * lax.rsqrt(jnp.mean(x * x, axis=-1, keepdims=True) + EPS) * gain


def _const_spec(shape):
    nd = len(shape)
    return pl.BlockSpec(shape, lambda *_: (0,) * nd, pipeline_mode=pl.Buffered(1))


def _mod_specs(bb, row0, first):
    assert row0 % bb == 0
    return [pl.BlockSpec((bb, 1, D_MODEL), lambda i, *_, k=k: (row0 // bb + i, 0, k))
            for k in range(first, first + 3)]


def _cast_plan(weights, nsteps):
    in_specs, out_specs, out_shapes = [], [], []
    for w in weights:
        rows, cols = w.shape
        nblk = max(n for n in range(1, nsteps + 1) if rows % n == 0 and (rows // n) % BF16_SUBLANES == 0)
        spec = pl.BlockSpec((rows // nblk, cols), lambda step, nblk=nblk: (step * nblk // nsteps, 0))
        in_specs.append(spec)
        out_specs.append(spec)
        out_shapes.append(jax.ShapeDtypeStruct(w.shape, BF16))
    return in_specs, out_specs, out_shapes


def _cast_blocks(src_refs, dst_refs):
    for src, dst in zip(src_refs, dst_refs, strict=True):
        dst[...] = src[...].astype(BF16)


def _ada_kernel(c_ref, w_ref, b_ref, *refs):
    ncast = (len(refs) - 1) // 2
    o_ref = refs[ncast]
    s = _silu(c_ref[...]).astype(BF16)
    m = _dot(s, w_ref[...].astype(BF16)) + b_ref[...]
    for r in range(m.shape[0]):
        o_ref[r] = m[r:r + 1, :]
    _cast_blocks(refs[:ncast], refs[ncast + 1:])


def _ada(c_all, w_ada, b_ada, cast):
    rows = c_all.shape[0]
    n = w_ada.shape[1]
    grid = (ADA_STEPS,)
    bn = n // ADA_STEPS
    cast_in, cast_out, cast_shapes = _cast_plan(cast, ADA_STEPS)
    out = pl.pallas_call(
        _ada_kernel,
        grid=grid,
        in_specs=[pl.BlockSpec((rows, D_MODEL), lambda j: (0, 0)),
                  pl.BlockSpec((D_MODEL, bn), lambda j: (0, j)),
                  pl.BlockSpec((1, bn), lambda j: (0, j))] + cast_in,
        out_specs=[pl.BlockSpec((rows, 1, bn), lambda j: (0, 0, j))] + cast_out,
        out_shape=[jax.ShapeDtypeStruct((rows, 1, n), F32)] + cast_shapes,
        compiler_params=pltpu.CompilerParams(
            dimension_semantics=("arbitrary",), vmem_limit_bytes=VMEM_LIMIT_V7X),
        name="ada",
    )(c_all, w_ada, b_ada.reshape(1, n), *cast)
    return out[0], out[1:]


def _ffn_rows(x_ref, sh_ref, sc_ref, g_ref, nxt_refs, ng_ref, wg_ref, wu_ref, wd_ref, eg_ref, o_ref, h_ref):
    nb, nr, _ = x_ref.shape
    split = nb * nr // FFN_GROUP_ROWS
    if nb > 1:
        groups = [(slice(s * nb // split, (s + 1) * nb // split), slice(None)) for s in range(split)]
    else:
        groups = [(slice(None), slice(s * nr // split, (s + 1) * nr // split)) for s in range(split)]
    for s, (seqs, toks) in enumerate(groups):
        x = x_ref[seqs, toks, :]
        bb, r, _ = x.shape
        h = (_rms_norm(x, ng_ref[...]) * (1.0 + sc_ref[seqs]) + sh_ref[seqs]).reshape(bb * r, D_MODEL).astype(BF16)
        act = (_silu(_dot(h, wg_ref[...])) * _dot(h, wu_ref[...])).astype(BF16)
        y = _dot(act, wd_ref[...]).reshape(bb, r, D_MODEL)
        out = x + (0.5 * g_ref[seqs]) * y
        if nxt_refs is None:
            o_ref[seqs, toks, :] = _rms_norm(out, eg_ref[...])
        else:
            shn_ref, scn_ref = nxt_refs
            o_ref[seqs, toks, :] = out
            hn = _rms_norm(out, eg_ref[...]) * (1.0 + scn_ref[seqs]) + shn_ref[seqs]
            h_ref[s * bb * r:(s + 1) * bb * r, :] = hn.reshape(bb * r, D_MODEL).astype(BF16)


N_FFN_WEIGHT_IN = 5


def _ffn_kernel(*refs, emit_next, batch_steps):
    nbatch = len(batch_steps)
    per_in = 6 if emit_next else 4
    per_out = 2 if emit_next else 1
    n_in = nbatch * per_in + N_FFN_WEIGHT_IN
    ncast = (len(refs) - n_in - nbatch * per_out) // 2
    weights = refs[nbatch * per_in:n_in]
    outs = refs[n_in + ncast:]
    step = pl.program_id(0)
    first = 0
    for b, nsteps in enumerate(batch_steps):
        ins = refs[b * per_in:(b + 1) * per_in]
        bouts = outs[b * per_out:(b + 1) * per_out]

        @pl.when((step >= first) & (step < first + nsteps))
        def _(ins=ins, bouts=bouts):
            _ffn_rows(*ins[:4], ins[4:] if emit_next else None, *weights,
                      bouts[0], bouts[1] if emit_next else None)

        first += nsteps
    _cast_blocks(refs[n_in:n_in + ncast], outs[nbatch * per_out:])


def _ffn(batches, mods, mod_first, norm_g, wg, wu, wd, out_gain, *, emit_next, cast=()):
    operands, in_specs, out_specs, out_shapes, batch_steps = [], [], [], [], []
    first = 0
    for x, mod_row0 in batches:
        nb, r, _ = x.shape
        rows = min(FFN_ROWS if emit_next else FFN_FINAL_ROWS, nb * r // 2)
        br = min(r, rows)
        bb = rows // br
        nj = r // br
        nsteps = (nb // bb) * nj
        assert mod_row0 % bb == 0
        local = lambda step, first=first, nsteps=nsteps: jnp.clip(step - first, 0, nsteps - 1)
        xspec = pl.BlockSpec((bb, br, D_MODEL), lambda step, local=local, nj=nj: (local(step) // nj, local(step) % nj, 0))
        nmod = 5 if emit_next else 3
        mspecs = [pl.BlockSpec((bb, 1, D_MODEL),
                               lambda step, local=local, nj=nj, row=mod_row0 // bb, k=k: (row + local(step) // nj, 0, k))
                  for k in range(mod_first, mod_first + nmod)]
        operands += [x] + [mods] * nmod
        in_specs += [xspec] + mspecs
        out_specs.append(xspec)
        out_shapes.append(jax.ShapeDtypeStruct(x.shape, F32))
        if emit_next:
            out_specs.append(pl.BlockSpec((bb * br, D_MODEL), lambda step, local=local: (local(step), 0)))
            out_shapes.append(jax.ShapeDtypeStruct((nb * r, D_MODEL), BF16))
        batch_steps.append(nsteps)
        first += nsteps
    cast_in, cast_out, cast_shapes = _cast_plan(cast, first)
    out = pl.pallas_call(
        functools.partial(_ffn_kernel, emit_next=emit_next, batch_steps=tuple(batch_steps)),
        grid=(first,),
        in_specs=in_specs + [
            _const_spec((1, D_MODEL)),
            _const_spec((D_MODEL, D_FF)), _const_spec((D_MODEL, D_FF)), _const_spec((D_FF, D_MODEL)),
            _const_spec((1, D_MODEL))] + cast_in,
        out_specs=out_specs + cast_out,
        out_shape=out_shapes + cast_shapes,
        compiler_params=pltpu.CompilerParams(
            dimension_semantics=("arbitrary",), vmem_limit_bytes=VMEM_LIMIT_V7X),
        name="ffn" if emit_next else "ffn_final",
    )(*operands, norm_g.reshape(1, D_MODEL), wg, wu, wd, out_gain.reshape(1, D_MODEL), *cast)
    nout = len(batches) * (2 if emit_next else 1)
    ys = [tuple(out[2 * b:2 * b + 2]) for b in range(len(batches))] if emit_next else list(out[:nout])
    return ys, out[nout:]


def _layer_norm(x, gain, bias):
    mu = jnp.mean(x, axis=-1, keepdims=True)
    var = jnp.mean(jnp.square(x - mu), axis=-1, keepdims=True)
    return (x - mu) * lax.rsqrt(var + EPS) * gain + bias


def _spatial_mix(u, vb, wmix_ref, mmask_ref, bst_ref, oa_scr):
    mmask = mmask_ref[...]
    for g in range(GROUPS):
        wg = (wmix_ref[g] * mmask).astype(BF16)
        cols = slice(g * GROUP_DIM, (g + 1) * GROUP_DIM)
        for c in range(u.shape[0] // CHUNK):
            rws = slice(c * CHUNK, (c + 1) * CHUNK)
            mix = _dot(wg, vb[rws, cols]) + bst_ref[:, g:g + 1]
            oa_scr[rws, cols] = (u[rws, cols] * mix).astype(BF16)


def _gmlp_branch(h, win_ref, lng_ref, lnb_ref, wmix_ref, mmask_ref, bst_ref, wa_ref, oa_scr):
    u = _gelu_tanh(_dot(h, win_ref[:, OFF_U:OFF_V]))
    v = _layer_norm(_gelu_tanh(_dot(h, win_ref[:, OFF_V:OFF_Q])), lng_ref[...], lnb_ref[...])
    _spatial_mix(u, v.astype(BF16), wmix_ref, mmask_ref, bst_ref, oa_scr)
    pa = _dot(oa_scr[...], wa_ref[...])
    ga = _dot(h, win_ref[:, OFF_GA:OFF_GB])
    return jax.nn.sigmoid(ga) * pa, v


def _rotary_heads(z, cos2, sin2):
    out = []
    for hd in range(HEADS):
        zh = z[:, hd * DK:(hd + 1) * DK]
        out.append(zh * cos2 + pltpu.roll(zh, DK // 2, 1) * sin2)
    return out


def _head_rms(o):
    return o * lax.rsqrt(jnp.mean(o * o, axis=-1, keepdims=True) + EPS)


def _state_step(q_ref, kt_ref, rv_ref, qd_ref, s0_ref, inter_ref, snew_ref, first_seq, s_decay):
    nstep = s0_ref.shape[1]
    ntok = qd_ref.shape[0]
    seq_per_chunk = CHUNK // ntok
    for t in range(nstep):
        tok = slice(t * ntok, (t + 1) * ntok)
        in_seq = lax.broadcasted_iota(jnp.int32, (DK, CHUNK), 1) // ntok == (first_seq + t) % seq_per_chunk
        for hd in range(HEADS):
            state = s0_ref[0, t, hd]
            qj = q_ref[tok, hd * DK:(hd + 1) * DK].astype(BF16)
            inter_ref[tok, hd * DV:(hd + 1) * DV] = _dot(qj, state.astype(BF16)) * qd_ref[:, hd:hd + 1]
            kt = jnp.where(in_seq, kt_ref[0, hd * DK:(hd + 1) * DK, :], 0.0).astype(BF16)
            upd = _dot(kt, rv_ref[0, :, hd * DV:(hd + 1) * DV])
            snew_ref[0, t, hd] = s_decay[hd] * state + upd


N_MIX_IN = 17
N_JOB_IN = 5


def _mix_prompt_kernel(*refs, s_decay, job_decay):
    (x_ref, h_ref, g_ref, win_ref, lng_ref, lnb_ref, wmix_ref, mmask_ref,
     bst_ref, cos_ref, sin_ref, dmask_ref, qd_ref, kd_ref, wa_ref, wb_ref, wo_ref) = refs[:N_MIX_IN]
    rest = refs[N_MIX_IN:]
    if job_decay is not None:
        jq_ref, jkt_ref, jrv_ref, jqd_ref, js0_ref = rest[:N_JOB_IN]
        o_ref, s_ref, jinter_ref, jsnew_ref, oa_scr, ob_scr = rest[N_JOB_IN:]
    else:
        o_ref, s_ref, oa_scr, ob_scr = rest

    @pl.when(pl.program_id(1) == 0)
    def _():
        s_ref[...] = jnp.zeros(s_ref.shape, F32)

    x = x_ref[0]
    tm = x.shape[0]
    h = h_ref[...]
    proj = lambda lo, hi: _dot(h, win_ref[:, lo:hi])
    cos2, sin2 = cos_ref[...], sin_ref[...]
    zv = proj(OFF_V, OFF_Q)
    zu = proj(OFF_U, OFF_V)
    gv = _gelu_tanh(zv)
    zq = proj(OFF_Q, OFF_K)
    v = _layer_norm(gv, lng_ref[...], lnb_ref[...])
    u = _gelu_tanh(zu)
    zk = proj(OFF_K, OFF_RV)
    qs = _rotary_heads(zq, cos2, sin2)
    rv = proj(OFF_RV, OFF_RG).astype(BF16)
    ks = _rotary_heads(zk, cos2, sin2)
    _spatial_mix(u, v.astype(BF16), wmix_ref, mmask_ref, bst_ref, oa_scr)
    kscale = np.float32(DK ** -0.5)
    gate_a = jax.nn.sigmoid(proj(OFF_GA, OFF_GB))
    heads = range(HEADS)
    qb = [qs[hd].astype(BF16) for hd in heads]
    kf = [ks[hd] * kscale for hd in heads]
    kb = [kf[hd].astype(BF16) for hd in heads]
    nt = (((1,), (1,)), ((), ()))
    for c in range(tm // CHUNK):
        rws = slice(c * CHUNK, (c + 1) * CHUNK)
        vc = [rv[rws, hd * DV:(hd + 1) * DV] for hd in heads]
        state = [s_ref[0, 0, hd] for hd in heads]
        scores = [lax.dot_general(qb[hd][rws], kb[hd][rws], nt, preferred_element_type=F32) for hd in heads]
        inter = [_dot(qb[hd][rws], state[hd].astype(BF16)) for hd in heads]
        kdt = [jnp.transpose(kf[hd][rws] * kd_ref[:, hd:hd + 1]).astype(BF16) for hd in heads]
        upd = [_dot(kdt[hd], vc[hd]) for hd in heads]
        intra = [_dot((scores[hd] * dmask_ref[hd]).astype(BF16), vc[hd]) for hd in heads]
        for hd in heads:
            s_ref[0, 0, hd] = s_decay[hd] * state[hd] + upd[hd]
            ob_scr[rws, hd * DV:(hd + 1) * DV] = _head_rms(intra[hd] + inter[hd] * qd_ref[:, hd:hd + 1])
        if c == 0:
            gate_r = _silu(proj(OFF_RG, OFF_GA))
    gate_b = jax.nn.sigmoid(proj(OFF_GB, OFF_END))
    pa = _dot(oa_scr[...], wa_ref[...])
    merged = gate_a * pa + gate_b * _dot((ob_scr[...] * gate_r).astype(BF16), wb_ref[...])
    o_ref[0] = x + g_ref[0] * _dot(merged.astype(BF16), wo_ref[...])
    if job_decay is not None:
        step = pl.program_id(0) * pl.num_programs(1) + pl.program_id(1)
        _state_step(jq_ref, jkt_ref, jrv_ref, jqd_ref, js0_ref, jinter_ref, jsnew_ref,
                    step * js0_ref.shape[1], job_decay)


def _mix_front_kernel(h_ref, win_ref, lng_ref, lnb_ref, wmix_ref, mmask_ref, bst_ref,
                      cos_ref, sin_ref, dmask_ref, kd_ref, wa_ref,
                      vrow_ref, mg_ref, q_ref, kt_ref, rv_ref, ob_ref, oa_scr):
    _, nseq, ntok, _ = vrow_ref.shape
    rows = nseq * ntok
    h = h_ref[...]
    merged, v = _gmlp_branch(h, win_ref, lng_ref, lnb_ref, wmix_ref, mmask_ref, bst_ref, wa_ref, oa_scr)
    mg_ref[...] = merged
    vrow_ref[0] = v.reshape(nseq, ntok, D_MODEL)
    cos2, sin2 = cos_ref[...], sin_ref[...]
    qs = _rotary_heads(_dot(h, win_ref[:, OFF_Q:OFF_K]), cos2, sin2)
    ks = _rotary_heads(_dot(h, win_ref[:, OFF_K:OFF_RV]), cos2, sin2)
    rv = _dot(h, win_ref[:, OFF_RV:OFF_RG]).astype(BF16)
    kscale = np.float32(DK ** -0.5)
    heads = range(HEADS)
    kf = [ks[hd] * kscale for hd in heads]
    nt = (((1,), (1,)), ((), ()))
    for c in range(rows // CHUNK):
        rws = slice(c * CHUNK, (c + 1) * CHUNK)
        rv_ref[c] = rv[rws]
        scores = [lax.dot_general(qs[hd][rws].astype(BF16), kf[hd][rws].astype(BF16), nt,
                                  preferred_element_type=F32) for hd in heads]
        for hd in heads:
            kt_ref[c, hd * DK:(hd + 1) * DK, :] = jnp.transpose(kf[hd][rws] * kd_ref[:, hd:hd + 1])
        for hd in heads:
            ob_ref[rws, hd * DV:(hd + 1) * DV] = _dot(
                (scores[hd] * dmask_ref[hd]).astype(BF16), rv[rws, hd * DV:(hd + 1) * DV])
    for hd in heads:
        q_ref[:, hd * DK:(hd + 1) * DK] = qs[hd]


def _mix_back_kernel(x_ref, g_ref, h_ref, mg_ref, ob_ref, inter_ref, wrg_ref, wgb_ref, wb_ref, wo_ref,
                     o_ref, on_scr):
    nseq, ntok, _ = x_ref.shape
    h = h_ref[...]
    for hd in range(HEADS):
        cols = slice(hd * DV, (hd + 1) * DV)
        on_scr[:, cols] = _head_rms(ob_ref[:, cols] + inter_ref[:, cols])
    ob = (on_scr[...] * _silu(_dot(h, wrg_ref[...]))).astype(BF16)
    merged = mg_ref[...] + jax.nn.sigmoid(_dot(h, wgb_ref[...])) * _dot(ob, wb_ref[...])
    out = _dot(merged.astype(BF16), wo_ref[...])
    o_ref[...] = x_ref[...] + g_ref[...] * out.reshape(nseq, ntok, D_MODEL)


def _retention_tables(seq_len):
    lc = min(seq_len, CHUNK)
    log_gamma = np.log1p(-np.power(2.0, -5.0 - np.arange(HEADS)))
    idx = np.arange(lc, dtype=np.float64)
    diff = idx[:, None] - idx[None, :]
    decay = np.where(diff >= 0, np.exp(np.maximum(diff, 0.0)[None] * log_gamma[:, None, None]), 0.0)
    q_decay = np.exp((idx + 1.0)[:, None] * log_gamma[None, :])
    k_decay = np.exp((lc - 1.0 - idx)[:, None] * log_gamma[None, :])
    s_decay = tuple(float(v) for v in np.exp(lc * log_gamma).astype(np.float32))
    rep = CHUNK // lc
    blockdiag = np.kron(np.eye(rep), np.ones((lc, lc)))
    dmask = np.tile(decay, (1, rep, rep)) * blockdiag[None]
    mmask = np.tile(np.tril(np.ones((lc, lc))), (rep, rep)) * blockdiag
    f32 = lambda a: jnp.asarray(a.astype(np.float32))
    return f32(dmask), f32(np.tile(q_decay, (rep, 1))), f32(np.tile(k_decay, (rep, 1))), s_decay, f32(mmask)


def _rotary_tables(pos, rows):
    half = DK // 2
    inv = ROPE_BASE ** (-np.arange(half, dtype=np.float64) / half)
    ang = np.asarray(pos, np.float64)[:, None] * inv[None, :]
    cos, sin = np.cos(ang), np.sin(ang)
    rep = rows // len(pos)
    cos2 = np.tile(np.concatenate([cos, cos], axis=-1), (rep, 1))
    sin2 = np.tile(np.concatenate([-sin, sin], axis=-1), (rep, 1))
    return jnp.asarray(cos2.astype(np.float32)), jnp.asarray(sin2.astype(np.float32))


def _mix_weight_specs():
    return [_const_spec((D_MODEL, OFF_END)),
            _const_spec((1, D_MODEL)), _const_spec((1, D_MODEL)),
            _const_spec((GROUPS, CHUNK, CHUNK)), _const_spec((CHUNK, CHUNK)),
            _const_spec((CHUNK, GROUPS))]


def _mix_out_weight_specs():
    return [_const_spec((HEADS, CHUNK, CHUNK)),
            _const_spec((CHUNK, HEADS)), _const_spec((CHUNK, HEADS)),
            _const_spec((D_MODEL, D_MODEL)), _const_spec((RET_V, D_MODEL)), _const_spec((D_MODEL, D_MODEL))]


def _mix_prompt(x, h, mods, mod_row0, w_in, ln_g, ln_b, gm_ws, gm_bs, w_a, w_b, w_o, job):
    nb, seq, _ = x.shape
    tm = MIX_ROWS
    nj = seq // tm
    dmask, qd, kd, s_decay, mmask = _retention_tables(seq)
    cos2, sin2 = _rotary_tables(np.arange(seq), seq)
    xspec = pl.BlockSpec((1, tm, D_MODEL), lambda b, j: (b, j, 0))
    hspec = pl.BlockSpec((tm, D_MODEL), lambda b, j: (b * nj + j, 0))
    tspec = pl.BlockSpec((tm, DK), lambda b, j: (j, 0))
    state_shape = (1, nb, HEADS, DK, DV)

    jq, jkt, jrv, jstate, ntok = job
    nseq_total = jstate.shape[1]
    nstep = nseq_total // (nb * nj)
    seq_per_chunk = CHUNK // ntok
    assert nstep * nb * nj == nseq_total and seq_per_chunk % nstep == 0 and (nstep * ntok) % 8 == 0
    _, jqd, _, job_decay, _ = _retention_tables(ntok)
    step = lambda b, j: b * nj + j
    chunk = lambda b, j: step(b, j) * nstep // seq_per_chunk
    jsspec = pl.BlockSpec((1, nstep, HEADS, DK, DV), lambda b, j: (0, step(b, j), 0, 0, 0))
    jispec = pl.BlockSpec((nstep * ntok, RET_V), lambda b, j: (step(b, j), 0))
    job_in = [pl.BlockSpec((nstep * ntok, D_MODEL), lambda b, j: (step(b, j), 0)),
              pl.BlockSpec((1, D_MODEL, CHUNK), lambda b, j: (chunk(b, j), 0, 0)),
              pl.BlockSpec((1, CHUNK, RET_V), lambda b, j: (chunk(b, j), 0, 0)),
              _const_spec((ntok, HEADS)), jsspec]

    return pl.pallas_call(
        functools.partial(_mix_prompt_kernel, s_decay=s_decay, job_decay=job_decay),
        grid=(nb, nj),
        in_specs=([xspec, hspec, _mod_specs(1, mod_row0, 3)[2]] + _mix_weight_specs() + [tspec, tspec]
                  + _mix_out_weight_specs() + job_in),
        out_specs=[xspec, pl.BlockSpec((1, 1, HEADS, DK, DV), lambda b, j: (0, b, 0, 0, 0)), jispec, jsspec],
        out_shape=[jax.ShapeDtypeStruct(x.shape, F32), jax.ShapeDtypeStruct(state_shape, F32),
                   jax.ShapeDtypeStruct((nseq_total * ntok, RET_V), F32), jax.ShapeDtypeStruct(jstate.shape, F32)],
        scratch_shapes=[pltpu.VMEM((tm, D_MODEL), BF16), pltpu.VMEM((tm, RET_V), F32)],
        compiler_params=pltpu.CompilerParams(
            dimension_semantics=("arbitrary", "arbitrary"), vmem_limit_bytes=VMEM_LIMIT_V7X),
        name="mix_prompt",
    )(x, h, mods, w_in, ln_g.reshape(1, D_MODEL), ln_b.reshape(1, D_MODEL),
      gm_ws, mmask, jnp.transpose(gm_bs), cos2, sin2, dmask, qd, kd, w_a, w_b, w_o,
      jq, jkt, jrv, jqd[:ntok], jstate)


def _mix_sample_front(h, nb, ntok, w_in, ln_g, ln_b, gm_ws, gm_bs, w_a):
    rows = MIX_ROWS
    nseq = rows // ntok
    nsub = rows // CHUNK
    nrow, nchunk = nb * ntok, nb * ntok // CHUNK
    rep = CHUNK // ntok
    dmask, _, kd, _, mmask = _retention_tables(ntok)
    cos2, sin2 = _rotary_tables(PAST_LEN + np.arange(ntok), rows)
    onehot = jnp.asarray(np.tile(np.eye(ntok, dtype=np.float32), (rep, 1)))
    wmix = jnp.einsum("ra,gab,cb->grc", onehot, gm_ws[:, :ntok, :ntok], onehot, precision=lax.Precision.HIGHEST)
    bst = jnp.einsum("ra,ga->rg", onehot, gm_bs[:, :ntok], precision=lax.Precision.HIGHEST)
    row_spec = lambda width: pl.BlockSpec((rows, width), lambda i: (i, 0))
    return pl.pallas_call(
        _mix_front_kernel,
        grid=(nb // nseq,),
        in_specs=([row_spec(D_MODEL)] + _mix_weight_specs()
                  + [_const_spec((rows, DK)), _const_spec((rows, DK)),
                     _const_spec((HEADS, CHUNK, CHUNK)), _const_spec((CHUNK, HEADS)),
                     _const_spec((D_MODEL, D_MODEL))]),
        out_specs=[pl.BlockSpec((1, nseq, ntok, D_MODEL), lambda i: (0, i, 0, 0)),
                   row_spec(D_MODEL), row_spec(D_MODEL),
                   pl.BlockSpec((nsub, D_MODEL, CHUNK), lambda i: (i, 0, 0)),
                   pl.BlockSpec((nsub, CHUNK, RET_V), lambda i: (i, 0, 0)),
                   row_spec(RET_V)],
        out_shape=[jax.ShapeDtypeStruct((1, nb, ntok, D_MODEL), F32),
                   jax.ShapeDtypeStruct((nrow, D_MODEL), F32),
                   jax.ShapeDtypeStruct((nrow, D_MODEL), F32),
                   jax.ShapeDtypeStruct((nchunk, D_MODEL, CHUNK), F32),
                   jax.ShapeDtypeStruct((nchunk, CHUNK, RET_V), BF16),
                   jax.ShapeDtypeStruct((nrow, RET_V), F32)],
        scratch_shapes=[pltpu.VMEM((rows, D_MODEL), BF16)],
        compiler_params=pltpu.CompilerParams(
            dimension_semantics=("arbitrary",), vmem_limit_bytes=VMEM_LIMIT_V7X),
        name="mix_sample_front",
    )(h, w_in, ln_g.reshape(1, D_MODEL), ln_b.reshape(1, D_MODEL),
      wmix, mmask, bst, cos2, sin2, dmask, kd, w_a)


def _mix_sample_back(x, mods, mod_row0, h, merged, ob, inter, w_in, w_b, w_o):
    nb, ntok, _ = x.shape
    rows = MIX_ROWS
    nseq = rows // ntok
    xspec = pl.BlockSpec((nseq, ntok, D_MODEL), lambda i: (i, 0, 0))
    row_spec = lambda width: pl.BlockSpec((rows, width), lambda i: (i, 0))
    col_spec = lambda lo, hi: pl.BlockSpec((D_MODEL, hi - lo), lambda i: (0, lo // (hi - lo)),
                                           pipeline_mode=pl.Buffered(1))
    assert OFF_RG % (OFF_GA - OFF_RG) == 0 and OFF_GB % (OFF_END - OFF_GB) == 0
    return pl.pallas_call(
        _mix_back_kernel,
        grid=(nb // nseq,),
        in_specs=[xspec, _mod_specs(nseq, mod_row0, 3)[2], row_spec(D_MODEL), row_spec(D_MODEL),
                  row_spec(RET_V), row_spec(RET_V),
                  col_spec(OFF_RG, OFF_GA), col_spec(OFF_GB, OFF_END),
                  _const_spec((RET_V, D_MODEL)), _const_spec((D_MODEL, D_MODEL))],
        out_specs=xspec,
        out_shape=jax.ShapeDtypeStruct(x.shape, F32),
        scratch_shapes=[pltpu.VMEM((rows, RET_V), F32)],
        compiler_params=pltpu.CompilerParams(
            dimension_semantics=("arbitrary",), vmem_limit_bytes=VMEM_LIMIT_V7X),
        name="mix_sample_back",
    )(x, mods, h, merged, ob, inter, w_in, w_in, w_b, w_o)


def kernel(x_prompt, x_sample, state_ret, c_prompt, c_sample, w_ada, b_ada, n1_g, w1_gate, w1_up, w1_down,
           nm_g, w_in, gm_ln_g, gm_ln_b, gm_ws, gm_bs, w_a, w_b, w_o, n2_g, w2_gate, w2_up, w2_down, final_g):
    assert w_ada.shape[0] == 1, "single-layer step"
    nbs = x_sample.shape[0]

    mods, w1 = _ada(jnp.concatenate([c_sample, c_prompt], axis=0), w_ada[0], b_ada[0],
                    cast=(w1_gate[0], w1_up[0], w1_down[0]))
    row_s, row_p = 0, nbs

    ((yp, hp), (ys, hs)), later = _ffn(
        [(x_prompt, row_p), (x_sample, row_s)], mods, 0, n1_g[0], *w1, nm_g[0], emit_next=True,
        cast=(w_in[0], w_a[0], w_b[0], w_o[0], w2_gate[0], w2_up[0], w2_down[0]))
    w_in_b, w_a_b, w_b_b, w_o_b = later[:4]
    w2 = later[4:]
    gmlp = (w_in_b, gm_ln_g[0], gm_ln_b[0], gm_ws[0], gm_bs[0], w_a_b)

    nbs, ntok, _ = x_sample.shape
    vs, merged_s, q_s, kt_s, rv_s, ob_s = _mix_sample_front(hs, nbs, ntok, *gmlp)
    yp, sp, inter_s, ss = _mix_prompt(yp, hp, mods, row_p, *gmlp, w_b_b, w_o_b,
                                      job=(q_s, kt_s, rv_s, state_ret.astype(F32), ntok))
    ys = _mix_sample_back(ys, mods, row_s, hs, merged_s, ob_s, inter_s, w_in_b, w_b_b, w_o_b)

    (yp, ys), _ = _ffn([(yp, row_p), (ys, row_s)], mods, 6, n2_g[0], *w2, final_g, emit_next=False)
    return (yp, ys, sp, ss, vs)
```

```python
import functools

import jax
import jax.numpy as jnp
import numpy as np
from jax import lax
from jax.experimental import pallas as pl
from jax.experimental.pallas import tpu as pltpu

F32 = jnp.float32
BF16 = jnp.bfloat16

D_MODEL = 1024
D_FF = 2816
N_MOD = 9
EPS = 1e-6
ROPE_BASE = 10000.0
PAST_LEN = 16384
CHUNK = 128
GROUPS = 8
GROUP_DIM = D_MODEL // GROUPS
HEADS = 8
DK = D_MODEL // HEADS
DV = 2 * DK
RET_V = HEADS * DV
OFF_U, OFF_V, OFF_Q, OFF_K, OFF_RV, OFF_RG, OFF_GA, OFF_GB, OFF_END = (
    0, 1024, 2048, 3072, 4096, 6144, 8192, 9216, 10240)

VMEM_LIMIT_V7X = 56 * 1024 * 1024
FFN_ROWS = 512
FFN_SPLIT = 2
MIX_ROWS = 256
ADA_STEPS = 8
BF16_SUBLANES = 16
F32_SUBLANES = 8
FFN1_MOD_CHUNK, MIX_GATE_CHUNK, FFN2_MOD_CHUNK = 0, 5, 6


def _dot(a, b):
    return jnp.dot(a, b, preferred_element_type=F32)


def _silu(x):
    return x * jax.nn.sigmoid(x)


def _gelu_tanh(x):
    c = np.float32(np.sqrt(2.0 / np.pi))
    return 0.5 * x * (1.0 + jnp.tanh(c * (x + 0.044715 * (x * x * x))))


def _rms_norm(x, gain):
    return x * lax.rsqrt(jnp.mean(x * x, axis=-1, keepdims=True) + EPS) * gain


def _const_spec(shape):
    nd = len(shape)
    return pl.BlockSpec(shape, lambda *_: (0,) * nd, pipeline_mode=pl.Buffered(1))


def _mod_spec(bb, row0, chunk):
    assert row0 % bb == 0
    return pl.BlockSpec((bb, 1, D_MODEL), lambda i, *_: (row0 // bb + i, 0, chunk))


def _cast_plan(weights, nsteps):
    in_specs, out_specs, out_shapes = [], [], []
    for w in weights:
        rows, cols = w.shape
        nblk = max(n for n in range(1, nsteps + 1) if rows % n == 0 and (rows // n) % BF16_SUBLANES == 0)
        spec = pl.BlockSpec((rows // nblk, cols), lambda step, nblk=nblk: (step * nblk // nsteps, 0))
        in_specs.append(spec)
        out_specs.append(spec)
        out_shapes.append(jax.ShapeDtypeStruct(w.shape, BF16))
    return in_specs, out_specs, out_shapes


def _cast_blocks(src_refs, dst_refs):
    for src, dst in zip(src_refs, dst_refs, strict=True):
        dst[...] = src[...].astype(BF16)


def _ada_kernel(c_ref, w_ref, b_ref, *refs):
    ncast = (len(refs) - 1) // 2
    o_ref = refs[ncast]
    s = _silu(c_ref[...]).astype(BF16)
    m = _dot(s, w_ref[...].astype(BF16)) + b_ref[...]
    for r in range(m.shape[0]):
        o_ref[r] = m[r:r + 1, :]
    _cast_blocks(refs[:ncast], refs[ncast + 1:])


def _ada(c_all, w_ada, b_ada, cast):
    rows = c_all.shape[0]
    n = w_ada.shape[1]
    grid = (ADA_STEPS,)
    bn = n // ADA_STEPS
    cast_in, cast_out, cast_shapes = _cast_plan(cast, ADA_STEPS)
    out = pl.pallas_call(
        _ada_kernel,
        grid=grid,
        in_specs=[pl.BlockSpec((rows, D_MODEL), lambda j: (0, 0)),
                  pl.BlockSpec((D_MODEL, bn), lambda j: (0, j)),
                  pl.BlockSpec((1, bn), lambda j: (0, j))] + cast_in,
        out_specs=[pl.BlockSpec((rows, 1, bn), lambda j: (0, 0, j))] + cast_out,
        out_shape=[jax.ShapeDtypeStruct((rows, 1, n), F32)] + cast_shapes,
        compiler_params=pltpu.CompilerParams(
            dimension_semantics=("arbitrary",), vmem_limit_bytes=VMEM_LIMIT_V7X),
        name="ada",
    )(c_all, w_ada, b_ada.reshape(1, n), *cast)
    return out[0], out[1:]


def _ffn_rows(x_ref, sh_ref, sc_ref, g_ref, nxt_refs, ng_ref, wg_ref, wu_ref, wd_ref, eg_ref, o_ref, h_ref):
    nb, nr, _ = x_ref.shape
    if nb > 1:
        groups = [(slice(s * nb // FFN_SPLIT, (s + 1) * nb // FFN_SPLIT), slice(None)) for s in range(FFN_SPLIT)]
    else:
        groups = [(slice(None), slice(s * nr // FFN_SPLIT, (s + 1) * nr // FFN_SPLIT)) for s in range(FFN_SPLIT)]
    xs = [x_ref[seqs, toks, :] for seqs, toks in groups]
    bb, r, _ = xs[0].shape
    hs = [(_rms_norm(x, ng_ref[...]) * (1.0 + sc_ref[seqs]) + sh_ref[seqs]).reshape(bb * r, D_MODEL).astype(BF16)
          for x, (seqs, _) in zip(xs, groups)]
    gates = [_dot(h, wg_ref[...]) for h in hs]
    ups = [_dot(h, wu_ref[...]) for h in hs]
    acts = [(_silu(gt) * up).astype(BF16) for gt, up in zip(gates, ups)]
    ys = [_dot(act, wd_ref[...]).reshape(bb, r, D_MODEL) for act in acts]
    for s, (x, y, (seqs, toks)) in enumerate(zip(xs, ys, groups)):
        out = x + (0.5 * g_ref[seqs]) * y
        if nxt_refs is None:
            o_ref[seqs, toks, :] = _rms_norm(out, eg_ref[...])
        else:
            shn_ref, scn_ref = nxt_refs
            o_ref[seqs, toks, :] = out
            hn = _rms_norm(out, eg_ref[...]) * (1.0 + scn_ref[seqs]) + shn_ref[seqs]
            h_ref[s * bb * r:(s + 1) * bb * r, :] = hn.reshape(bb * r, D_MODEL).astype(BF16)


N_FFN_WEIGHT_IN = 5


def _ffn_kernel(*refs, emit_next, batch_steps):
    nbatch = len(batch_steps)
    per_in = 6 if emit_next else 4
    per_out = 2 if emit_next else 1
    n_in = nbatch * per_in + N_FFN_WEIGHT_IN
    ncast = (len(refs) - n_in - nbatch * per_out) // 2
    weights = refs[nbatch * per_in:n_in]
    outs = refs[n_in + ncast:]
    step = pl.program_id(0)
    first = 0
    for b, nsteps in enumerate(batch_steps):
        ins = refs[b * per_in:(b + 1) * per_in]
        bouts = outs[b * per_out:(b + 1) * per_out]

        @pl.when((step >= first) & (step < first + nsteps))
        def _(ins=ins, bouts=bouts):
            _ffn_rows(*ins[:4], ins[4:] if emit_next else None, *weights,
                      bouts[0], bouts[1] if emit_next else None)

        first += nsteps
    _cast_blocks(refs[n_in:n_in + ncast], outs[nbatch * per_out:])


def _ffn(batches, mods, mod_first, norm_g, wg, wu, wd, out_gain, *, emit_next, cast=()):
    operands, in_specs, out_specs, out_shapes, batch_steps = [], [], [], [], []
    first = 0
    for x, mod_row0 in batches:
        nb, r, _ = x.shape
        br = min(r, FFN_ROWS)
        bb = FFN_ROWS // br
        nj = r // br
        nsteps = (nb // bb) * nj
        assert mod_row0 % bb == 0
        local = lambda step, first=first, nsteps=nsteps: jnp.clip(step - first, 0, nsteps - 1)
        xspec = pl.BlockSpec((bb, br, D_MODEL), lambda step, local=local, nj=nj: (local(step) // nj, local(step) % nj, 0))
        nmod = 5 if emit_next else 3
        mspecs = [pl.BlockSpec((bb, 1, D_MODEL),
                               lambda step, local=local, nj=nj, row=mod_row0 // bb, k=k: (row + local(step) // nj, 0, k))
                  for k in range(mod_first, mod_first + nmod)]
        operands += [x] + [mods] * nmod
        in_specs += [xspec] + mspecs
        out_specs.append(xspec)
        out_shapes.append(jax.ShapeDtypeStruct(x.shape, F32))
        if emit_next:
            out_specs.append(pl.BlockSpec((bb * br, D_MODEL), lambda step, local=local: (local(step), 0)))
            out_shapes.append(jax.ShapeDtypeStruct((nb * r, D_MODEL), BF16))
        batch_steps.append(nsteps)
        first += nsteps
    cast_in, cast_out, cast_shapes = _cast_plan(cast, first)
    out = pl.pallas_call(
        functools.partial(_ffn_kernel, emit_next=emit_next, batch_steps=tuple(batch_steps)),
        grid=(first,),
        in_specs=in_specs + [
            _const_spec((1, D_MODEL)),
            _const_spec((D_MODEL, D_FF)), _const_spec((D_MODEL, D_FF)), _const_spec((D_FF, D_MODEL)),
            _const_spec((1, D_MODEL))] + cast_in,
        out_specs=out_specs + cast_out,
        out_shape=out_shapes + cast_shapes,
        compiler_params=pltpu.CompilerParams(
            dimension_semantics=("arbitrary",), vmem_limit_bytes=VMEM_LIMIT_V7X),
        name="ffn" if emit_next else "ffn_final",
    )(*operands, norm_g.reshape(1, D_MODEL), wg, wu, wd, out_gain.reshape(1, D_MODEL), *cast)
    nout = len(batches) * (2 if emit_next else 1)
    ys = [tuple(out[2 * b:2 * b + 2]) for b in range(len(batches))] if emit_next else list(out[:nout])
    return ys, out[nout:]


def _layer_norm(x, gain, bias):
    mu = jnp.mean(x, axis=-1, keepdims=True)
    var = jnp.mean(jnp.square(x - mu), axis=-1, keepdims=True)
    return (x - mu) * lax.rsqrt(var + EPS) * gain + bias


def _spatial_mix(u, vb, wmix_ref, mmask_ref, bst_ref, oa_scr):
    mmask = mmask_ref[...]
    for g in range(GROUPS):
        wg = (wmix_ref[g] * mmask).astype(BF16)
        cols = slice(g * GROUP_DIM, (g + 1) * GROUP_DIM)
        for c in range(u.shape[0] // CHUNK):
            rws = slice(c * CHUNK, (c + 1) * CHUNK)
            mix = _dot(wg, vb[rws, cols]) + bst_ref[:, g:g + 1]
            oa_scr[rws, cols] = (u[rws, cols] * mix).astype(BF16)


def _gmlp_branch(h, win_ref, lng_ref, lnb_ref, wmix_ref, mmask_ref, bst_ref, wa_ref, oa_scr):
    u = _gelu_tanh(_dot(h, win_ref[:, OFF_U:OFF_V]))
    v = _layer_norm(_gelu_tanh(_dot(h, win_ref[:, OFF_V:OFF_Q])), lng_ref[...], lnb_ref[...])
    _spatial_mix(u, v.astype(BF16), wmix_ref, mmask_ref, bst_ref, oa_scr)
    pa = _dot(oa_scr[...], wa_ref[...])
    ga = _dot(h, win_ref[:, OFF_GA:OFF_GB])
    return jax.nn.sigmoid(ga) * pa, v


def _rotary_heads(z, cos2, sin2):
    out = []
    for hd in range(HEADS):
        zh = z[:, hd * DK:(hd + 1) * DK]
        out.append(zh * cos2 + pltpu.roll(zh, DK // 2, 1) * sin2)
    return out


def _head_rms(o):
    return o * lax.rsqrt(jnp.mean(o * o, axis=-1, keepdims=True) + EPS)


def _state_step(q_ref, kt_ref, rv_ref, qd_ref, s0_ref, inter_ref, snew_ref, first_seq, s_decay):
    nstep = s0_ref.shape[1]
    ntok = qd_ref.shape[0]
    seq_per_chunk = CHUNK // ntok
    for t in range(nstep):
        tok = slice(t * ntok, (t + 1) * ntok)
        in_seq = lax.broadcasted_iota(jnp.int32, (DK, CHUNK), 1) // ntok == (first_seq + t) % seq_per_chunk
        for hd in range(HEADS):
            state = s0_ref[0, t, hd]
            qj = q_ref[tok, hd * DK:(hd + 1) * DK].astype(BF16)
            inter_ref[tok, hd * DV:(hd + 1) * DV] = _dot(qj, state.astype(BF16)) * qd_ref[:, hd:hd + 1]
            kt = jnp.where(in_seq, kt_ref[0, hd * DK:(hd + 1) * DK, :], 0.0).astype(BF16)
            upd = _dot(kt, rv_ref[0, :, hd * DV:(hd + 1) * DV])
            snew_ref[0, t, hd] = s_decay[hd] * state + upd


N_MIX_IN = 17
N_JOB_IN = 5


def _mix_prompt_kernel(*refs, s_decay, job_decay):
    (x_ref, h_ref, g_ref, win_ref, lng_ref, lnb_ref, wmix_ref, mmask_ref,
     bst_ref, cos_ref, sin_ref, dmask_ref, qd_ref, kd_ref, wa_ref, wb_ref, wo_ref) = refs[:N_MIX_IN]
    rest = refs[N_MIX_IN:]
    if job_decay is not None:
        jq_ref, jkt_ref, jrv_ref, jqd_ref, js0_ref = rest[:N_JOB_IN]
        o_ref, s_ref, jinter_ref, jsnew_ref, oa_scr, ob_scr = rest[N_JOB_IN:]
    else:
        o_ref, s_ref, oa_scr, ob_scr = rest

    @pl.when(pl.program_id(1) == 0)
    def _():
        s_ref[...] = jnp.zeros(s_ref.shape, F32)

    x = x_ref[0]
    tm = x.shape[0]
    h = h_ref[...]
    proj = lambda lo, hi: _dot(h, win_ref[:, lo:hi])
    cos2, sin2 = cos_ref[...], sin_ref[...]
    zv = proj(OFF_V, OFF_Q)
    zu = proj(OFF_U, OFF_V)
    gv = _gelu_tanh(zv)
    zq = proj(OFF_Q, OFF_K)
    v = _layer_norm(gv, lng_ref[...], lnb_ref[...])
    u = _gelu_tanh(zu)
    zk = proj(OFF_K, OFF_RV)
    qs = _rotary_heads(zq, cos2, sin2)
    rv = proj(OFF_RV, OFF_RG).astype(BF16)
    ks = _rotary_heads(zk, cos2, sin2)
    _spatial_mix(u, v.astype(BF16), wmix_ref, mmask_ref, bst_ref, oa_scr)
    kscale = np.float32(DK ** -0.5)
    gate_a = jax.nn.sigmoid(proj(OFF_GA, OFF_GB))
    heads = range(HEADS)
    qb = [qs[hd].astype(BF16) for hd in heads]
    kf = [ks[hd] * kscale for hd in heads]
    kb = [kf[hd].astype(BF16) for hd in heads]
    nt = (((1,), (1,)), ((), ()))
    for c in range(tm // CHUNK):
        rws = slice(c * CHUNK, (c + 1) * CHUNK)
        vc = [rv[rws, hd * DV:(hd + 1) * DV] for hd in heads]
        state = [s_ref[0, 0, hd] for hd in heads]
        scores = [lax.dot_general(qb[hd][rws], kb[hd][rws], nt, preferred_element_type=F32) for hd in heads]
        inter = [_dot(qb[hd][rws], state[hd].astype(BF16)) for hd in heads]
        kdt = [jnp.transpose(kf[hd][rws] * kd_ref[:, hd:hd + 1]).astype(BF16) for hd in heads]
        upd = [_dot(kdt[hd], vc[hd]) for hd in heads]
        intra = [_dot((scores[hd] * dmask_ref[hd]).astype(BF16), vc[hd]) for hd in heads]
        for hd in heads:
            s_ref[0, 0, hd] = s_decay[hd] * state[hd] + upd[hd]
            ob_scr[rws, hd * DV:(hd + 1) * DV] = _head_rms(intra[hd] + inter[hd] * qd_ref[:, hd:hd + 1])
        if c == 0:
            gate_r = _silu(proj(OFF_RG, OFF_GA))
    gate_b = jax.nn.sigmoid(proj(OFF_GB, OFF_END))
    pa = _dot(oa_scr[...], wa_ref[...])
    merged = gate_a * pa + gate_b * _dot((ob_scr[...] * gate_r).astype(BF16), wb_ref[...])
    o_ref[0] = x + g_ref[0] * _dot(merged.astype(BF16), wo_ref[...])
    if job_decay is not None:
        step = pl.program_id(0) * pl.num_programs(1) + pl.program_id(1)
        _state_step(jq_ref, jkt_ref, jrv_ref, jqd_ref, js0_ref, jinter_ref, jsnew_ref,
                    step * js0_ref.shape[1], job_decay)


def _mix_front_kernel(h_ref, win_ref, lng_ref, lnb_ref, wmix_ref, mmask_ref, bst_ref,
                      cos_ref, sin_ref, dmask_ref, kd_ref, wa_ref,
                      vrow_ref, mg_ref, q_ref, kt_ref, rv_ref, ob_ref, oa_scr):
    _, nseq, ntok, _ = vrow_ref.shape
    rows = nseq * ntok
    h = h_ref[...]
    merged, v = _gmlp_branch(h, win_ref, lng_ref, lnb_ref, wmix_ref, mmask_ref, bst_ref, wa_ref, oa_scr)
    mg_ref[...] = merged
    vrow_ref[0] = v.reshape(nseq, ntok, D_MODEL)
    cos2, sin2 = cos_ref[...], sin_ref[...]
    qs = _rotary_heads(_dot(h, win_ref[:, OFF_Q:OFF_K]), cos2, sin2)
    ks = _rotary_heads(_dot(h, win_ref[:, OFF_K:OFF_RV]), cos2, sin2)
    rv = _dot(h, win_ref[:, OFF_RV:OFF_RG]).astype(BF16)
    kscale = np.float32(DK ** -0.5)
    heads = range(HEADS)
    kf = [ks[hd] * kscale for hd in heads]
    nt = (((1,), (1,)), ((), ()))
    for c in range(rows // CHUNK):
        rws = slice(c * CHUNK, (c + 1) * CHUNK)
        rv_ref[c] = rv[rws]
        scores = [lax.dot_general(qs[hd][rws].astype(BF16), kf[hd][rws].astype(BF16), nt,
                                  preferred_element_type=F32) for hd in heads]
        for hd in heads:
            kt_ref[c, hd * DK:(hd + 1) * DK, :] = jnp.transpose(kf[hd][rws] * kd_ref[:, hd:hd + 1])
        for hd in heads:
            ob_ref[rws, hd * DV:(hd + 1) * DV] = _dot(
                (scores[hd] * dmask_ref[hd]).astype(BF16), rv[rws, hd * DV:(hd + 1) * DV])
    for hd in heads:
        q_ref[:, hd * DK:(hd + 1) * DK] = qs[hd]


def _mix_back_kernel(x_ref, g_ref, h_ref, mg_ref, ob_ref, inter_ref, wrg_ref, wgb_ref, wb_ref, wo_ref,
                     o_ref, on_scr):
    nseq, ntok, _ = x_ref.shape
    h = h_ref[...]
    for hd in range(HEADS):
        cols = slice(hd * DV, (hd + 1) * DV)
        on_scr[:, cols] = _head_rms(ob_ref[:, cols] + inter_ref[:, cols])
    ob = (on_scr[...] * _silu(_dot(h, wrg_ref[...]))).astype(BF16)
    merged = mg_ref[...] + jax.nn.sigmoid(_dot(h, wgb_ref[...])) * _dot(ob, wb_ref[...])
    out = _dot(merged.astype(BF16), wo_ref[...])
    o_ref[...] = x_ref[...] + g_ref[...] * out.reshape(nseq, ntok, D_MODEL)


def _retention_tables(seq_len):
    lc = min(seq_len, CHUNK)
    log_gamma = np.log1p(-np.power(2.0, -5.0 - np.arange(HEADS)))
    idx = np.arange(lc, dtype=np.float64)
    diff = idx[:, None] - idx[None, :]
    decay = np.where(diff >= 0, np.exp(np.maximum(diff, 0.0)[None] * log_gamma[:, None, None]), 0.0)
    q_decay = np.exp((idx + 1.0)[:, None] * log_gamma[None, :])
    k_decay = np.exp((lc - 1.0 - idx)[:, None] * log_gamma[None, :])
    s_decay = tuple(float(v) for v in np.exp(lc * log_gamma).astype(np.float32))
    rep = CHUNK // lc
    blockdiag = np.kron(np.eye(rep), np.ones((lc, lc)))
    dmask = np.tile(decay, (1, rep, rep)) * blockdiag[None]
    mmask = np.tile(np.tril(np.ones((lc, lc))), (rep, rep)) * blockdiag
    f32 = lambda a: jnp.asarray(a.astype(np.float32))
    return f32(dmask), f32(np.tile(q_decay, (rep, 1))), f32(np.tile(k_decay, (rep, 1))), s_decay, f32(mmask)


def _rotary_tables(pos, rows):
    half = DK // 2
    inv = ROPE_BASE ** (-np.arange(half, dtype=np.float64) / half)
    ang = np.asarray(pos, np.float64)[:, None] * inv[None, :]
    cos, sin = np.cos(ang), np.sin(ang)
    rep = rows // len(pos)
    cos2 = np.tile(np.concatenate([cos, cos], axis=-1), (rep, 1))
    sin2 = np.tile(np.concatenate([-sin, sin], axis=-1), (rep, 1))
    return jnp.asarray(cos2.astype(np.float32)), jnp.asarray(sin2.astype(np.float32))


def _mix_weight_specs():
    return [_const_spec((D_MODEL, OFF_END)),
            _const_spec((1, D_MODEL)), _const_spec((1, D_MODEL)),
            _const_spec((GROUPS, CHUNK, CHUNK)), _const_spec((CHUNK, CHUNK)),
            _const_spec((CHUNK, GROUPS))]


def _mix_out_weight_specs():
    return [_const_spec((HEADS, CHUNK, CHUNK)),
            _const_spec((CHUNK, HEADS)), _const_spec((CHUNK, HEADS)),
            _const_spec((D_MODEL, D_MODEL)), _const_spec((RET_V, D_MODEL)), _const_spec((D_MODEL, D_MODEL))]


def _mix_prompt(x, h, mods, mod_row0, w_in, ln_g, ln_b, gm_ws, gm_bs, w_a, w_b, w_o, job):
    nb, seq, _ = x.shape
    tm = MIX_ROWS
    nj = seq // tm
    dmask, qd, kd, s_decay, mmask = _retention_tables(seq)
    cos2, sin2 = _rotary_tables(np.arange(seq), seq)
    xspec = pl.BlockSpec((1, tm, D_MODEL), lambda b, j: (b, j, 0))
    hspec = pl.BlockSpec((tm, D_MODEL), lambda b, j: (b * nj + j, 0))
    tspec = pl.BlockSpec((tm, DK), lambda b, j: (j, 0))
    state_shape = (1, nb, HEADS, DK, DV)

    jq, jkt, jrv, jstate, ntok = job
    nseq_total = jstate.shape[1]
    nstep = nseq_total // (nb * nj)
    seq_per_chunk = CHUNK // ntok
    assert nstep * nb * nj == nseq_total and seq_per_chunk % nstep == 0 and (nstep * ntok) % F32_SUBLANES == 0
    _, jqd, _, job_decay, _ = _retention_tables(ntok)
    step = lambda b, j: b * nj + j
    chunk = lambda b, j: step(b, j) * nstep // seq_per_chunk
    jsspec = pl.BlockSpec((1, nstep, HEADS, DK, DV), lambda b, j: (0, step(b, j), 0, 0, 0))
    jispec = pl.BlockSpec((nstep * ntok, RET_V), lambda b, j: (step(b, j), 0))
    job_in = [pl.BlockSpec((nstep * ntok, D_MODEL), lambda b, j: (step(b, j), 0)),
              pl.BlockSpec((1, D_MODEL, CHUNK), lambda b, j: (chunk(b, j), 0, 0)),
              pl.BlockSpec((1, CHUNK, RET_V), lambda b, j: (chunk(b, j), 0, 0)),
              _const_spec((ntok, HEADS)), jsspec]

    return pl.pallas_call(
        functools.partial(_mix_prompt_kernel, s_decay=s_decay, job_decay=job_decay),
        grid=(nb, nj),
        in_specs=([xspec, hspec, _mod_spec(1, mod_row0, MIX_GATE_CHUNK)] + _mix_weight_specs() + [tspec, tspec]
                  + _mix_out_weight_specs() + job_in),
        out_specs=[xspec, pl.BlockSpec((1, 1, HEADS, DK, DV), lambda b, j: (0, b, 0, 0, 0)), jispec, jsspec],
        out_shape=[jax.ShapeDtypeStruct(x.shape, F32), jax.ShapeDtypeStruct(state_shape, F32),
                   jax.ShapeDtypeStruct((nseq_total * ntok, RET_V), F32), jax.ShapeDtypeStruct(jstate.shape, F32)],
        scratch_shapes=[pltpu.VMEM((tm, D_MODEL), BF16), pltpu.VMEM((tm, RET_V), F32)],
        compiler_params=pltpu.CompilerParams(
            dimension_semantics=("arbitrary", "arbitrary"), vmem_limit_bytes=VMEM_LIMIT_V7X),
        name="mix_prompt",
    )(x, h, mods, w_in, ln_g.reshape(1, D_MODEL), ln_b.reshape(1, D_MODEL),
      gm_ws, mmask, jnp.transpose(gm_bs), cos2, sin2, dmask, qd, kd, w_a, w_b, w_o,
      jq, jkt, jrv, jqd[:ntok], jstate)


def _mix_sample_front(h, nb, ntok, w_in, ln_g, ln_b, gm_ws, gm_bs, w_a):
    rows = MIX_ROWS
    nseq = rows // ntok
    nsub = rows // CHUNK
    nrow, nchunk = nb * ntok, nb * ntok // CHUNK
    rep = CHUNK // ntok
    dmask, _, kd, _, mmask = _retention_tables(ntok)
    cos2, sin2 = _rotary_tables(PAST_LEN + np.arange(ntok), rows)
    onehot = jnp.asarray(np.tile(np.eye(ntok, dtype=np.float32), (rep, 1)))
    wmix = jnp.einsum("ra,gab,cb->grc", onehot, gm_ws[:, :ntok, :ntok], onehot, precision=lax.Precision.HIGHEST)
    bst = jnp.einsum("ra,ga->rg", onehot, gm_bs[:, :ntok], precision=lax.Precision.HIGHEST)
    row_spec = lambda width: pl.BlockSpec((rows, width), lambda i: (i, 0))
    return pl.pallas_call(
        _mix_front_kernel,
        grid=(nb // nseq,),
        in_specs=([row_spec(D_MODEL)] + _mix_weight_specs()
                  + [_const_spec((rows, DK)), _const_spec((rows, DK)),
                     _const_spec((HEADS, CHUNK, CHUNK)), _const_spec((CHUNK, HEADS)),
                     _const_spec((D_MODEL, D_MODEL))]),
        out_specs=[pl.BlockSpec((1, nseq, ntok, D_MODEL), lambda i: (0, i, 0, 0)),
                   row_spec(D_MODEL), row_spec(D_MODEL),
                   pl.BlockSpec((nsub, D_MODEL, CHUNK), lambda i: (i, 0, 0)),
                   pl.BlockSpec((nsub, CHUNK, RET_V), lambda i: (i, 0, 0)),
                   row_spec(RET_V)],
        out_shape=[jax.ShapeDtypeStruct((1, nb, ntok, D_MODEL), F32),
                   jax.ShapeDtypeStruct((nrow, D_MODEL), F32),
                   jax.ShapeDtypeStruct((nrow, D_MODEL), F32),
                   jax.ShapeDtypeStruct((nchunk, D_MODEL, CHUNK), F32),
                   jax.ShapeDtypeStruct((nchunk, CHUNK, RET_V), BF16),
                   jax.ShapeDtypeStruct((nrow, RET_V), F32)],
        scratch_shapes=[pltpu.VMEM((rows, D_MODEL), BF16)],
        compiler_params=pltpu.CompilerParams(
            dimension_semantics=("arbitrary",), vmem_limit_bytes=VMEM_LIMIT_V7X),
        name="mix_sample_front",
    )(h, w_in, ln_g.reshape(1, D_MODEL), ln_b.reshape(1, D_MODEL),
      wmix, mmask, bst, cos2, sin2, dmask, kd, w_a)


def _mix_sample_back(x, mods, mod_row0, h, merged, ob, inter, w_in, w_b, w_o):
    nb, ntok, _ = x.shape
    rows = MIX_ROWS
    nseq = rows // ntok
    xspec = pl.BlockSpec((nseq, ntok, D_MODEL), lambda i: (i, 0, 0))
    row_spec = lambda width: pl.BlockSpec((rows, width), lambda i: (i, 0))
    col_spec = lambda lo, hi: pl.BlockSpec((D_MODEL, hi - lo), lambda i: (0, lo // (hi - lo)),
                                           pipeline_mode=pl.Buffered(1))
    assert OFF_RG % (OFF_GA - OFF_RG) == 0 and OFF_GB % (OFF_END - OFF_GB) == 0
    return pl.pallas_call(
        _mix_back_kernel,
        grid=(nb // nseq,),
        in_specs=[xspec, _mod_spec(nseq, mod_row0, MIX_GATE_CHUNK), row_spec(D_MODEL), row_spec(D_MODEL),
                  row_spec(RET_V), row_spec(RET_V),
                  col_spec(OFF_RG, OFF_GA), col_spec(OFF_GB, OFF_END),
                  _const_spec((RET_V, D_MODEL)), _const_spec((D_MODEL, D_MODEL))],
        out_specs=xspec,
        out_shape=jax.ShapeDtypeStruct(x.shape, F32),
        scratch_shapes=[pltpu.VMEM((rows, RET_V), F32)],
        compiler_params=pltpu.CompilerParams(
            dimension_semantics=("arbitrary",), vmem_limit_bytes=VMEM_LIMIT_V7X),
        name="mix_sample_back",
    )(x, mods, h, merged, ob, inter, w_in, w_in, w_b, w_o)


def kernel(x_prompt, x_sample, state_ret, c_prompt, c_sample, w_ada, b_ada, n1_g, w1_gate, w1_up, w1_down,
           nm_g, w_in, gm_ln_g, gm_ln_b, gm_ws, gm_bs, w_a, w_b, w_o, n2_g, w2_gate, w2_up, w2_down, final_g):
    assert w_ada.shape[0] == 1, "single-layer step"
    nbs = x_sample.shape[0]

    mods, w1 = _ada(jnp.concatenate([c_sample, c_prompt], axis=0), w_ada[0], b_ada[0],
                    cast=(w1_gate[0], w1_up[0], w1_down[0]))
    row_s, row_p = 0, nbs

    ((yp, hp), (ys, hs)), later = _ffn(
        [(x_prompt, row_p), (x_sample, row_s)], mods, FFN1_MOD_CHUNK, n1_g[0], *w1, nm_g[0], emit_next=True,
        cast=(w_in[0], w_a[0], w_b[0], w_o[0], w2_gate[0], w2_up[0], w2_down[0]))
    w_in_b, w_a_b, w_b_b, w_o_b = later[:4]
    w2 = later[4:]
    gmlp = (w_in_b, gm_ln_g[0], gm_ln_b[0], gm_ws[0], gm_bs[0], w_a_b)

    nbs, ntok, _ = x_sample.shape
    vs, merged_s, q_s, kt_s, rv_s, ob_s = _mix_sample_front(hs, nbs, ntok, *gmlp)
    yp, sp, inter_s, ss = _mix_prompt(yp, hp, mods, row_p, *gmlp, w_b_b, w_o_b,
                                      job=(q_s, kt_s, rv_s, state_ret.astype(F32), ntok))
    ys = _mix_sample_back(ys, mods, row_s, hs, merged_s, ob_s, inter_s, w_in_b, w_b_b, w_o_b)

    (yp, ys), _ = _ffn([(yp, row_p), (ys, row_s)], mods, FFN2_MOD_CHUNK, n2_g[0], *w2, final_g, emit_next=False)
    return (yp, ys, sp, ss, vs)
```

```python
import functools

import jax
import jax.numpy as jnp
import numpy as np
from jax import lax
from jax.experimental import pallas as pl
from jax.experimental.pallas import tpu as pltpu

F32 = jnp.float32
BF16 = jnp.bfloat16

D_MODEL = 1024
D_FF = 2816
N_MOD = 9
EPS = 1e-6
ROPE_BASE = 10000.0
PAST_LEN = 16384
CHUNK = 128
GROUPS = 8
GROUP_DIM = D_MODEL // GROUPS
HEADS = 8
DK = D_MODEL // HEADS
DV = 2 * DK
RET_V = HEADS * DV
OFF_U, OFF_V, OFF_Q, OFF_K, OFF_RV, OFF_RG, OFF_GA, OFF_GB, OFF_END = (
    0, 1024, 2048, 3072, 4096, 6144, 8192, 9216, 10240)

VMEM_LIMIT_V7X = 56 * 1024 * 1024
FFN_ROWS = 512
FFN_SPLIT = 2
MIX_ROWS = 256
ADA_STEPS = 8
BF16_SUBLANES = 16
F32_SUBLANES = 8
FFN1_MOD_CHUNK, MIX_GATE_CHUNK, FFN2_MOD_CHUNK = 0, 5, 6


def _dot(a, b):
    return jnp.dot(a, b, preferred_element_type=F32)


def _silu(x):
    return x * jax.nn.sigmoid(x)


def _gelu_tanh(x):
    c = np.float32(np.sqrt(2.0 / np.pi))
    return 0.5 * x * (1.0 + jnp.tanh(c * (x + 0.044715 * (x * x * x))))


def _rms_norm(x, gain):
    return x * lax.rsqrt(jnp.mean(x * x, axis=-1, keepdims=True) + EPS) * gain


def _const_spec(shape):
    nd = len(shape)
    return pl.BlockSpec(shape, lambda *_: (0,) * nd, pipeline_mode=pl.Buffered(1))


def _mod_spec(bb, row0, chunk):
    assert row0 % bb == 0
    return pl.BlockSpec((bb, 1, D_MODEL), lambda i, *_: (row0 // bb + i, 0, chunk))


def _cast_plan(weights, nsteps):
    in_specs, out_specs, out_shapes = [], [], []
    for w in weights:
        rows, cols = w.shape
        nblk = max(n for n in range(1, nsteps + 1) if rows % n == 0 and (rows // n) % BF16_SUBLANES == 0)
        spec = pl.BlockSpec((rows // nblk, cols), lambda step, nblk=nblk: (step * nblk // nsteps, 0))
        in_specs.append(spec)
        out_specs.append(spec)
        out_shapes.append(jax.ShapeDtypeStruct(w.shape, BF16))
    return in_specs, out_specs, out_shapes


def _cast_blocks(src_refs, dst_refs):
    for src, dst in zip(src_refs, dst_refs, strict=True):
        dst[...] = src[...].astype(BF16)


def _ada_kernel(c_ref, w_ref, b_ref, *refs):
    ncast = (len(refs) - 1) // 2
    o_ref = refs[ncast]
    s = _silu(c_ref[...]).astype(BF16)
    m = _dot(s, w_ref[...].astype(BF16)) + b_ref[...]
    for r in range(m.shape[0]):
        o_ref[r] = m[r:r + 1, :]
    _cast_blocks(refs[:ncast], refs[ncast + 1:])


def _ada(c_all, w_ada, b_ada, cast):
    rows = c_all.shape[0]
    n = w_ada.shape[1]
    grid = (ADA_STEPS,)
    bn = n // ADA_STEPS
    cast_in, cast_out, cast_shapes = _cast_plan(cast, ADA_STEPS)
    out = pl.pallas_call(
        _ada_kernel,
        grid=grid,
        in_specs=[pl.BlockSpec((rows, D_MODEL), lambda j: (0, 0)),
                  pl.BlockSpec((D_MODEL, bn), lambda j: (0, j)),
                  pl.BlockSpec((1, bn), lambda j: (0, j))] + cast_in,
        out_specs=[pl.BlockSpec((rows, 1, bn), lambda j: (0, 0, j))] + cast_out,
        out_shape=[jax.ShapeDtypeStruct((rows, 1, n), F32)] + cast_shapes,
        compiler_params=pltpu.CompilerParams(
            dimension_semantics=("arbitrary",), vmem_limit_bytes=VMEM_LIMIT_V7X),
        name="ada",
    )(c_all, w_ada, b_ada.reshape(1, n), *cast)
    return out[0], out[1:]


def _ffn_rows(x_ref, sh_ref, sc_ref, g_ref, nxt_refs, ng_ref, wg_ref, wu_ref, wd_ref, eg_ref, o_ref, h_ref):
    nb, nr, _ = x_ref.shape
    if nb > 1:
        groups = [(slice(s * nb // FFN_SPLIT, (s + 1) * nb // FFN_SPLIT), slice(None)) for s in range(FFN_SPLIT)]
    else:
        groups = [(slice(None), slice(s * nr // FFN_SPLIT, (s + 1) * nr // FFN_SPLIT)) for s in range(FFN_SPLIT)]
    xs = [x_ref[seqs, toks, :] for seqs, toks in groups]
    bb, r, _ = xs[0].shape
    hs = [(_rms_norm(x, ng_ref[...]) * (1.0 + sc_ref[seqs]) + sh_ref[seqs]).reshape(bb * r, D_MODEL).astype(BF16)
          for x, (seqs, _) in zip(xs, groups)]
    gates = [_dot(h, wg_ref[...]) for h in hs]
    ups = [_dot(h, wu_ref[...]) for h in hs]
    acts = [(_silu(gt) * up).astype(BF16) for gt, up in zip(gates, ups)]
    ys = [_dot(act, wd_ref[...]).reshape(bb, r, D_MODEL) for act in acts]
    for s, (x, y, (seqs, toks)) in enumerate(zip(xs, ys, groups)):
        out = x + (0.5 * g_ref[seqs]) * y
        if nxt_refs is None:
            o_ref[seqs, toks, :] = _rms_norm(out, eg_ref[...])
        else:
            shn_ref, scn_ref = nxt_refs
            o_ref[seqs, toks, :] = out
            hn = _rms_norm(out, eg_ref[...]) * (1.0 + scn_ref[seqs]) + shn_ref[seqs]
            h_ref[s * bb * r:(s + 1) * bb * r, :] = hn.reshape(bb * r, D_MODEL).astype(BF16)


N_FFN_WEIGHT_IN = 5


def _ffn_kernel(*refs, emit_next, batch_steps):
    nbatch = len(batch_steps)
    per_in = 6 if emit_next else 4
    per_out = 2 if emit_next else 1
    n_in = nbatch * per_in + N_FFN_WEIGHT_IN
    ncast = (len(refs) - n_in - nbatch * per_out) // 2
    weights = refs[nbatch * per_in:n_in]
    outs = refs[n_in + ncast:]
    step = pl.program_id(0)
    first = 0
    for b, nsteps in enumerate(batch_steps):
        ins = refs[b * per_in:(b + 1) * per_in]
        bouts = outs[b * per_out:(b + 1) * per_out]

        @pl.when((step >= first) & (step < first + nsteps))
        def _(ins=ins, bouts=bouts):
            _ffn_rows(*ins[:4], ins[4:] if emit_next else None, *weights,
                      bouts[0], bouts[1] if emit_next else None)

        first += nsteps
    _cast_blocks(refs[n_in:n_in + ncast], outs[nbatch * per_out:])


def _ffn(batches, mods, mod_first, norm_g, wg, wu, wd, out_gain, *, emit_next, cast=()):
    operands, in_specs, out_specs, out_shapes, batch_steps = [], [], [], [], []
    first = 0
    for x, mod_row0 in batches:
        nb, r, _ = x.shape
        br = min(r, FFN_ROWS)
        bb = FFN_ROWS // br
        nj = r // br
        nsteps = (nb // bb) * nj
        assert mod_row0 % bb == 0
        local = lambda step, first=first, nsteps=nsteps: jnp.clip(step - first, 0, nsteps - 1)
        xspec = pl.BlockSpec((bb, br, D_MODEL), lambda step, local=local, nj=nj: (local(step) // nj, local(step) % nj, 0))
        nmod = 5 if emit_next else 3
        mspecs = [pl.BlockSpec((bb, 1, D_MODEL),
                               lambda step, local=local, nj=nj, row=mod_row0 // bb, k=k: (row + local(step) // nj, 0, k))
                  for k in range(mod_first, mod_first + nmod)]
        operands += [x] + [mods] * nmod
        in_specs += [xspec] + mspecs
        out_specs.append(xspec)
        out_shapes.append(jax.ShapeDtypeStruct(x.shape, F32))
        if emit_next:
            out_specs.append(pl.BlockSpec((bb * br, D_MODEL), lambda step, local=local: (local(step), 0)))
            out_shapes.append(jax.ShapeDtypeStruct((nb * r, D_MODEL), BF16))
        batch_steps.append(nsteps)
        first += nsteps
    cast_in, cast_out, cast_shapes = _cast_plan(cast, first)
    out = pl.pallas_call(
        functools.partial(_ffn_kernel, emit_next=emit_next, batch_steps=tuple(batch_steps)),
        grid=(first,),
        in_specs=in_specs + [
            _const_spec((1, D_MODEL)),
            _const_spec((D_MODEL, D_FF)), _const_spec((D_MODEL, D_FF)), _const_spec((D_FF, D_MODEL)),
            _const_spec((1, D_MODEL))] + cast_in,
        out_specs=out_specs + cast_out,
        out_shape=out_shapes + cast_shapes,
        compiler_params=pltpu.CompilerParams(
            dimension_semantics=("arbitrary",), vmem_limit_bytes=VMEM_LIMIT_V7X),
        name="ffn" if emit_next else "ffn_final",
    )(*operands, norm_g.reshape(1, D_MODEL), wg, wu, wd, out_gain.reshape(1, D_MODEL), *cast)
    nout = len(batches) * (2 if emit_next else 1)
    ys = [tuple(out[2 * b:2 * b + 2]) for b in range(len(batches))] if emit_next else list(out[:nout])
    return ys, out[nout:]


def _layer_norm(x, gain, bias):
    mu = jnp.mean(x, axis=-1, keepdims=True)
    var = jnp.mean(jnp.square(x - mu), axis=-1, keepdims=True)
    return (x - mu) * lax.rsqrt(var + EPS) * gain + bias


def _spatial_mix(u, vb, wmix_ref, mmask_ref, bst_ref, oa_scr):
    mmask = mmask_ref[...]
    for g in range(GROUPS):
        wg = (wmix_ref[g] * mmask).astype(BF16)
        cols = slice(g * GROUP_DIM, (g + 1) * GROUP_DIM)
        for c in range(u.shape[0] // CHUNK):
            rws = slice(c * CHUNK, (c + 1) * CHUNK)
            mix = _dot(wg, vb[rws, cols]) + bst_ref[:, g:g + 1]
            oa_scr[rws, cols] = (u[rws, cols] * mix).astype(BF16)


def _gmlp_branch(h, win_ref, lng_ref, lnb_ref, wmix_ref, mmask_ref, bst_ref, wa_ref, oa_scr):
    u = _gelu_tanh(_dot(h, win_ref[:, OFF_U:OFF_V]))
    v = _layer_norm(_gelu_tanh(_dot(h, win_ref[:, OFF_V:OFF_Q])), lng_ref[...], lnb_ref[...])
    _spatial_mix(u, v.astype(BF16), wmix_ref, mmask_ref, bst_ref, oa_scr)
    pa = _dot(oa_scr[...], wa_ref[...])
    ga = _dot(h, win_ref[:, OFF_GA:OFF_GB])
    return jax.nn.sigmoid(ga) * pa, v


def _rotary_heads(z, cos2, sin2):
    out = []
    for hd in range(HEADS):
        zh = z[:, hd * DK:(hd + 1) * DK]
        out.append(zh * cos2 + pltpu.roll(zh, DK // 2, 1) * sin2)
    return out


def _head_rms(o):
    return o * lax.rsqrt(jnp.mean(o * o, axis=-1, keepdims=True) + EPS)


def _state_step(q_ref, kt_ref, rv_ref, qd_ref, s0_ref, inter_ref, snew_ref, first_seq, s_decay):
    nstep = s0_ref.shape[1]
    ntok = qd_ref.shape[0]
    seq_per_chunk = CHUNK // ntok
    for t in range(nstep):
        tok = slice(t * ntok, (t + 1) * ntok)
        in_seq = lax.broadcasted_iota(jnp.int32, (DK, CHUNK), 1) // ntok == (first_seq + t) % seq_per_chunk
        for hd in range(HEADS):
            state = s0_ref[0, t, hd]
            qj = q_ref[tok, hd * DK:(hd + 1) * DK].astype(BF16)
            inter_ref[tok, hd * DV:(hd + 1) * DV] = _dot(qj, state.astype(BF16)) * qd_ref[:, hd:hd + 1]
            kt = jnp.where(in_seq, kt_ref[0, hd * DK:(hd + 1) * DK, :], 0.0).astype(BF16)
            upd = _dot(kt, rv_ref[0, :, hd * DV:(hd + 1) * DV])
            snew_ref[0, t, hd] = s_decay[hd] * state + upd


N_MIX_IN = 17
N_JOB_IN = 5


def _mix_prompt_kernel(*refs, s_decay, job_decay):
    (x_ref, h_ref, g_ref, win_ref, lng_ref, lnb_ref, wmix_ref, mmask_ref,
     bst_ref, cos_ref, sin_ref, dmask_ref, qd_ref, kd_ref, wa_ref, wb_ref, wo_ref) = refs[:N_MIX_IN]
    rest = refs[N_MIX_IN:]
    if job_decay is not None:
        jq_ref, jkt_ref, jrv_ref, jqd_ref, js0_ref = rest[:N_JOB_IN]
        o_ref, s_ref, jinter_ref, jsnew_ref, oa_scr, ob_scr = rest[N_JOB_IN:]
    else:
        o_ref, s_ref, oa_scr, ob_scr = rest

    @pl.when(pl.program_id(1) == 0)
    def _():
        s_ref[...] = jnp.zeros(s_ref.shape, F32)

    x = x_ref[0]
    tm = x.shape[0]
    h = h_ref[...]
    proj = lambda lo, hi: _dot(h, win_ref[:, lo:hi])
    cos2, sin2 = cos_ref[...], sin_ref[...]
    zv = proj(OFF_V, OFF_Q)
    zu = proj(OFF_U, OFF_V)
    gv = _gelu_tanh(zv)
    zq = proj(OFF_Q, OFF_K)
    v = _layer_norm(gv, lng_ref[...], lnb_ref[...])
    u = _gelu_tanh(zu)
    zk = proj(OFF_K, OFF_RV)
    qs = _rotary_heads(zq, cos2, sin2)
    rv = proj(OFF_RV, OFF_RG).astype(BF16)
    ks = _rotary_heads(zk, cos2, sin2)
    kscale = np.float32(DK ** -0.5)
    heads = range(HEADS)
    qb = [qs[hd].astype(BF16) for hd in heads]
    kf = [ks[hd] * kscale for hd in heads]
    kb = [kf[hd].astype(BF16) for hd in heads]
    nt = (((1,), (1,)), ((), ()))
    for c in range(tm // CHUNK):
        rws = slice(c * CHUNK, (c + 1) * CHUNK)
        vc = [rv[rws, hd * DV:(hd + 1) * DV] for hd in heads]
        state = [s_ref[0, 0, hd] for hd in heads]
        scores = [lax.dot_general(qb[hd][rws], kb[hd][rws], nt, preferred_element_type=F32) for hd in heads]
        inter = [_dot(qb[hd][rws], state[hd].astype(BF16)) for hd in heads]
        kdt = [jnp.transpose(kf[hd][rws] * kd_ref[:, hd:hd + 1]).astype(BF16) for hd in heads]
        upd = [_dot(kdt[hd], vc[hd]) for hd in heads]
        intra = [_dot((scores[hd] * dmask_ref[hd]).astype(BF16), vc[hd]) for hd in heads]
        for hd in heads:
            s_ref[0, 0, hd] = s_decay[hd] * state[hd] + upd[hd]
            ob_scr[rws, hd * DV:(hd + 1) * DV] = _head_rms(intra[hd] + inter[hd] * qd_ref[:, hd:hd + 1])
        if c == 0:
            gate_r = _silu(proj(OFF_RG, OFF_GA))
            gate_a = jax.nn.sigmoid(proj(OFF_GA, OFF_GB))
            _spatial_mix(u, v.astype(BF16), wmix_ref, mmask_ref, bst_ref, oa_scr)
    gate_b = jax.nn.sigmoid(proj(OFF_GB, OFF_END))
    pa = _dot(oa_scr[...], wa_ref[...])
    merged = gate_a * pa + gate_b * _dot((ob_scr[...] * gate_r).astype(BF16), wb_ref[...])
    o_ref[0] = x + g_ref[0] * _dot(merged.astype(BF16), wo_ref[...])
    if job_decay is not None:
        step = pl.program_id(0) * pl.num_programs(1) + pl.program_id(1)
        _state_step(jq_ref, jkt_ref, jrv_ref, jqd_ref, js0_ref, jinter_ref, jsnew_ref,
                    step * js0_ref.shape[1], job_decay)


def _mix_front_kernel(h_ref, win_ref, lng_ref, lnb_ref, wmix_ref, mmask_ref, bst_ref,
                      cos_ref, sin_ref, dmask_ref, kd_ref, wa_ref,
                      vrow_ref, mg_ref, q_ref, kt_ref, rv_ref, ob_ref, oa_scr):
    _, nseq, ntok, _ = vrow_ref.shape
    rows = nseq * ntok
    h = h_ref[...]
    merged, v = _gmlp_branch(h, win_ref, lng_ref, lnb_ref, wmix_ref, mmask_ref, bst_ref, wa_ref, oa_scr)
    mg_ref[...] = merged
    vrow_ref[0] = v.reshape(nseq, ntok, D_MODEL)
    cos2, sin2 = cos_ref[...], sin_ref[...]
    qs = _rotary_heads(_dot(h, win_ref[:, OFF_Q:OFF_K]), cos2, sin2)
    ks = _rotary_heads(_dot(h, win_ref[:, OFF_K:OFF_RV]), cos2, sin2)
    rv = _dot(h, win_ref[:, OFF_RV:OFF_RG]).astype(BF16)
    kscale = np.float32(DK ** -0.5)
    heads = range(HEADS)
    kf = [ks[hd] * kscale for hd in heads]
    nt = (((1,), (1,)), ((), ()))
    for c in range(rows // CHUNK):
        rws = slice(c * CHUNK, (c + 1) * CHUNK)
        rv_ref[c] = rv[rws]
        scores = [lax.dot_general(qs[hd][rws].astype(BF16), kf[hd][rws].astype(BF16), nt,
                                  preferred_element_type=F32) for hd in heads]
        for hd in heads:
            kt_ref[c, hd * DK:(hd + 1) * DK, :] = jnp.transpose(kf[hd][rws] * kd_ref[:, hd:hd + 1])
        for hd in heads:
            ob_ref[rws, hd * DV:(hd + 1) * DV] = _dot(
                (scores[hd] * dmask_ref[hd]).astype(BF16), rv[rws, hd * DV:(hd + 1) * DV])
    for hd in heads:
        q_ref[:, hd * DK:(hd + 1) * DK] = qs[hd]


def _mix_back_kernel(x_ref, g_ref, h_ref, mg_ref, ob_ref, inter_ref, wrg_ref, wgb_ref, wb_ref, wo_ref,
                     o_ref, on_scr):
    nseq, ntok, _ = x_ref.shape
    h = h_ref[...]
    for hd in range(HEADS):
        cols = slice(hd * DV, (hd + 1) * DV)
        on_scr[:, cols] = _head_rms(ob_ref[:, cols] + inter_ref[:, cols])
    ob = (on_scr[...] * _silu(_dot(h, wrg_ref[...]))).astype(BF16)
    merged = mg_ref[...] + jax.nn.sigmoid(_dot(h, wgb_ref[...])) * _dot(ob, wb_ref[...])
    out = _dot(merged.astype(BF16), wo_ref[...])
    o_ref[...] = x_ref[...] + g_ref[...] * out.reshape(nseq, ntok, D_MODEL)


def _retention_tables(seq_len):
    lc = min(seq_len, CHUNK)
    log_gamma = np.log1p(-np.power(2.0, -5.0 - np.arange(HEADS)))
    idx = np.arange(lc, dtype=np.float64)
    diff = idx[:, None] - idx[None, :]
    decay = np.where(diff >= 0, np.exp(np.maximum(diff, 0.0)[None] * log_gamma[:, None, None]), 0.0)
    q_decay = np.exp((idx + 1.0)[:, None] * log_gamma[None, :])
    k_decay = np.exp((lc - 1.0 - idx)[:, None] * log_gamma[None, :])
    s_decay = tuple(float(v) for v in np.exp(lc * log_gamma).astype(np.float32))
    rep = CHUNK // lc
    blockdiag = np.kron(np.eye(rep), np.ones((lc, lc)))
    dmask = np.tile(decay, (1, rep, rep)) * blockdiag[None]
    mmask = np.tile(np.tril(np.ones((lc, lc))), (rep, rep)) * blockdiag
    f32 = lambda a: jnp.asarray(a.astype(np.float32))
    return f32(dmask), f32(np.tile(q_decay, (rep, 1))), f32(np.tile(k_decay, (rep, 1))), s_decay, f32(mmask)


def _rotary_tables(pos, rows):
    half = DK // 2
    inv = ROPE_BASE ** (-np.arange(half, dtype=np.float64) / half)
    ang = np.asarray(pos, np.float64)[:, None] * inv[None, :]
    cos, sin = np.cos(ang), np.sin(ang)
    rep = rows // len(pos)
    cos2 = np.tile(np.concatenate([cos, cos], axis=-1), (rep, 1))
    sin2 = np.tile(np.concatenate([-sin, sin], axis=-1), (rep, 1))
    return jnp.asarray(cos2.astype(np.float32)), jnp.asarray(sin2.astype(np.float32))


def _mix_weight_specs():
    return [_const_spec((D_MODEL, OFF_END)),
            _const_spec((1, D_MODEL)), _const_spec((1, D_MODEL)),
            _const_spec((GROUPS, CHUNK, CHUNK)), _const_spec((CHUNK, CHUNK)),
            _const_spec((CHUNK, GROUPS))]


def _mix_out_weight_specs():
    return [_const_spec((HEADS, CHUNK, CHUNK)),
            _const_spec((CHUNK, HEADS)), _const_spec((CHUNK, HEADS)),
            _const_spec((D_MODEL, D_MODEL)), _const_spec((RET_V, D_MODEL)), _const_spec((D_MODEL, D_MODEL))]


def _mix_prompt(x, h, mods, mod_row0, w_in, ln_g, ln_b, gm_ws, gm_bs, w_a, w_b, w_o, job):
    nb, seq, _ = x.shape
    tm = MIX_ROWS
    nj = seq // tm
    dmask, qd, kd, s_decay, mmask = _retention_tables(seq)
    cos2, sin2 = _rotary_tables(np.arange(seq), seq)
    xspec = pl.BlockSpec((1, tm, D_MODEL), lambda b, j: (b, j, 0))
    hspec = pl.BlockSpec((tm, D_MODEL), lambda b, j: (b * nj + j, 0))
    tspec = pl.BlockSpec((tm, DK), lambda b, j: (j, 0))
    state_shape = (1, nb, HEADS, DK, DV)

    jq, jkt, jrv, jstate, ntok = job
    nseq_total = jstate.shape[1]
    nstep = nseq_total // (nb * nj)
    seq_per_chunk = CHUNK // ntok
    assert nstep * nb * nj == nseq_total and seq_per_chunk % nstep == 0 and (nstep * ntok) % F32_SUBLANES == 0
    _, jqd, _, job_decay, _ = _retention_tables(ntok)
    step = lambda b, j: b * nj + j
    chunk = lambda b, j: step(b, j) * nstep // seq_per_chunk
    jsspec = pl.BlockSpec((1, nstep, HEADS, DK, DV), lambda b, j: (0, step(b, j), 0, 0, 0))
    jispec = pl.BlockSpec((nstep * ntok, RET_V), lambda b, j: (step(b, j), 0))
    job_in = [pl.BlockSpec((nstep * ntok, D_MODEL), lambda b, j: (step(b, j), 0)),
              pl.BlockSpec((1, D_MODEL, CHUNK), lambda b, j: (chunk(b, j), 0, 0)),
              pl.BlockSpec((1, CHUNK, RET_V), lambda b, j: (chunk(b, j), 0, 0)),
              _const_spec((ntok, HEADS)), jsspec]

    return pl.pallas_call(
        functools.partial(_mix_prompt_kernel, s_decay=s_decay, job_decay=job_decay),
        grid=(nb, nj),
        in_specs=([xspec, hspec, _mod_spec(1, mod_row0, MIX_GATE_CHUNK)] + _mix_weight_specs() + [tspec, tspec]
                  + _mix_out_weight_specs() + job_in),
        out_specs=[xspec, pl.BlockSpec((1, 1, HEADS, DK, DV), lambda b, j: (0, b, 0, 0, 0)), jispec, jsspec],
        out_shape=[jax.ShapeDtypeStruct(x.shape, F32), jax.ShapeDtypeStruct(state_shape, F32),
                   jax.ShapeDtypeStruct((nseq_total * ntok, RET_V), F32), jax.ShapeDtypeStruct(jstate.shape, F32)],
        scratch_shapes=[pltpu.VMEM((tm, D_MODEL), BF16), pltpu.VMEM((tm, RET_V), F32)],
        compiler_params=pltpu.CompilerParams(
            dimension_semantics=("arbitrary", "arbitrary"), vmem_limit_bytes=VMEM_LIMIT_V7X),
        name="mix_prompt",
    )(x, h, mods, w_in, ln_g.reshape(1, D_MODEL), ln_b.reshape(1, D_MODEL),
      gm_ws, mmask, jnp.transpose(gm_bs), cos2, sin2, dmask, qd, kd, w_a, w_b, w_o,
      jq, jkt, jrv, jqd[:ntok], jstate)


def _mix_sample_front(h, nb, ntok, w_in, ln_g, ln_b, gm_ws, gm_bs, w_a):
    rows = MIX_ROWS
    nseq = rows // ntok
    nsub = rows // CHUNK
    nrow, nchunk = nb * ntok, nb * ntok // CHUNK
    rep = CHUNK // ntok
    dmask, _, kd, _, mmask = _retention_tables(ntok)
    cos2, sin2 = _rotary_tables(PAST_LEN + np.arange(ntok), rows)
    onehot = jnp.asarray(np.tile(np.eye(ntok, dtype=np.float32), (rep, 1)))
    wmix = jnp.einsum("ra,gab,cb->grc", onehot, gm_ws[:, :ntok, :ntok], onehot, precision=lax.Precision.HIGHEST)
    bst = jnp.einsum("ra,ga->rg", onehot, gm_bs[:, :ntok], precision=lax.Precision.HIGHEST)
    row_spec = lambda width: pl.BlockSpec((rows, width), lambda i: (i, 0))
    return pl.pallas_call(
        _mix_front_kernel,
        grid=(nb // nseq,),
        in_specs=([row_spec(D_MODEL)] + _mix_weight_specs()
                  + [_const_spec((rows, DK)), _const_spec((rows, DK)),
                     _const_spec((HEADS, CHUNK, CHUNK)), _const_spec((CHUNK, HEADS)),
                     _const_spec((D_MODEL, D_MODEL))]),
        out_specs=[pl.BlockSpec((1, nseq, ntok, D_MODEL), lambda i: (0, i, 0, 0)),
                   row_spec(D_MODEL), row_spec(D_MODEL),
                   pl.BlockSpec((nsub, D_MODEL, CHUNK), lambda i: (i, 0, 0)),
                   pl.BlockSpec((nsub, CHUNK, RET_V), lambda i: (i, 0, 0)),
                   row_spec(RET_V)],
        out_shape=[jax.ShapeDtypeStruct((1, nb, ntok, D_MODEL), F32),
                   jax.ShapeDtypeStruct((nrow, D_MODEL), F32),
                   jax.ShapeDtypeStruct((nrow, D_MODEL), F32),
                   jax.ShapeDtypeStruct((nchunk, D_MODEL, CHUNK), F32),
                   jax.ShapeDtypeStruct((nchunk, CHUNK, RET_V), BF16),
                   jax.ShapeDtypeStruct((nrow, RET_V), F32)],
        scratch_shapes=[pltpu.VMEM((rows, D_MODEL), BF16)],
        compiler_params=pltpu.CompilerParams(
            dimension_semantics=("arbitrary",), vmem_limit_bytes=VMEM_LIMIT_V7X),
        name="mix_sample_front",
    )(h, w_in, ln_g.reshape(1, D_MODEL), ln_b.reshape(1, D_MODEL),
      wmix, mmask, bst, cos2, sin2, dmask, kd, w_a)


def _mix_sample_back(x, mods, mod_row0, h, merged, ob, inter, w_in, w_b, w_o):
    nb, ntok, _ = x.shape
    rows = MIX_ROWS
    nseq = rows // ntok
    xspec = pl.BlockSpec((nseq, ntok, D_MODEL), lambda i: (i, 0, 0))
    row_spec = lambda width: pl.BlockSpec((rows, width), lambda i: (i, 0))
    col_spec = lambda lo, hi: pl.BlockSpec((D_MODEL, hi - lo), lambda i: (0, lo // (hi - lo)),
                                           pipeline_mode=pl.Buffered(1))
    assert OFF_RG % (OFF_GA - OFF_RG) == 0 and OFF_GB % (OFF_END - OFF_GB) == 0
    return pl.pallas_call(
        _mix_back_kernel,
        grid=(nb // nseq,),
        in_specs=[xspec, _mod_spec(nseq, mod_row0, MIX_GATE_CHUNK), row_spec(D_MODEL), row_spec(D_MODEL),
                  row_spec(RET_V), row_spec(RET_V),
                  col_spec(OFF_RG, OFF_GA), col_spec(OFF_GB, OFF_END),
                  _const_spec((RET_V, D_MODEL)), _const_spec((D_MODEL, D_MODEL))],
        out_specs=xspec,
        out_shape=jax.ShapeDtypeStruct(x.shape, F32),
        scratch_shapes=[pltpu.VMEM((rows, RET_V), F32)],
        compiler_params=pltpu.CompilerParams(
            dimension_semantics=("arbitrary",), vmem_limit_bytes=VMEM_LIMIT_V7X),
        name="mix_sample_back",
    )(x, mods, h, merged, ob, inter, w_in, w_in, w_b, w_o)


def kernel(x_prompt, x_sample, state_ret, c_prompt, c_sample, w_ada, b_ada, n1_g, w1_gate, w1_up, w1_down,
           nm_g, w_in, gm_ln_g, gm_ln_b, gm_ws, gm_bs, w_a, w_b, w_o, n2_g, w2_gate, w2_up, w2_down, final_g):
    assert w_ada.shape[0] == 1, "single-layer step"
    nbs = x_sample.shape[0]

    mods, w1 = _ada(jnp.concatenate([c_sample, c_prompt], axis=0), w_ada[0], b_ada[0],
                    cast=(w1_gate[0], w1_up[0], w1_down[0]))
    row_s, row_p = 0, nbs

    ((yp, hp), (ys, hs)), later = _ffn(
        [(x_prompt, row_p), (x_sample, row_s)], mods, FFN1_MOD_CHUNK, n1_g[0], *w1, nm_g[0], emit_next=True,
        cast=(w_in[0], w_a[0], w_b[0], w_o[0], w2_gate[0], w2_up[0], w2_down[0]))
    w_in_b, w_a_b, w_b_b, w_o_b = later[:4]
    w2 = later[4:]
    gmlp = (w_in_b, gm_ln_g[0], gm_ln_b[0], gm_ws[0], gm_bs[0], w_a_b)

    nbs, ntok, _ = x_sample.shape
    vs, merged_s, q_s, kt_s, rv_s, ob_s = _mix_sample_front(hs, nbs, ntok, *gmlp)
    yp, sp, inter_s, ss = _mix_prompt(yp, hp, mods, row_p, *gmlp, w_b_b, w_o_b,
                                      job=(q_s, kt_s, rv_s, state_ret.astype(F32), ntok))
    ys = _mix_sample_back(ys, mods, row_s, hs, merged_s, ob_s, inter_s, w_in_b, w_b_b, w_o_b)

    (yp, ys), _ = _ffn([(yp, row_p), (ys, row_s)], mods, FFN2_MOD_CHUNK, n2_g[0], *w2, final_g, emit_next=False)
    return (yp, ys, sp, ss, vs)
```

```python
import functools

import jax
import jax.numpy as jnp
import numpy as np
from jax import lax
from jax.experimental import pallas as pl
from jax.experimental.pallas import tpu as pltpu

F32 = jnp.float32
BF16 = jnp.bfloat16

D_MODEL = 1024
D_FF = 2816
N_MOD = 9
EPS = 1e-6
ROPE_BASE = 10000.0
PAST_LEN = 16384
CHUNK = 128
GROUPS = 8
GROUP_DIM = D_MODEL // GROUPS
HEADS = 8
DK = D_MODEL // HEADS
DV = 2 * DK
RET_V = HEADS * DV
OFF_U, OFF_V, OFF_Q, OFF_K, OFF_RV, OFF_RG, OFF_GA, OFF_GB, OFF_END = (
    0, 1024, 2048, 3072, 4096, 6144, 8192, 9216, 10240)

VMEM_LIMIT_V7X = 56 * 1024 * 1024
FFN_ROWS = 512
FFN_SPLIT = 2
MIX_ROWS = 256
ADA_STEPS = 8
BF16_SUBLANES = 16
F32_SUBLANES = 8
FFN1_MOD_CHUNK, MIX_GATE_CHUNK, FFN2_MOD_CHUNK = 0, 5, 6


def _dot(a, b):
    return jnp.dot(a, b, preferred_element_type=F32)


def _silu(x):
    return x * jax.nn.sigmoid(x)


def _gelu_tanh(x):
    c = np.float32(np.sqrt(2.0 / np.pi))
    return 0.5 * x * (1.0 + jnp.tanh(c * (x + 0.044715 * (x * x * x))))


def _rms_norm(x, gain):
    return x * lax.rsqrt(jnp.mean(x * x, axis=-1, keepdims=True) + EPS) * gain


def _const_spec(shape):
    nd = len(shape)
    return pl.BlockSpec(shape, lambda *_: (0,) * nd, pipeline_mode=pl.Buffered(1))


def _mod_spec(bb, row0, chunk):
    assert row0 % bb == 0
    return pl.BlockSpec((bb, 1, D_MODEL), lambda i, *_: (row0 // bb + i, 0, chunk))


def _cast_plan(weights, nsteps):
    in_specs, out_specs, out_shapes = [], [], []
    for w in weights:
        rows, cols = w.shape
        nblk = max(n for n in range(1, nsteps + 1) if rows % n == 0 and (rows // n) % BF16_SUBLANES == 0)
        spec = pl.BlockSpec((rows // nblk, cols), lambda step, nblk=nblk: (step * nblk // nsteps, 0))
        in_specs.append(spec)
        out_specs.append(spec)
        out_shapes.append(jax.ShapeDtypeStruct(w.shape, BF16))
    return in_specs, out_specs, out_shapes


def _cast_blocks(src_refs, dst_refs):
    for src, dst in zip(src_refs, dst_refs, strict=True):
        dst[...] = src[...].astype(BF16)


def _ada_kernel(c_ref, w_ref, b_ref, *refs):
    ncast = (len(refs) - 1) // 2
    o_ref = refs[ncast]
    s = _silu(c_ref[...]).astype(BF16)
    m = _dot(s, w_ref[...].astype(BF16)) + b_ref[...]
    for r in range(m.shape[0]):
        o_ref[r] = m[r:r + 1, :]
    _cast_blocks(refs[:ncast], refs[ncast + 1:])


def _ada(c_all, w_ada, b_ada, cast):
    rows = c_all.shape[0]
    n = w_ada.shape[1]
    grid = (ADA_STEPS,)
    bn = n // ADA_STEPS
    cast_in, cast_out, cast_shapes = _cast_plan(cast, ADA_STEPS)
    out = pl.pallas_call(
        _ada_kernel,
        grid=grid,
        in_specs=[pl.BlockSpec((rows, D_MODEL), lambda j: (0, 0)),
                  pl.BlockSpec((D_MODEL, bn), lambda j: (0, j)),
                  pl.BlockSpec((1, bn), lambda j: (0, j))] + cast_in,
        out_specs=[pl.BlockSpec((rows, 1, bn), lambda j: (0, 0, j))] + cast_out,
        out_shape=[jax.ShapeDtypeStruct((rows, 1, n), F32)] + cast_shapes,
        compiler_params=pltpu.CompilerParams(
            dimension_semantics=("arbitrary",), vmem_limit_bytes=VMEM_LIMIT_V7X),
        name="ada",
    )(c_all, w_ada, b_ada.reshape(1, n), *cast)
    return out[0], out[1:]


def _ffn_rows(x_ref, sh_ref, sc_ref, g_ref, nxt_refs, ng_ref, wg_ref, wu_ref, wd_ref, eg_ref, o_ref, h_ref):
    nb, nr, _ = x_ref.shape
    if nb > 1:
        groups = [(slice(s * nb // FFN_SPLIT, (s + 1) * nb // FFN_SPLIT), slice(None)) for s in range(FFN_SPLIT)]
    else:
        groups = [(slice(None), slice(s * nr // FFN_SPLIT, (s + 1) * nr // FFN_SPLIT)) for s in range(FFN_SPLIT)]
    xs = [x_ref[seqs, toks, :] for seqs, toks in groups]
    bb, r, _ = xs[0].shape
    hs = [(_rms_norm(x, ng_ref[...]) * (1.0 + sc_ref[seqs]) + sh_ref[seqs]).reshape(bb * r, D_MODEL).astype(BF16)
          for x, (seqs, _) in zip(xs, groups)]
    gates = [_dot(h, wg_ref[...]) for h in hs]
    ups = [_dot(h, wu_ref[...]) for h in hs]
    acts = [(_silu(gt) * up).astype(BF16) for gt, up in zip(gates, ups)]
    ys = [_dot(act, wd_ref[...]).reshape(bb, r, D_MODEL) for act in acts]
    for s, (x, y, (seqs, toks)) in enumerate(zip(xs, ys, groups)):
        out = x + (0.5 * g_ref[seqs]) * y
        if nxt_refs is None:
            o_ref[seqs, toks, :] = _rms_norm(out, eg_ref[...])
        else:
            shn_ref, scn_ref = nxt_refs
            o_ref[seqs, toks, :] = out
            hn = _rms_norm(out, eg_ref[...]) * (1.0 + scn_ref[seqs]) + shn_ref[seqs]
            h_ref[s * bb * r:(s + 1) * bb * r, :] = hn.reshape(bb * r, D_MODEL).astype(BF16)


N_FFN_WEIGHT_IN = 5


def _ffn_kernel(*refs, emit_next, batch_steps):
    nbatch = len(batch_steps)
    per_in = 6 if emit_next else 4
    per_out = 2 if emit_next else 1
    n_in = nbatch * per_in + N_FFN_WEIGHT_IN
    ncast = (len(refs) - n_in - nbatch * per_out) // 2
    weights = refs[nbatch * per_in:n_in]
    outs = refs[n_in + ncast:]
    step = pl.program_id(0)
    first = 0
    for b, nsteps in enumerate(batch_steps):
        ins = refs[b * per_in:(b + 1) * per_in]
        bouts = outs[b * per_out:(b + 1) * per_out]

        @pl.when((step >= first) & (step < first + nsteps))
        def _(ins=ins, bouts=bouts):
            _ffn_rows(*ins[:4], ins[4:] if emit_next else None, *weights,
                      bouts[0], bouts[1] if emit_next else None)

        first += nsteps
    _cast_blocks(refs[n_in:n_in + ncast], outs[nbatch * per_out:])


def _ffn(batches, mods, mod_first, norm_g, wg, wu, wd, out_gain, *, emit_next, cast=()):
    operands, in_specs, out_specs, out_shapes, batch_steps = [], [], [], [], []
    first = 0
    for x, mod_row0 in batches:
        nb, r, _ = x.shape
        br = min(r, FFN_ROWS)
        bb = FFN_ROWS // br
        nj = r // br
        nsteps = (nb // bb) * nj
        assert mod_row0 % bb == 0
        local = lambda step, first=first, nsteps=nsteps: jnp.clip(step - first, 0, nsteps - 1)
        xspec = pl.BlockSpec((bb, br, D_MODEL), lambda step, local=local, nj=nj: (local(step) // nj, local(step) % nj, 0))
        nmod = 5 if emit_next else 3
        mspecs = [pl.BlockSpec((bb, 1, D_MODEL),
                               lambda step, local=local, nj=nj, row=mod_row0 // bb, k=k: (row + local(step) // nj, 0, k))
                  for k in range(mod_first, mod_first + nmod)]
        operands += [x] + [mods] * nmod
        in_specs += [xspec] + mspecs
        out_specs.append(xspec)
        out_shapes.append(jax.ShapeDtypeStruct(x.shape, F32))
        if emit_next:
            out_specs.append(pl.BlockSpec((bb * br, D_MODEL), lambda step, local=local: (local(step), 0)))
            out_shapes.append(jax.ShapeDtypeStruct((nb * r, D_MODEL), BF16))
        batch_steps.append(nsteps)
        first += nsteps
    cast_in, cast_out, cast_shapes = _cast_plan(cast, first)
    out = pl.pallas_call(
        functools.partial(_ffn_kernel, emit_next=emit_next, batch_steps=tuple(batch_steps)),
        grid=(first,),
        in_specs=in_specs + [
            _const_spec((1, D_MODEL)),
            _const_spec((D_MODEL, D_FF)), _const_spec((D_MODEL, D_FF)), _const_spec((D_FF, D_MODEL)),
            _const_spec((1, D_MODEL))] + cast_in,
        out_specs=out_specs + cast_out,
        out_shape=out_shapes + cast_shapes,
        compiler_params=pltpu.CompilerParams(
            dimension_semantics=("arbitrary",), vmem_limit_bytes=VMEM_LIMIT_V7X),
        name="ffn" if emit_next else "ffn_final",
    )(*operands, norm_g.reshape(1, D_MODEL), wg, wu, wd, out_gain.reshape(1, D_MODEL), *cast)
    nout = len(batches) * (2 if emit_next else 1)
    ys = [tuple(out[2 * b:2 * b + 2]) for b in range(len(batches))] if emit_next else list(out[:nout])
    return ys, out[nout:]


def _layer_norm(x, gain, bias):
    mu = jnp.mean(x, axis=-1, keepdims=True)
    var = jnp.mean(jnp.square(x - mu), axis=-1, keepdims=True)
    return (x - mu) * lax.rsqrt(var + EPS) * gain + bias


def _spatial_mix(u, vb, wmix_ref, mmask_ref, bst_ref, oa_scr):
    mmask = mmask_ref[...]
    for g in range(GROUPS):
        wg = (wmix_ref[g] * mmask).astype(BF16)
        cols = slice(g * GROUP_DIM, (g + 1) * GROUP_DIM)
        for c in range(u.shape[0] // CHUNK):
            rws = slice(c * CHUNK, (c + 1) * CHUNK)
            mix = _dot(wg, vb[rws, cols]) + bst_ref[:, g:g + 1]
            oa_scr[rws, cols] = (u[rws, cols] * mix).astype(BF16)


def _rotary_heads(z, cos2, sin2):
    out = []
    for hd in range(HEADS):
        zh = z[:, hd * DK:(hd + 1) * DK]
        out.append(zh * cos2 + pltpu.roll(zh, DK // 2, 1) * sin2)
    return out


def _head_rms(o):
    return o * lax.rsqrt(jnp.mean(o * o, axis=-1, keepdims=True) + EPS)


def _state_step(q_ref, kt_ref, rv_ref, qd_ref, s0_ref, inter_ref, snew_ref, first_seq, s_decay):
    nstep = s0_ref.shape[1]
    ntok = qd_ref.shape[0]
    seq_per_chunk = CHUNK // ntok
    for t in range(nstep):
        tok = slice(t * ntok, (t + 1) * ntok)
        in_seq = lax.broadcasted_iota(jnp.int32, (DK, CHUNK), 1) // ntok == (first_seq + t) % seq_per_chunk
        for hd in range(HEADS):
            state = s0_ref[0, t, hd]
            qj = q_ref[tok, hd * DK:(hd + 1) * DK].astype(BF16)
            inter_ref[tok, hd * DV:(hd + 1) * DV] = _dot(qj, state.astype(BF16)) * qd_ref[:, hd:hd + 1]
            kt = jnp.where(in_seq, kt_ref[0, hd * DK:(hd + 1) * DK, :], 0.0).astype(BF16)
            upd = _dot(kt, rv_ref[0, :, hd * DV:(hd + 1) * DV])
            snew_ref[0, t, hd] = s_decay[hd] * state + upd


N_MIX_IN = 17
N_JOB_IN = 5


def _mix_prompt_kernel(*refs, s_decay, job_decay):
    (x_ref, h_ref, g_ref, win_ref, lng_ref, lnb_ref, wmix_ref, mmask_ref,
     bst_ref, cos_ref, sin_ref, dmask_ref, qd_ref, kd_ref, wa_ref, wb_ref, wo_ref) = refs[:N_MIX_IN]
    rest = refs[N_MIX_IN:]
    if job_decay is not None:
        jq_ref, jkt_ref, jrv_ref, jqd_ref, js0_ref = rest[:N_JOB_IN]
        o_ref, s_ref, jinter_ref, jsnew_ref, oa_scr, ob_scr = rest[N_JOB_IN:]
    else:
        o_ref, s_ref, oa_scr, ob_scr = rest

    @pl.when(pl.program_id(1) == 0)
    def _():
        s_ref[...] = jnp.zeros(s_ref.shape, F32)

    x = x_ref[0]
    tm = x.shape[0]
    h = h_ref[...]
    proj = lambda lo, hi: _dot(h, win_ref[:, lo:hi])
    cos2, sin2 = cos_ref[...], sin_ref[...]
    zv = proj(OFF_V, OFF_Q)
    zu = proj(OFF_U, OFF_V)
    gv = _gelu_tanh(zv)
    zq = proj(OFF_Q, OFF_K)
    v = _layer_norm(gv, lng_ref[...], lnb_ref[...])
    u = _gelu_tanh(zu)
    zk = proj(OFF_K, OFF_RV)
    qs = _rotary_heads(zq, cos2, sin2)
    rv = proj(OFF_RV, OFF_RG).astype(BF16)
    ks = _rotary_heads(zk, cos2, sin2)
    kscale = np.float32(DK ** -0.5)
    heads = range(HEADS)
    qb = [qs[hd].astype(BF16) for hd in heads]
    kf = [ks[hd] * kscale for hd in heads]
    kb = [kf[hd].astype(BF16) for hd in heads]
    nt = (((1,), (1,)), ((), ()))
    for c in range(tm // CHUNK):
        rws = slice(c * CHUNK, (c + 1) * CHUNK)
        vc = [rv[rws, hd * DV:(hd + 1) * DV] for hd in heads]
        state = [s_ref[0, 0, hd] for hd in heads]
        scores = [lax.dot_general(qb[hd][rws], kb[hd][rws], nt, preferred_element_type=F32) for hd in heads]
        inter = [_dot(qb[hd][rws], state[hd].astype(BF16)) for hd in heads]
        kdt = [jnp.transpose(kf[hd][rws] * kd_ref[:, hd:hd + 1]).astype(BF16) for hd in heads]
        upd = [_dot(kdt[hd], vc[hd]) for hd in heads]
        intra = [_dot((scores[hd] * dmask_ref[hd]).astype(BF16), vc[hd]) for hd in heads]
        for hd in heads:
            s_ref[0, 0, hd] = s_decay[hd] * state[hd] + upd[hd]
            ob_scr[rws, hd * DV:(hd + 1) * DV] = _head_rms(intra[hd] + inter[hd] * qd_ref[:, hd:hd + 1])
        if c == 0:
            gate_r = _silu(proj(OFF_RG, OFF_GA))
            gate_a = jax.nn.sigmoid(proj(OFF_GA, OFF_GB))
            _spatial_mix(u, v.astype(BF16), wmix_ref, mmask_ref, bst_ref, oa_scr)
    gate_b = jax.nn.sigmoid(proj(OFF_GB, OFF_END))
    pa = _dot(oa_scr[...], wa_ref[...])
    merged = gate_a * pa + gate_b * _dot((ob_scr[...] * gate_r).astype(BF16), wb_ref[...])
    o_ref[0] = x + g_ref[0] * _dot(merged.astype(BF16), wo_ref[...])
    if job_decay is not None:
        step = pl.program_id(0) * pl.num_programs(1) + pl.program_id(1)
        _state_step(jq_ref, jkt_ref, jrv_ref, jqd_ref, js0_ref, jinter_ref, jsnew_ref,
                    step * js0_ref.shape[1], job_decay)


def _mix_front_kernel(h_ref, win_ref, lng_ref, lnb_ref, wmix_ref, mmask_ref, bst_ref,
                      cos_ref, sin_ref, dmask_ref, kd_ref, wa_ref,
                      vrow_ref, mg_ref, q_ref, kt_ref, rv_ref, ob_ref, oa_scr):
    _, nseq, ntok, _ = vrow_ref.shape
    rows = nseq * ntok
    h = h_ref[...]
    proj = lambda lo, hi: _dot(h, win_ref[:, lo:hi])
    cos2, sin2 = cos_ref[...], sin_ref[...]
    zv, zu = proj(OFF_V, OFF_Q), proj(OFF_U, OFF_V)
    zq, zk = proj(OFF_Q, OFF_K), proj(OFF_K, OFF_RV)
    rv = proj(OFF_RV, OFF_RG).astype(BF16)
    gate_a = jax.nn.sigmoid(proj(OFF_GA, OFF_GB))
    v = _layer_norm(_gelu_tanh(zv), lng_ref[...], lnb_ref[...])
    vrow_ref[0] = v.reshape(nseq, ntok, D_MODEL)
    u = _gelu_tanh(zu)
    qs = _rotary_heads(zq, cos2, sin2)
    ks = _rotary_heads(zk, cos2, sin2)
    kscale = np.float32(DK ** -0.5)
    heads = range(HEADS)
    kf = [ks[hd] * kscale for hd in heads]
    nt = (((1,), (1,)), ((), ()))
    for c in range(rows // CHUNK):
        rws = slice(c * CHUNK, (c + 1) * CHUNK)
        rv_ref[c] = rv[rws]
        scores = [lax.dot_general(qs[hd][rws].astype(BF16), kf[hd][rws].astype(BF16), nt,
                                  preferred_element_type=F32) for hd in heads]
        for hd in heads:
            kt_ref[c, hd * DK:(hd + 1) * DK, :] = jnp.transpose(kf[hd][rws] * kd_ref[:, hd:hd + 1])
        for hd in heads:
            ob_ref[rws, hd * DV:(hd + 1) * DV] = _dot(
                (scores[hd] * dmask_ref[hd]).astype(BF16), rv[rws, hd * DV:(hd + 1) * DV])
    for hd in heads:
        q_ref[:, hd * DK:(hd + 1) * DK] = qs[hd]
    _spatial_mix(u, v.astype(BF16), wmix_ref, mmask_ref, bst_ref, oa_scr)
    mg_ref[...] = gate_a * _dot(oa_scr[...], wa_ref[...])


def _mix_back_kernel(x_ref, g_ref, h_ref, mg_ref, ob_ref, inter_ref, wrg_ref, wgb_ref, wb_ref, wo_ref,
                     o_ref, on_scr):
    nseq, ntok, _ = x_ref.shape
    h = h_ref[...]
    for hd in range(HEADS):
        cols = slice(hd * DV, (hd + 1) * DV)
        on_scr[:, cols] = _head_rms(ob_ref[:, cols] + inter_ref[:, cols])
    ob = (on_scr[...] * _silu(_dot(h, wrg_ref[...]))).astype(BF16)
    merged = mg_ref[...] + jax.nn.sigmoid(_dot(h, wgb_ref[...])) * _dot(ob, wb_ref[...])
    out = _dot(merged.astype(BF16), wo_ref[...])
    o_ref[...] = x_ref[...] + g_ref[...] * out.reshape(nseq, ntok, D_MODEL)


def _retention_tables(seq_len):
    lc = min(seq_len, CHUNK)
    log_gamma = np.log1p(-np.power(2.0, -5.0 - np.arange(HEADS)))
    idx = np.arange(lc, dtype=np.float64)
    diff = idx[:, None] - idx[None, :]
    decay = np.where(diff >= 0, np.exp(np.maximum(diff, 0.0)[None] * log_gamma[:, None, None]), 0.0)
    q_decay = np.exp((idx + 1.0)[:, None] * log_gamma[None, :])
    k_decay = np.exp((lc - 1.0 - idx)[:, None] * log_gamma[None, :])
    s_decay = tuple(float(v) for v in np.exp(lc * log_gamma).astype(np.float32))
    rep = CHUNK // lc
    blockdiag = np.kron(np.eye(rep), np.ones((lc, lc)))
    dmask = np.tile(decay, (1, rep, rep)) * blockdiag[None]
    mmask = np.tile(np.tril(np.ones((lc, lc))), (rep, rep)) * blockdiag
    f32 = lambda a: jnp.asarray(a.astype(np.float32))
    return f32(dmask), f32(np.tile(q_decay, (rep, 1))), f32(np.tile(k_decay, (rep, 1))), s_decay, f32(mmask)


def _rotary_tables(pos, rows):
    half = DK // 2
    inv = ROPE_BASE ** (-np.arange(half, dtype=np.float64) / half)
    ang = np.asarray(pos, np.float64)[:, None] * inv[None, :]
    cos, sin = np.cos(ang), np.sin(ang)
    rep = rows // len(pos)
    cos2 = np.tile(np.concatenate([cos, cos], axis=-1), (rep, 1))
    sin2 = np.tile(np.concatenate([-sin, sin], axis=-1), (rep, 1))
    return jnp.asarray(cos2.astype(np.float32)), jnp.asarray(sin2.astype(np.float32))


def _mix_weight_specs():
    return [_const_spec((D_MODEL, OFF_END)),
            _const_spec((1, D_MODEL)), _const_spec((1, D_MODEL)),
            _const_spec((GROUPS, CHUNK, CHUNK)), _const_spec((CHUNK, CHUNK)),
            _const_spec((CHUNK, GROUPS))]


def _mix_out_weight_specs():
    return [_const_spec((HEADS, CHUNK, CHUNK)),
            _const_spec((CHUNK, HEADS)), _const_spec((CHUNK, HEADS)),
            _const_spec((D_MODEL, D_MODEL)), _const_spec((RET_V, D_MODEL)), _const_spec((D_MODEL, D_MODEL))]


def _mix_prompt(x, h, mods, mod_row0, w_in, ln_g, ln_b, gm_ws, gm_bs, w_a, w_b, w_o, job):
    nb, seq, _ = x.shape
    tm = MIX_ROWS
    nj = seq // tm
    dmask, qd, kd, s_decay, mmask = _retention_tables(seq)
    cos2, sin2 = _rotary_tables(np.arange(seq), seq)
    xspec = pl.BlockSpec((1, tm, D_MODEL), lambda b, j: (b, j, 0))
    hspec = pl.BlockSpec((tm, D_MODEL), lambda b, j: (b * nj + j, 0))
    tspec = pl.BlockSpec((tm, DK), lambda b, j: (j, 0))
    state_shape = (1, nb, HEADS, DK, DV)

    jq, jkt, jrv, jstate, ntok = job
    nseq_total = jstate.shape[1]
    nstep = nseq_total // (nb * nj)
    seq_per_chunk = CHUNK // ntok
    assert nstep * nb * nj == nseq_total and seq_per_chunk % nstep == 0 and (nstep * ntok) % F32_SUBLANES == 0
    _, jqd, _, job_decay, _ = _retention_tables(ntok)
    step = lambda b, j: b * nj + j
    chunk = lambda b, j: step(b, j) * nstep // seq_per_chunk
    jsspec = pl.BlockSpec((1, nstep, HEADS, DK, DV), lambda b, j: (0, step(b, j), 0, 0, 0))
    jispec = pl.BlockSpec((nstep * ntok, RET_V), lambda b, j: (step(b, j), 0))
    job_in = [pl.BlockSpec((nstep * ntok, D_MODEL), lambda b, j: (step(b, j), 0)),
              pl.BlockSpec((1, D_MODEL, CHUNK), lambda b, j: (chunk(b, j), 0, 0)),
              pl.BlockSpec((1, CHUNK, RET_V), lambda b, j: (chunk(b, j), 0, 0)),
              _const_spec((ntok, HEADS)), jsspec]

    return pl.pallas_call(
        functools.partial(_mix_prompt_kernel, s_decay=s_decay, job_decay=job_decay),
        grid=(nb, nj),
        in_specs=([xspec, hspec, _mod_spec(1, mod_row0, MIX_GATE_CHUNK)] + _mix_weight_specs() + [tspec, tspec]
                  + _mix_out_weight_specs() + job_in),
        out_specs=[xspec, pl.BlockSpec((1, 1, HEADS, DK, DV), lambda b, j: (0, b, 0, 0, 0)), jispec, jsspec],
        out_shape=[jax.ShapeDtypeStruct(x.shape, F32), jax.ShapeDtypeStruct(state_shape, F32),
                   jax.ShapeDtypeStruct((nseq_total * ntok, RET_V), F32), jax.ShapeDtypeStruct(jstate.shape, F32)],
        scratch_shapes=[pltpu.VMEM((tm, D_MODEL), BF16), pltpu.VMEM((tm, RET_V), F32)],
        compiler_params=pltpu.CompilerParams(
            dimension_semantics=("arbitrary", "arbitrary"), vmem_limit_bytes=VMEM_LIMIT_V7X),
        name="mix_prompt",
    )(x, h, mods, w_in, ln_g.reshape(1, D_MODEL), ln_b.reshape(1, D_MODEL),
      gm_ws, mmask, jnp.transpose(gm_bs), cos2, sin2, dmask, qd, kd, w_a, w_b, w_o,
      jq, jkt, jrv, jqd[:ntok], jstate)


def _mix_sample_front(h, nb, ntok, w_in, ln_g, ln_b, gm_ws, gm_bs, w_a):
    rows = MIX_ROWS
    nseq = rows // ntok
    nsub = rows // CHUNK
    nrow, nchunk = nb * ntok, nb * ntok // CHUNK
    rep = CHUNK // ntok
    dmask, _, kd, _, mmask = _retention_tables(ntok)
    cos2, sin2 = _rotary_tables(PAST_LEN + np.arange(ntok), rows)
    onehot = jnp.asarray(np.tile(np.eye(ntok, dtype=np.float32), (rep, 1)))
    wmix = jnp.einsum("ra,gab,cb->grc", onehot, gm_ws[:, :ntok, :ntok], onehot, precision=lax.Precision.HIGHEST)
    bst = jnp.einsum("ra,ga->rg", onehot, gm_bs[:, :ntok], precision=lax.Precision.HIGHEST)
    row_spec = lambda width: pl.BlockSpec((rows, width), lambda i: (i, 0))
    return pl.pallas_call(
        _mix_front_kernel,
        grid=(nb // nseq,),
        in_specs=([row_spec(D_MODEL)] + _mix_weight_specs()
                  + [_const_spec((rows, DK)), _const_spec((rows, DK)),
                     _const_spec((HEADS, CHUNK, CHUNK)), _const_spec((CHUNK, HEADS)),
                     _const_spec((D_MODEL, D_MODEL))]),
        out_specs=[pl.BlockSpec((1, nseq, ntok, D_MODEL), lambda i: (0, i, 0, 0)),
                   row_spec(D_MODEL), row_spec(D_MODEL),
                   pl.BlockSpec((nsub, D_MODEL, CHUNK), lambda i: (i, 0, 0)),
                   pl.BlockSpec((nsub, CHUNK, RET_V), lambda i: (i, 0, 0)),
                   row_spec(RET_V)],
        out_shape=[jax.ShapeDtypeStruct((1, nb, ntok, D_MODEL), F32),
                   jax.ShapeDtypeStruct((nrow, D_MODEL), F32),
                   jax.ShapeDtypeStruct((nrow, D_MODEL), F32),
                   jax.ShapeDtypeStruct((nchunk, D_MODEL, CHUNK), F32),
                   jax.ShapeDtypeStruct((nchunk, CHUNK, RET_V), BF16),
                   jax.ShapeDtypeStruct((nrow, RET_V), F32)],
        scratch_shapes=[pltpu.VMEM((rows, D_MODEL), BF16)],
        compiler_params=pltpu.CompilerParams(
            dimension_semantics=("arbitrary",), vmem_limit_bytes=VMEM_LIMIT_V7X),
        name="mix_sample_front",
    )(h, w_in, ln_g.reshape(1, D_MODEL), ln_b.reshape(1, D_MODEL),
      wmix, mmask, bst, cos2, sin2, dmask, kd, w_a)


def _mix_sample_back(x, mods, mod_row0, h, merged, ob, inter, w_in, w_b, w_o):
    nb, ntok, _ = x.shape
    rows = MIX_ROWS
    nseq = rows // ntok
    xspec = pl.BlockSpec((nseq, ntok, D_MODEL), lambda i: (i, 0, 0))
    row_spec = lambda width: pl.BlockSpec((rows, width), lambda i: (i, 0))
    col_spec = lambda lo, hi: pl.BlockSpec((D_MODEL, hi - lo), lambda i: (0, lo // (hi - lo)),
                                           pipeline_mode=pl.Buffered(1))
    assert OFF_RG % (OFF_GA - OFF_RG) == 0 and OFF_GB % (OFF_END - OFF_GB) == 0
    return pl.pallas_call(
        _mix_back_kernel,
        grid=(nb // nseq,),
        in_specs=[xspec, _mod_spec(nseq, mod_row0, MIX_GATE_CHUNK), row_spec(D_MODEL), row_spec(D_MODEL),
                  row_spec(RET_V), row_spec(RET_V),
                  col_spec(OFF_RG, OFF_GA), col_spec(OFF_GB, OFF_END),
                  _const_spec((RET_V, D_MODEL)), _const_spec((D_MODEL, D_MODEL))],
        out_specs=xspec,
        out_shape=jax.ShapeDtypeStruct(x.shape, F32),
        scratch_shapes=[pltpu.VMEM((rows, RET_V), F32)],
        compiler_params=pltpu.CompilerParams(
            dimension_semantics=("arbitrary",), vmem_limit_bytes=VMEM_LIMIT_V7X),
        name="mix_sample_back",
    )(x, mods, h, merged, ob, inter, w_in, w_in, w_b, w_o)


def kernel(x_prompt, x_sample, state_ret, c_prompt, c_sample, w_ada, b_ada, n1_g, w1_gate, w1_up, w1_down,
           nm_g, w_in, gm_ln_g, gm_ln_b, gm_ws, gm_bs, w_a, w_b, w_o, n2_g, w2_gate, w2_up, w2_down, final_g):
    assert w_ada.shape[0] == 1, "single-layer step"
    nbs = x_sample.shape[0]

    mods, w1 = _ada(jnp.concatenate([c_sample, c_prompt], axis=0), w_ada[0], b_ada[0],
                    cast=(w1_gate[0], w1_up[0], w1_down[0]))
    row_s, row_p = 0, nbs

    ((yp, hp), (ys, hs)), later = _ffn(
        [(x_prompt, row_p), (x_sample, row_s)], mods, FFN1_MOD_CHUNK, n1_g[0], *w1, nm_g[0], emit_next=True,
        cast=(w_in[0], w_a[0], w_b[0], w_o[0], w2_gate[0], w2_up[0], w2_down[0]))
    w_in_b, w_a_b, w_b_b, w_o_b = later[:4]
    w2 = later[4:]
    gmlp = (w_in_b, gm_ln_g[0], gm_ln_b[0], gm_ws[0], gm_bs[0], w_a_b)

    nbs, ntok, _ = x_sample.shape
    vs, merged_s, q_s, kt_s, rv_s, ob_s = _mix_sample_front(hs, nbs, ntok, *gmlp)
    yp, sp, inter_s, ss = _mix_prompt(yp, hp, mods, row_p, *gmlp, w_b_b, w_o_b,
                                      job=(q_s, kt_s, rv_s, state_ret.astype(F32), ntok))
    ys = _mix_sample_back(ys, mods, row_s, hs, merged_s, ob_s, inter_s, w_in_b, w_b_b, w_o_b)

    (yp, ys), _ = _ffn([(yp, row_p), (ys, row_s)], mods, FFN2_MOD_CHUNK, n2_g[0], *w2, final_g, emit_next=False)
    return (yp, ys, sp, ss, vs)
```

```python
import functools

import jax
import jax.numpy as jnp
import numpy as np
from jax import lax
from jax.experimental import pallas as pl
from jax.experimental.pallas import tpu as pltpu

F32 = jnp.float32
BF16 = jnp.bfloat16

D_MODEL = 1024
D_FF = 2816
N_MOD = 9
EPS = 1e-6
ROPE_BASE = 10000.0
PAST_LEN = 16384
CHUNK = 128
GROUPS = 8
GROUP_DIM = D_MODEL // GROUPS
HEADS = 8
DK = D_MODEL // HEADS
DV = 2 * DK
RET_V = HEADS * DV
OFF_U, OFF_V, OFF_Q, OFF_K, OFF_RV, OFF_RG, OFF_GA, OFF_GB, OFF_END = (
    0, 1024, 2048, 3072, 4096, 6144, 8192, 9216, 10240)

VMEM_LIMIT_V7X = 56 * 1024 * 1024
FFN_ROWS = 512
FFN_SPLIT = 2
MIX_ROWS = 256
ADA_STEPS = 8
BF16_SUBLANES = 16
F32_SUBLANES = 8
FFN1_MOD_CHUNK, MIX_GATE_CHUNK, FFN2_MOD_CHUNK = 0, 5, 6


def _dot(a, b):
    return jnp.dot(a, b, preferred_element_type=F32)


def _silu(x):
    return x * jax.nn.sigmoid(x)


def _gelu_tanh(x):
    c = np.float32(np.sqrt(2.0 / np.pi))
    return 0.5 * x * (1.0 + jnp.tanh(c * (x + 0.044715 * (x * x * x))))


def _rms_norm(x, gain):
    return x * lax.rsqrt(jnp.mean(x * x, axis=-1, keepdims=True) + EPS) * gain


def _const_spec(shape):
    nd = len(shape)
    return pl.BlockSpec(shape, lambda *_: (0,) * nd, pipeline_mode=pl.Buffered(1))


def _mod_spec(bb, row0, chunk):
    assert row0 % bb == 0
    return pl.BlockSpec((bb, 1, D_MODEL), lambda i, *_: (row0 // bb + i, 0, chunk))


def _cast_plan(weights, nsteps):
    in_specs, out_specs, out_shapes = [], [], []
    for w in weights:
        rows, cols = w.shape
        nblk = max(n for n in range(1, nsteps + 1) if rows % n == 0 and (rows // n) % BF16_SUBLANES == 0)
        spec = pl.BlockSpec((rows // nblk, cols), lambda step, nblk=nblk: (step * nblk // nsteps, 0))
        in_specs.append(spec)
        out_specs.append(spec)
        out_shapes.append(jax.ShapeDtypeStruct(w.shape, BF16))
    return in_specs, out_specs, out_shapes


def _cast_blocks(src_refs, dst_refs):
    for src, dst in zip(src_refs, dst_refs, strict=True):
        dst[...] = src[...].astype(BF16)


def _ada_kernel(c_ref, w_ref, b_ref, *refs):
    ncast = (len(refs) - 1) // 2
    o_ref = refs[ncast]
    s = _silu(c_ref[...]).astype(BF16)
    m = _dot(s, w_ref[...].astype(BF16)) + b_ref[...]
    for r in range(m.shape[0]):
        o_ref[r] = m[r:r + 1, :]
    _cast_blocks(refs[:ncast], refs[ncast + 1:])


def _ada(c_all, w_ada, b_ada, cast):
    rows = c_all.shape[0]
    n = w_ada.shape[1]
    grid = (ADA_STEPS,)
    bn = n // ADA_STEPS
    cast_in, cast_out, cast_shapes = _cast_plan(cast, ADA_STEPS)
    out = pl.pallas_call(
        _ada_kernel,
        grid=grid,
        in_specs=[pl.BlockSpec((rows, D_MODEL), lambda j: (0, 0)),
                  pl.BlockSpec((D_MODEL, bn), lambda j: (0, j)),
                  pl.BlockSpec((1, bn), lambda j: (0, j))] + cast_in,
        out_specs=[pl.BlockSpec((rows, 1, bn), lambda j: (0, 0, j))] + cast_out,
        out_shape=[jax.ShapeDtypeStruct((rows, 1, n), F32)] + cast_shapes,
        compiler_params=pltpu.CompilerParams(
            dimension_semantics=("arbitrary",), vmem_limit_bytes=VMEM_LIMIT_V7X),
        name="ada",
    )(c_all, w_ada, b_ada.reshape(1, n), *cast)
    return out[0], out[1:]


def _ffn_rows(x_ref, m_ref, ng_ref, wg_ref, wu_ref, wd_ref, eg_ref, o_ref, h_ref):
    nb, nr, _ = x_ref.shape
    chunk = lambda seqs, k: m_ref[seqs, :, k * D_MODEL:(k + 1) * D_MODEL]
    if nb > 1:
        groups = [(slice(s * nb // FFN_SPLIT, (s + 1) * nb // FFN_SPLIT), slice(None)) for s in range(FFN_SPLIT)]
    else:
        groups = [(slice(None), slice(s * nr // FFN_SPLIT, (s + 1) * nr // FFN_SPLIT)) for s in range(FFN_SPLIT)]
    xs = [x_ref[seqs, toks, :] for seqs, toks in groups]
    bb, r, _ = xs[0].shape
    hs = [(_rms_norm(x, ng_ref[...]) * (1.0 + chunk(seqs, 1)) + chunk(seqs, 0)).reshape(bb * r, D_MODEL).astype(BF16)
          for x, (seqs, _) in zip(xs, groups)]
    gates = [_dot(h, wg_ref[...]) for h in hs]
    ups = [_dot(h, wu_ref[...]) for h in hs]
    acts = [(_silu(gt) * up).astype(BF16) for gt, up in zip(gates, ups)]
    ys = [_dot(act, wd_ref[...]).reshape(bb, r, D_MODEL) for act in acts]
    for s, (x, y, (seqs, toks)) in enumerate(zip(xs, ys, groups)):
        out = x + (0.5 * chunk(seqs, 2)) * y
        if h_ref is None:
            o_ref[seqs, toks, :] = _rms_norm(out, eg_ref[...])
        else:
            o_ref[seqs, toks, :] = out
            hn = _rms_norm(out, eg_ref[...]) * (1.0 + chunk(seqs, 4)) + chunk(seqs, 3)
            h_ref[s * bb * r:(s + 1) * bb * r, :] = hn.reshape(bb * r, D_MODEL).astype(BF16)


N_FFN_WEIGHT_IN = 5


def _ffn_kernel(*refs, emit_next, batch_steps):
    nbatch = len(batch_steps)
    per_in = 2
    per_out = 2 if emit_next else 1
    n_in = nbatch * per_in + N_FFN_WEIGHT_IN
    ncast = (len(refs) - n_in - nbatch * per_out) // 2
    weights = refs[nbatch * per_in:n_in]
    outs = refs[n_in + ncast:]
    step = pl.program_id(0)
    first = 0
    for b, nsteps in enumerate(batch_steps):
        ins = refs[b * per_in:(b + 1) * per_in]
        bouts = outs[b * per_out:(b + 1) * per_out]

        @pl.when((step >= first) & (step < first + nsteps))
        def _(ins=ins, bouts=bouts):
            _ffn_rows(*ins, *weights, bouts[0], bouts[1] if emit_next else None)

        first += nsteps
    _cast_blocks(refs[n_in:n_in + ncast], outs[nbatch * per_out:])


def _ffn(batches, mods, mod_first, norm_g, wg, wu, wd, out_gain, *, emit_next, cast=()):
    operands, in_specs, out_specs, out_shapes, batch_steps = [], [], [], [], []
    first = 0
    for x, mod_row0 in batches:
        nb, r, _ = x.shape
        br = min(r, FFN_ROWS)
        bb = FFN_ROWS // br
        nj = r // br
        nsteps = (nb // bb) * nj
        assert mod_row0 % bb == 0
        local = lambda step, first=first, nsteps=nsteps: jnp.clip(step - first, 0, nsteps - 1)
        xspec = pl.BlockSpec((bb, br, D_MODEL), lambda step, local=local, nj=nj: (local(step) // nj, local(step) % nj, 0))
        nmod = 5 if emit_next else 3
        assert mod_first % nmod == 0
        mspec = pl.BlockSpec((bb, 1, nmod * D_MODEL),
                             lambda step, local=local, nj=nj, row=mod_row0 // bb: (row + local(step) // nj, 0, mod_first // nmod))
        operands += [x, mods]
        in_specs += [xspec, mspec]
        out_specs.append(xspec)
        out_shapes.append(jax.ShapeDtypeStruct(x.shape, F32))
        if emit_next:
            out_specs.append(pl.BlockSpec((bb * br, D_MODEL), lambda step, local=local: (local(step), 0)))
            out_shapes.append(jax.ShapeDtypeStruct((nb * r, D_MODEL), BF16))
        batch_steps.append(nsteps)
        first += nsteps
    cast_in, cast_out, cast_shapes = _cast_plan(cast, first)
    out = pl.pallas_call(
        functools.partial(_ffn_kernel, emit_next=emit_next, batch_steps=tuple(batch_steps)),
        grid=(first,),
        in_specs=in_specs + [
            _const_spec((1, D_MODEL)),
            _const_spec((D_MODEL, D_FF)), _const_spec((D_MODEL, D_FF)), _const_spec((D_FF, D_MODEL)),
            _const_spec((1, D_MODEL))] + cast_in,
        out_specs=out_specs + cast_out,
        out_shape=out_shapes + cast_shapes,
        compiler_params=pltpu.CompilerParams(
            dimension_semantics=("arbitrary",), vmem_limit_bytes=VMEM_LIMIT_V7X),
        name="ffn" if emit_next else "ffn_final",
    )(*operands, norm_g.reshape(1, D_MODEL), wg, wu, wd, out_gain.reshape(1, D_MODEL), *cast)
    nout = len(batches) * (2 if emit_next else 1)
    ys = [tuple(out[2 * b:2 * b + 2]) for b in range(len(batches))] if emit_next else list(out[:nout])
    return ys, out[nout:]


def _layer_norm(x, gain, bias):
    mu = jnp.mean(x, axis=-1, keepdims=True)
    var = jnp.mean(jnp.square(x - mu), axis=-1, keepdims=True)
    return (x - mu) * lax.rsqrt(var + EPS) * gain + bias


def _spatial_mix(u, vb, wmix_ref, mmask_ref, bst_ref, oa_scr):
    mmask = mmask_ref[...]
    for g in range(GROUPS):
        wg = (wmix_ref[g] * mmask).astype(BF16)
        cols = slice(g * GROUP_DIM, (g + 1) * GROUP_DIM)
        for c in range(u.shape[0] // CHUNK):
            rws = slice(c * CHUNK, (c + 1) * CHUNK)
            mix = _dot(wg, vb[rws, cols]) + bst_ref[:, g:g + 1]
            oa_scr[rws, cols] = (u[rws, cols] * mix).astype(BF16)


def _gmlp_branch(h, win_ref, lng_ref, lnb_ref, wmix_ref, mmask_ref, bst_ref, wa_ref, oa_scr):
    u = _gelu_tanh(_dot(h, win_ref[:, OFF_U:OFF_V]))
    v = _layer_norm(_gelu_tanh(_dot(h, win_ref[:, OFF_V:OFF_Q])), lng_ref[...], lnb_ref[...])
    _spatial_mix(u, v.astype(BF16), wmix_ref, mmask_ref, bst_ref, oa_scr)
    pa = _dot(oa_scr[...], wa_ref[...])
    ga = _dot(h, win_ref[:, OFF_GA:OFF_GB])
    return jax.nn.sigmoid(ga) * pa, v


def _rotary_heads(z, cos2, sin2):
    out = []
    for hd in range(HEADS):
        zh = z[:, hd * DK:(hd + 1) * DK]
        out.append(zh * cos2 + pltpu.roll(zh, DK // 2, 1) * sin2)
    return out


def _head_rms(o):
    return o * lax.rsqrt(jnp.mean(o * o, axis=-1, keepdims=True) + EPS)


def _state_step(q_ref, kt_ref, rv_ref, qd_ref, s0_ref, inter_ref, snew_ref, first_seq, s_decay):
    nstep = s0_ref.shape[1]
    ntok = qd_ref.shape[0]
    seq_per_chunk = CHUNK // ntok
    for t in range(nstep):
        tok = slice(t * ntok, (t + 1) * ntok)
        in_seq = lax.broadcasted_iota(jnp.int32, (DK, CHUNK), 1) // ntok == (first_seq + t) % seq_per_chunk
        for hd in range(HEADS):
            state = s0_ref[0, t, hd]
            qj = q_ref[tok, hd * DK:(hd + 1) * DK].astype(BF16)
            inter_ref[tok, hd * DV:(hd + 1) * DV] = _dot(qj, state.astype(BF16)) * qd_ref[:, hd:hd + 1]
            kt = jnp.where(in_seq, kt_ref[0, hd * DK:(hd + 1) * DK, :], 0.0).astype(BF16)
            upd = _dot(kt, rv_ref[0, :, hd * DV:(hd + 1) * DV])
            snew_ref[0, t, hd] = s_decay[hd] * state + upd


N_MIX_IN = 17
N_JOB_IN = 5


def _mix_prompt_kernel(*refs, s_decay, job_decay):
    (x_ref, h_ref, g_ref, win_ref, lng_ref, lnb_ref, wmix_ref, mmask_ref,
     bst_ref, cos_ref, sin_ref, dmask_ref, qd_ref, kd_ref, wa_ref, wb_ref, wo_ref) = refs[:N_MIX_IN]
    rest = refs[N_MIX_IN:]
    if job_decay is not None:
        jq_ref, jkt_ref, jrv_ref, jqd_ref, js0_ref = rest[:N_JOB_IN]
        o_ref, s_ref, jinter_ref, jsnew_ref, oa_scr, ob_scr = rest[N_JOB_IN:]
    else:
        o_ref, s_ref, oa_scr, ob_scr = rest

    @pl.when(pl.program_id(1) == 0)
    def _():
        s_ref[...] = jnp.zeros(s_ref.shape, F32)

    x = x_ref[0]
    tm = x.shape[0]
    h = h_ref[...]
    proj = lambda lo, hi: _dot(h, win_ref[:, lo:hi])
    cos2, sin2 = cos_ref[...], sin_ref[...]
    zv = proj(OFF_V, OFF_Q)
    zu = proj(OFF_U, OFF_V)
    gv = _gelu_tanh(zv)
    zq = proj(OFF_Q, OFF_K)
    v = _layer_norm(gv, lng_ref[...], lnb_ref[...])
    u = _gelu_tanh(zu)
    zk = proj(OFF_K, OFF_RV)
    qs = _rotary_heads(zq, cos2, sin2)
    rv = proj(OFF_RV, OFF_RG).astype(BF16)
    ks = _rotary_heads(zk, cos2, sin2)
    kscale = np.float32(DK ** -0.5)
    heads = range(HEADS)
    qb = [qs[hd].astype(BF16) for hd in heads]
    kf = [ks[hd] * kscale for hd in heads]
    kb = [kf[hd].astype(BF16) for hd in heads]
    nt = (((1,), (1,)), ((), ()))
    for c in range(tm // CHUNK):
        rws = slice(c * CHUNK, (c + 1) * CHUNK)
        vc = [rv[rws, hd * DV:(hd + 1) * DV] for hd in heads]
        state = [s_ref[0, 0, hd] for hd in heads]
        scores = [lax.dot_general(qb[hd][rws], kb[hd][rws], nt, preferred_element_type=F32) for hd in heads]
        inter = [_dot(qb[hd][rws], state[hd].astype(BF16)) for hd in heads]
        kdt = [jnp.transpose(kf[hd][rws] * kd_ref[:, hd:hd + 1]).astype(BF16) for hd in heads]
        upd = [_dot(kdt[hd], vc[hd]) for hd in heads]
        intra = [_dot((scores[hd] * dmask_ref[hd]).astype(BF16), vc[hd]) for hd in heads]
        for hd in heads:
            s_ref[0, 0, hd] = s_decay[hd] * state[hd] + upd[hd]
            ob_scr[rws, hd * DV:(hd + 1) * DV] = _head_rms(intra[hd] + inter[hd] * qd_ref[:, hd:hd + 1])
        if c == 0:
            gate_r = _silu(proj(OFF_RG, OFF_GA))
            gate_a = jax.nn.sigmoid(proj(OFF_GA, OFF_GB))
            _spatial_mix(u, v.astype(BF16), wmix_ref, mmask_ref, bst_ref, oa_scr)
    gate_b = jax.nn.sigmoid(proj(OFF_GB, OFF_END))
    pa = _dot(oa_scr[...], wa_ref[...])
    merged = gate_a * pa + gate_b * _dot((ob_scr[...] * gate_r).astype(BF16), wb_ref[...])
    o_ref[0] = x + g_ref[0] * _dot(merged.astype(BF16), wo_ref[...])
    if job_decay is not None:
        step = pl.program_id(0) * pl.num_programs(1) + pl.program_id(1)
        _state_step(jq_ref, jkt_ref, jrv_ref, jqd_ref, js0_ref, jinter_ref, jsnew_ref,
                    step * js0_ref.shape[1], job_decay)


def _mix_front_kernel(h_ref, win_ref, lng_ref, lnb_ref, wmix_ref, mmask_ref, bst_ref,
                      cos_ref, sin_ref, dmask_ref, kd_ref, wa_ref,
                      vrow_ref, mg_ref, q_ref, kt_ref, rv_ref, ob_ref, oa_scr):
    _, nseq, ntok, _ = vrow_ref.shape
    rows = nseq * ntok
    h = h_ref[...]
    merged, v = _gmlp_branch(h, win_ref, lng_ref, lnb_ref, wmix_ref, mmask_ref, bst_ref, wa_ref, oa_scr)
    mg_ref[...] = merged
    vrow_ref[0] = v.reshape(nseq, ntok, D_MODEL)
    cos2, sin2 = cos_ref[...], sin_ref[...]
    qs = _rotary_heads(_dot(h, win_ref[:, OFF_Q:OFF_K]), cos2, sin2)
    ks = _rotary_heads(_dot(h, win_ref[:, OFF_K:OFF_RV]), cos2, sin2)
    rv = _dot(h, win_ref[:, OFF_RV:OFF_RG]).astype(BF16)
    kscale = np.float32(DK ** -0.5)
    heads = range(HEADS)
    kf = [ks[hd] * kscale for hd in heads]
    nt = (((1,), (1,)), ((), ()))
    for c in range(rows // CHUNK):
        rws = slice(c * CHUNK, (c + 1) * CHUNK)
        rv_ref[c] = rv[rws]
        scores = [lax.dot_general(qs[hd][rws].astype(BF16), kf[hd][rws].astype(BF16), nt,
                                  preferred_element_type=F32) for hd in heads]
        for hd in heads:
            kt_ref[c, hd * DK:(hd + 1) * DK, :] = jnp.transpose(kf[hd][rws] * kd_ref[:, hd:hd + 1])
        for hd in heads:
            ob_ref[rws, hd * DV:(hd + 1) * DV] = _dot(
                (scores[hd] * dmask_ref[hd]).astype(BF16), rv[rws, hd * DV:(hd + 1) * DV])
    for hd in heads:
        q_ref[:, hd * DK:(hd + 1) * DK] = qs[hd]


def _mix_back_kernel(x_ref, g_ref, h_ref, mg_ref, ob_ref, inter_ref, wrg_ref, wgb_ref, wb_ref, wo_ref,
                     o_ref, on_scr):
    nseq, ntok, _ = x_ref.shape
    h = h_ref[...]
    for hd in range(HEADS):
        cols = slice(hd * DV, (hd + 1) * DV)
        on_scr[:, cols] = _head_rms(ob_ref[:, cols] + inter_ref[:, cols])
    ob = (on_scr[...] * _silu(_dot(h, wrg_ref[...]))).astype(BF16)
    merged = mg_ref[...] + jax.nn.sigmoid(_dot(h, wgb_ref[...])) * _dot(ob, wb_ref[...])
    out = _dot(merged.astype(BF16), wo_ref[...])
    o_ref[...] = x_ref[...] + g_ref[...] * out.reshape(nseq, ntok, D_MODEL)


def _retention_tables(seq_len):
    lc = min(seq_len, CHUNK)
    log_gamma = np.log1p(-np.power(2.0, -5.0 - np.arange(HEADS)))
    idx = np.arange(lc, dtype=np.float64)
    diff = idx[:, None] - idx[None, :]
    decay = np.where(diff >= 0, np.exp(np.maximum(diff, 0.0)[None] * log_gamma[:, None, None]), 0.0)
    q_decay = np.exp((idx + 1.0)[:, None] * log_gamma[None, :])
    k_decay = np.exp((lc - 1.0 - idx)[:, None] * log_gamma[None, :])
    s_decay = tuple(float(v) for v in np.exp(lc * log_gamma).astype(np.float32))
    rep = CHUNK // lc
    blockdiag = np.kron(np.eye(rep), np.ones((lc, lc)))
    dmask = np.tile(decay, (1, rep, rep)) * blockdiag[None]
    mmask = np.tile(np.tril(np.ones((lc, lc))), (rep, rep)) * blockdiag
    f32 = lambda a: jnp.asarray(a.astype(np.float32))
    return f32(dmask), f32(np.tile(q_decay, (rep, 1))), f32(np.tile(k_decay, (rep, 1))), s_decay, f32(mmask)


def _rotary_tables(pos, rows):
    half = DK // 2
    inv = ROPE_BASE ** (-np.arange(half, dtype=np.float64) / half)
    ang = np.asarray(pos, np.float64)[:, None] * inv[None, :]
    cos, sin = np.cos(ang), np.sin(ang)
    rep = rows // len(pos)
    cos2 = np.tile(np.concatenate([cos, cos], axis=-1), (rep, 1))
    sin2 = np.tile(np.concatenate([-sin, sin], axis=-1), (rep, 1))
    return jnp.asarray(cos2.astype(np.float32)), jnp.asarray(sin2.astype(np.float32))


def _mix_weight_specs():
    return [_const_spec((D_MODEL, OFF_END)),
            _const_spec((1, D_MODEL)), _const_spec((1, D_MODEL)),
            _const_spec((GROUPS, CHUNK, CHUNK)), _const_spec((CHUNK, CHUNK)),
            _const_spec((CHUNK, GROUPS))]


def _mix_out_weight_specs():
    return [_const_spec((HEADS, CHUNK, CHUNK)),
            _const_spec((CHUNK, HEADS)), _const_spec((CHUNK, HEADS)),
            _const_spec((D_MODEL, D_MODEL)), _const_spec((RET_V, D_MODEL)), _const_spec((D_MODEL, D_MODEL))]


def _mix_prompt(x, h, mods, mod_row0, w_in, ln_g, ln_b, gm_ws, gm_bs, w_a, w_b, w_o, job):
    nb, seq, _ = x.shape
    tm = MIX_ROWS
    nj = seq // tm
    dmask, qd, kd, s_decay, mmask = _retention_tables(seq)
    cos2, sin2 = _rotary_tables(np.arange(seq), seq)
    xspec = pl.BlockSpec((1, tm, D_MODEL), lambda b, j: (b, j, 0))
    hspec = pl.BlockSpec((tm, D_MODEL), lambda b, j: (b * nj + j, 0))
    tspec = pl.BlockSpec((tm, DK), lambda b, j: (j, 0))
    state_shape = (1, nb, HEADS, DK, DV)

    jq, jkt, jrv, jstate, ntok = job
    nseq_total = jstate.shape[1]
    nstep = nseq_total // (nb * nj)
    seq_per_chunk = CHUNK // ntok
    assert nstep * nb * nj == nseq_total and seq_per_chunk % nstep == 0 and (nstep * ntok) % F32_SUBLANES == 0
    _, jqd, _, job_decay, _ = _retention_tables(ntok)
    step = lambda b, j: b * nj + j
    chunk = lambda b, j: step(b, j) * nstep // seq_per_chunk
    jsspec = pl.BlockSpec((1, nstep, HEADS, DK, DV), lambda b, j: (0, step(b, j), 0, 0, 0))
    jispec = pl.BlockSpec((nstep * ntok, RET_V), lambda b, j: (step(b, j), 0))
    job_in = [pl.BlockSpec((nstep * ntok, D_MODEL), lambda b, j: (step(b, j), 0)),
              pl.BlockSpec((1, D_MODEL, CHUNK), lambda b, j: (chunk(b, j), 0, 0)),
              pl.BlockSpec((1, CHUNK, RET_V), lambda b, j: (chunk(b, j), 0, 0)),
              _const_spec((ntok, HEADS)), jsspec]

    return pl.pallas_call(
        functools.partial(_mix_prompt_kernel, s_decay=s_decay, job_decay=job_decay),
        grid=(nb, nj),
        in_specs=([xspec, hspec, _mod_spec(1, mod_row0, MIX_GATE_CHUNK)] + _mix_weight_specs() + [tspec, tspec]
                  + _mix_out_weight_specs() + job_in),
        out_specs=[xspec, pl.BlockSpec((1, 1, HEADS, DK, DV), lambda b, j: (0, b, 0, 0, 0)), jispec, jsspec],
        out_shape=[jax.ShapeDtypeStruct(x.shape, F32), jax.ShapeDtypeStruct(state_shape, F32),
                   jax.ShapeDtypeStruct((nseq_total * ntok, RET_V), F32), jax.ShapeDtypeStruct(jstate.shape, F32)],
        scratch_shapes=[pltpu.VMEM((tm, D_MODEL), BF16), pltpu.VMEM((tm, RET_V), F32)],
        compiler_params=pltpu.CompilerParams(
            dimension_semantics=("arbitrary", "arbitrary"), vmem_limit_bytes=VMEM_LIMIT_V7X),
        name="mix_prompt",
    )(x, h, mods, w_in, ln_g.reshape(1, D_MODEL), ln_b.reshape(1, D_MODEL),
      gm_ws, mmask, jnp.transpose(gm_bs), cos2, sin2, dmask, qd, kd, w_a, w_b, w_o,
      jq, jkt, jrv, jqd[:ntok], jstate)


def _mix_sample_front(h, nb, ntok, w_in, ln_g, ln_b, gm_ws, gm_bs, w_a):
    rows = MIX_ROWS
    nseq = rows // ntok
    nsub = rows // CHUNK
    nrow, nchunk = nb * ntok, nb * ntok // CHUNK
    rep = CHUNK // ntok
    dmask, _, kd, _, mmask = _retention_tables(ntok)
    cos2, sin2 = _rotary_tables(PAST_LEN + np.arange(ntok), rows)
    onehot = jnp.asarray(np.tile(np.eye(ntok, dtype=np.float32), (rep, 1)))
    wmix = jnp.einsum("ra,gab,cb->grc", onehot, gm_ws[:, :ntok, :ntok], onehot, precision=lax.Precision.HIGHEST)
    bst = jnp.einsum("ra,ga->rg", onehot, gm_bs[:, :ntok], precision=lax.Precision.HIGHEST)
    row_spec = lambda width: pl.BlockSpec((rows, width), lambda i: (i, 0))
    return pl.pallas_call(
        _mix_front_kernel,
        grid=(nb // nseq,),
        in_specs=([row_spec(D_MODEL)] + _mix_weight_specs()
                  + [_const_spec((rows, DK)), _const_spec((rows, DK)),
                     _const_spec((HEADS, CHUNK, CHUNK)), _const_spec((CHUNK, HEADS)),
                     _const_spec((D_MODEL, D_MODEL))]),
        out_specs=[pl.BlockSpec((1, nseq, ntok, D_MODEL), lambda i: (0, i, 0, 0)),
                   row_spec(D_MODEL), row_spec(D_MODEL),
                   pl.BlockSpec((nsub, D_MODEL, CHUNK), lambda i: (i, 0, 0)),
                   pl.BlockSpec((nsub, CHUNK, RET_V), lambda i: (i, 0, 0)),
                   row_spec(RET_V)],
        out_shape=[jax.ShapeDtypeStruct((1, nb, ntok, D_MODEL), F32),
                   jax.ShapeDtypeStruct((nrow, D_MODEL), F32),
                   jax.ShapeDtypeStruct((nrow, D_MODEL), F32),
                   jax.ShapeDtypeStruct((nchunk, D_MODEL, CHUNK), F32),
                   jax.ShapeDtypeStruct((nchunk, CHUNK, RET_V), BF16),
                   jax.ShapeDtypeStruct((nrow, RET_V), F32)],
        scratch_shapes=[pltpu.VMEM((rows, D_MODEL), BF16)],
        compiler_params=pltpu.CompilerParams(
            dimension_semantics=("arbitrary",), vmem_limit_bytes=VMEM_LIMIT_V7X),
        name="mix_sample_front",
    )(h, w_in, ln_g.reshape(1, D_MODEL), ln_b.reshape(1, D_MODEL),
      wmix, mmask, bst, cos2, sin2, dmask, kd, w_a)


def _mix_sample_back(x, mods, mod_row0, h, merged, ob, inter, w_in, w_b, w_o):
    nb, ntok, _ = x.shape
    rows = MIX_ROWS
    nseq = rows // ntok
    xspec = pl.BlockSpec((nseq, ntok, D_MODEL), lambda i: (i, 0, 0))
    row_spec = lambda width: pl.BlockSpec((rows, width), lambda i: (i, 0))
    col_spec = lambda lo, hi: pl.BlockSpec((D_MODEL, hi - lo), lambda i: (0, lo // (hi - lo)),
                                           pipeline_mode=pl.Buffered(1))
    assert OFF_RG % (OFF_GA - OFF_RG) == 0 and OFF_GB % (OFF_END - OFF_GB) == 0
    return pl.pallas_call(
        _mix_back_kernel,
        grid=(nb // nseq,),
        in_specs=[xspec, _mod_spec(nseq, mod_row0, MIX_GATE_CHUNK), row_spec(D_MODEL), row_spec(D_MODEL),
                  row_spec(RET_V), row_spec(RET_V),
                  col_spec(OFF_RG, OFF_GA), col_spec(OFF_GB, OFF_END),
                  _const_spec((RET_V, D_MODEL)), _const_spec((D_MODEL, D_MODEL))],
        out_specs=xspec,
        out_shape=jax.ShapeDtypeStruct(x.shape, F32),
        scratch_shapes=[pltpu.VMEM((rows, RET_V), F32)],
        compiler_params=pltpu.CompilerParams(
            dimension_semantics=("arbitrary",), vmem_limit_bytes=VMEM_LIMIT_V7X),
        name="mix_sample_back",
    )(x, mods, h, merged, ob, inter, w_in, w_in, w_b, w_o)


def kernel(x_prompt, x_sample, state_ret, c_prompt, c_sample, w_ada, b_ada, n1_g, w1_gate, w1_up, w1_down,
           nm_g, w_in, gm_ln_g, gm_ln_b, gm_ws, gm_bs, w_a, w_b, w_o, n2_g, w2_gate, w2_up, w2_down, final_g):
    assert w_ada.shape[0] == 1, "single-layer step"
    nbs = x_sample.shape[0]

    mods, w1 = _ada(jnp.concatenate([c_sample, c_prompt], axis=0), w_ada[0], b_ada[0],
                    cast=(w1_gate[0], w1_up[0], w1_down[0]))
    row_s, row_p = 0, nbs

    ((yp, hp), (ys, hs)), later = _ffn(
        [(x_prompt, row_p), (x_sample, row_s)], mods, FFN1_MOD_CHUNK, n1_g[0], *w1, nm_g[0], emit_next=True,
        cast=(w_in[0], w_a[0], w_b[0], w_o[0], w2_gate[0], w2_up[0], w2_down[0]))
    w_in_b, w_a_b, w_b_b, w_o_b = later[:4]
    w2 = later[4:]
    gmlp = (w_in_b, gm_ln_g[0], gm_ln_b[0], gm_ws[0], gm_bs[0], w_a_b)

    nbs, ntok, _ = x_sample.shape
    vs, merged_s, q_s, kt_s, rv_s, ob_s = _mix_sample_front(hs, nbs, ntok, *gmlp)
    yp, sp, inter_s, ss = _mix_prompt(yp, hp, mods, row_p, *gmlp, w_b_b, w_o_b,
                                      job=(q_s, kt_s, rv_s, state_ret.astype(F32), ntok))
    ys = _mix_sample_back(ys, mods, row_s, hs, merged_s, ob_s, inter_s, w_in_b, w_b_b, w_o_b)

    (yp, ys), _ = _ffn([(yp, row_p), (ys, row_s)], mods, FFN2_MOD_CHUNK, n2_g[0], *w2, final_g, emit_next=False)
    return (yp, ys, sp, ss, vs)
```

```python
import functools

import jax
import jax.numpy as jnp
import numpy as np
from jax import lax
from jax.experimental import pallas as pl
from jax.experimental.pallas import tpu as pltpu

F32 = jnp.float32
BF16 = jnp.bfloat16

D_MODEL = 1024
D_FF = 2816
N_MOD = 9
EPS = 1e-6
ROPE_BASE = 10000.0
PAST_LEN = 16384
CHUNK = 128
GROUPS = 8
GROUP_DIM = D_MODEL // GROUPS
HEADS = 8
DK = D_MODEL // HEADS
DV = 2 * DK
RET_V = HEADS * DV
OFF_U, OFF_V, OFF_Q, OFF_K, OFF_RV, OFF_RG, OFF_GA, OFF_GB, OFF_END = (
    0, 1024, 2048, 3072, 4096, 6144, 8192, 9216, 10240)

VMEM_LIMIT_V7X = 56 * 1024 * 1024
VMEM_LIMIT_LOADING_V7X = 58 * 1024 * 1024
LOAD_ROWS_IN, LOAD_ROWS_OUT = 32, 128
FFN_ROWS = 512
FFN_SPLIT = 2
MIX_ROWS = 256
ADA_STEPS = 8
BF16_SUBLANES = 16
F32_SUBLANES = 8
FFN1_MOD_CHUNK, MIX_GATE_CHUNK, FFN2_MOD_CHUNK = 0, 5, 6


def _dot(a, b):
    return jnp.dot(a, b, preferred_element_type=F32)


def _silu(x):
    return x * jax.nn.sigmoid(x)


def _gelu_tanh(x):
    c = np.float32(np.sqrt(2.0 / np.pi))
    return 0.5 * x * (1.0 + jnp.tanh(c * (x + 0.044715 * (x * x * x))))


def _rms_norm(x, gain):
    return x * lax.rsqrt(jnp.mean(x * x, axis=-1, keepdims=True) + EPS) * gain


def _const_spec(shape):
    nd = len(shape)
    return pl.BlockSpec(shape, lambda *_: (0,) * nd, pipeline_mode=pl.Buffered(1))


def _mod_spec(bb, row0, chunk):
    assert row0 % bb == 0
    return pl.BlockSpec((bb, 1, D_MODEL), lambda i, *_: (row0 // bb + i, 0, chunk))


def _cast_plan(weights, nsteps):
    in_specs, out_specs, out_shapes = [], [], []
    for w in weights:
        rows, cols = w.shape
        nblk = max(n for n in range(1, nsteps + 1) if rows % n == 0 and (rows // n) % BF16_SUBLANES == 0)
        spec = pl.BlockSpec((rows // nblk, cols), lambda step, nblk=nblk: (step * nblk // nsteps, 0))
        in_specs.append(spec)
        out_specs.append(spec)
        out_shapes.append(jax.ShapeDtypeStruct(w.shape, BF16))
    return in_specs, out_specs, out_shapes


def _cast_blocks(src_refs, dst_refs):
    for src, dst in zip(src_refs, dst_refs, strict=True):
        dst[...] = src[...].astype(BF16)


def _ada_kernel(c_ref, w_ref, b_ref, *refs):
    ncast = (len(refs) - 1) // 2
    o_ref = refs[ncast]
    s = _silu(c_ref[...]).astype(BF16)
    m = _dot(s, w_ref[...].astype(BF16)) + b_ref[...]
    for r in range(m.shape[0]):
        o_ref[r] = m[r:r + 1, :]
    _cast_blocks(refs[:ncast], refs[ncast + 1:])


def _ada(c_all, w_ada, b_ada, cast):
    rows = c_all.shape[0]
    n = w_ada.shape[1]
    grid = (ADA_STEPS,)
    bn = n // ADA_STEPS
    cast_in, cast_out, cast_shapes = _cast_plan(cast, ADA_STEPS)
    out = pl.pallas_call(
        _ada_kernel,
        grid=grid,
        in_specs=[pl.BlockSpec((rows, D_MODEL), lambda j: (0, 0)),
                  pl.BlockSpec((D_MODEL, bn), lambda j: (0, j)),
                  pl.BlockSpec((1, bn), lambda j: (0, j))] + cast_in,
        out_specs=[pl.BlockSpec((rows, 1, bn), lambda j: (0, 0, j))] + cast_out,
        out_shape=[jax.ShapeDtypeStruct((rows, 1, n), F32)] + cast_shapes,
        compiler_params=pltpu.CompilerParams(
            dimension_semantics=("arbitrary",), vmem_limit_bytes=VMEM_LIMIT_V7X),
        name="ada",
    )(c_all, w_ada, b_ada.reshape(1, n), *cast)
    return out[0], out[1:]


def _ffn_rows(x_ref, m_ref, ng_ref, wg_ref, wu_ref, wd_ref, eg_ref, o_ref, h_ref):
    nb, nr, _ = x_ref.shape
    chunk = lambda seqs, k: m_ref[seqs, :, k * D_MODEL:(k + 1) * D_MODEL]
    if nb > 1:
        groups = [(slice(s * nb // FFN_SPLIT, (s + 1) * nb // FFN_SPLIT), slice(None)) for s in range(FFN_SPLIT)]
    else:
        groups = [(slice(None), slice(s * nr // FFN_SPLIT, (s + 1) * nr // FFN_SPLIT)) for s in range(FFN_SPLIT)]
    xs = [x_ref[seqs, toks, :] for seqs, toks in groups]
    bb, r, _ = xs[0].shape
    hs = [(_rms_norm(x, ng_ref[...]) * (1.0 + chunk(seqs, 1)) + chunk(seqs, 0)).reshape(bb * r, D_MODEL).astype(BF16)
          for x, (seqs, _) in zip(xs, groups)]
    gates = [_dot(h, wg_ref[...]) for h in hs]
    ups = [_dot(h, wu_ref[...]) for h in hs]
    acts = [(_silu(gt) * up).astype(BF16) for gt, up in zip(gates, ups)]
    ys = [_dot(act, wd_ref[...]).reshape(bb, r, D_MODEL) for act in acts]
    for s, (x, y, (seqs, toks)) in enumerate(zip(xs, ys, groups)):
        out = x + (0.5 * chunk(seqs, 2)) * y
        if h_ref is None:
            o_ref[seqs, toks, :] = _rms_norm(out, eg_ref[...])
        else:
            o_ref[seqs, toks, :] = out
            hn = _rms_norm(out, eg_ref[...]) * (1.0 + chunk(seqs, 4)) + chunk(seqs, 3)
            h_ref[s * bb * r:(s + 1) * bb * r, :] = hn.reshape(bb * r, D_MODEL).astype(BF16)


N_FFN_WEIGHT_IN = 5


def _load_cast(src_hbm, dst_scr, stage, sems):
    rows = stage.shape[1]
    nchunk = src_hbm.shape[0] // rows
    copy = lambda i: pltpu.make_async_copy(src_hbm.at[pl.ds(i * rows, rows), :], stage.at[i % 2], sems.at[i % 2])
    copy(0).start()
    for i in range(nchunk):
        if i + 1 < nchunk:
            copy(i + 1).start()
        copy(i).wait()
        dst_scr[i * rows:(i + 1) * rows, :] = stage[i % 2].astype(BF16)


N_FFN_LOAD_SCRATCH = 6


def _ffn_kernel(*refs, emit_next, batch_steps, load_weights):
    nbatch = len(batch_steps)
    per_in = 2
    per_out = 2 if emit_next else 1
    n_in = nbatch * per_in + N_FFN_WEIGHT_IN
    nscratch = N_FFN_LOAD_SCRATCH if load_weights else 0
    ncast = (len(refs) - n_in - nbatch * per_out - nscratch) // 2
    weights = refs[nbatch * per_in:n_in]
    outs = refs[n_in + ncast:len(refs) - nscratch]
    step = pl.program_id(0)
    if load_weights:
        ng_ref, wg_hbm, wu_hbm, wd_hbm, eg_ref = weights
        wg_scr, wu_scr, wd_scr, stage_in, stage_out, sems = refs[len(refs) - nscratch:]

        @pl.when(step == 0)
        def _():
            _load_cast(wg_hbm, wg_scr, stage_in, sems)
            _load_cast(wu_hbm, wu_scr, stage_in, sems)
            _load_cast(wd_hbm, wd_scr, stage_out, sems)

        weights = (ng_ref, wg_scr, wu_scr, wd_scr, eg_ref)
    first = 0
    for b, nsteps in enumerate(batch_steps):
        ins = refs[b * per_in:(b + 1) * per_in]
        bouts = outs[b * per_out:(b + 1) * per_out]

        @pl.when((step >= first) & (step < first + nsteps))
        def _(ins=ins, bouts=bouts):
            _ffn_rows(*ins, *weights, bouts[0], bouts[1] if emit_next else None)

        first += nsteps
    _cast_blocks(refs[n_in:n_in + ncast], outs[nbatch * per_out:])


def _ffn(batches, mods, mod_first, norm_g, wg, wu, wd, out_gain, *, emit_next, cast=()):
    operands, in_specs, out_specs, out_shapes, batch_steps = [], [], [], [], []
    first = 0
    for x, mod_row0 in batches:
        nb, r, _ = x.shape
        br = min(r, FFN_ROWS)
        bb = FFN_ROWS // br
        nj = r // br
        nsteps = (nb // bb) * nj
        assert mod_row0 % bb == 0
        local = lambda step, first=first, nsteps=nsteps: jnp.clip(step - first, 0, nsteps - 1)
        xspec = pl.BlockSpec((bb, br, D_MODEL), lambda step, local=local, nj=nj: (local(step) // nj, local(step) % nj, 0))
        nmod = 5 if emit_next else 3
        assert mod_first % nmod == 0
        mspec = pl.BlockSpec((bb, 1, nmod * D_MODEL),
                             lambda step, local=local, nj=nj, row=mod_row0 // bb: (row + local(step) // nj, 0, mod_first // nmod))
        operands += [x, mods]
        in_specs += [xspec, mspec]
        out_specs.append(xspec)
        out_shapes.append(jax.ShapeDtypeStruct(x.shape, F32))
        if emit_next:
            out_specs.append(pl.BlockSpec((bb * br, D_MODEL), lambda step, local=local: (local(step), 0)))
            out_shapes.append(jax.ShapeDtypeStruct((nb * r, D_MODEL), BF16))
        batch_steps.append(nsteps)
        first += nsteps
    cast_in, cast_out, cast_shapes = _cast_plan(cast, first)
    load_weights = wg.dtype == F32
    if load_weights:
        hbm = pl.BlockSpec(memory_space=pl.ANY)
        wspecs = [hbm, hbm, hbm]
        scratch = [pltpu.VMEM((D_MODEL, D_FF), BF16), pltpu.VMEM((D_MODEL, D_FF), BF16), pltpu.VMEM((D_FF, D_MODEL), BF16),
                   pltpu.VMEM((2, LOAD_ROWS_IN, D_FF), F32), pltpu.VMEM((2, LOAD_ROWS_OUT, D_MODEL), F32),
                   pltpu.SemaphoreType.DMA((2,))]
        assert D_MODEL % LOAD_ROWS_IN == 0 and D_FF % LOAD_ROWS_OUT == 0
    else:
        wspecs = [_const_spec((D_MODEL, D_FF)), _const_spec((D_MODEL, D_FF)), _const_spec((D_FF, D_MODEL))]
        scratch = []
    out = pl.pallas_call(
        functools.partial(_ffn_kernel, emit_next=emit_next, batch_steps=tuple(batch_steps),
                          load_weights=load_weights),
        grid=(first,),
        in_specs=in_specs + [_const_spec((1, D_MODEL))] + wspecs + [_const_spec((1, D_MODEL))] + cast_in,
        out_specs=out_specs + cast_out,
        out_shape=out_shapes + cast_shapes,
        scratch_shapes=scratch,
        compiler_params=pltpu.CompilerParams(
            dimension_semantics=("arbitrary",),
            vmem_limit_bytes=VMEM_LIMIT_LOADING_V7X if load_weights else VMEM_LIMIT_V7X),
        name="ffn" if emit_next else "ffn_final",
    )(*operands, norm_g.reshape(1, D_MODEL), wg, wu, wd, out_gain.reshape(1, D_MODEL), *cast)
    nout = len(batches) * (2 if emit_next else 1)
    ys = [tuple(out[2 * b:2 * b + 2]) for b in range(len(batches))] if emit_next else list(out[:nout])
    return ys, out[nout:]


def _layer_norm(x, gain, bias):
    mu = jnp.mean(x, axis=-1, keepdims=True)
    var = jnp.mean(jnp.square(x - mu), axis=-1, keepdims=True)
    return (x - mu) * lax.rsqrt(var + EPS) * gain + bias


def _spatial_mix(u, vb, wmix_ref, mmask_ref, bst_ref, oa_scr):
    mmask = mmask_ref[...]
    for g in range(GROUPS):
        wg = (wmix_ref[g] * mmask).astype(BF16)
        cols = slice(g * GROUP_DIM, (g + 1) * GROUP_DIM)
        for c in range(u.shape[0] // CHUNK):
            rws = slice(c * CHUNK, (c + 1) * CHUNK)
            mix = _dot(wg, vb[rws, cols]) + bst_ref[:, g:g + 1]
            oa_scr[rws, cols] = (u[rws, cols] * mix).astype(BF16)


def _gmlp_branch(h, win_ref, lng_ref, lnb_ref, wmix_ref, mmask_ref, bst_ref, wa_ref, oa_scr):
    u = _gelu_tanh(_dot(h, win_ref[:, OFF_U:OFF_V]))
    v = _layer_norm(_gelu_tanh(_dot(h, win_ref[:, OFF_V:OFF_Q])), lng_ref[...], lnb_ref[...])
    _spatial_mix(u, v.astype(BF16), wmix_ref, mmask_ref, bst_ref, oa_scr)
    pa = _dot(oa_scr[...], wa_ref[...])
    ga = _dot(h, win_ref[:, OFF_GA:OFF_GB])
    return jax.nn.sigmoid(ga) * pa, v


def _rotary_heads(z, cos2, sin2):
    out = []
    for hd in range(HEADS):
        zh = z[:, hd * DK:(hd + 1) * DK]
        out.append(zh * cos2 + pltpu.roll(zh, DK // 2, 1) * sin2)
    return out


def _head_rms(o):
    return o * lax.rsqrt(jnp.mean(o * o, axis=-1, keepdims=True) + EPS)


def _state_step(q_ref, kt_ref, rv_ref, qd_ref, s0_ref, inter_ref, snew_ref, first_seq, s_decay):
    nstep = s0_ref.shape[1]
    ntok = qd_ref.shape[0]
    seq_per_chunk = CHUNK // ntok
    for t in range(nstep):
        tok = slice(t * ntok, (t + 1) * ntok)
        in_seq = lax.broadcasted_iota(jnp.int32, (DK, CHUNK), 1) // ntok == (first_seq + t) % seq_per_chunk
        for hd in range(HEADS):
            state = s0_ref[0, t, hd]
            qj = q_ref[tok, hd * DK:(hd + 1) * DK].astype(BF16)
            inter_ref[tok, hd * DV:(hd + 1) * DV] = _dot(qj, state.astype(BF16)) * qd_ref[:, hd:hd + 1]
            kt = jnp.where(in_seq, kt_ref[0, hd * DK:(hd + 1) * DK, :], 0.0).astype(BF16)
            upd = _dot(kt, rv_ref[0, :, hd * DV:(hd + 1) * DV])
            snew_ref[0, t, hd] = s_decay[hd] * state + upd


N_MIX_IN = 17
N_JOB_IN = 5


def _mix_prompt_kernel(*refs, s_decay, job_decay):
    (x_ref, h_ref, g_ref, win_ref, lng_ref, lnb_ref, wmix_ref, mmask_ref,
     bst_ref, cos_ref, sin_ref, dmask_ref, qd_ref, kd_ref, wa_ref, wb_ref, wo_ref) = refs[:N_MIX_IN]
    rest = refs[N_MIX_IN:]
    if job_decay is not None:
        jq_ref, jkt_ref, jrv_ref, jqd_ref, js0_ref = rest[:N_JOB_IN]
        o_ref, s_ref, jinter_ref, jsnew_ref, oa_scr, ob_scr = rest[N_JOB_IN:]
    else:
        o_ref, s_ref, oa_scr, ob_scr = rest

    @pl.when(pl.program_id(1) == 0)
    def _():
        s_ref[...] = jnp.zeros(s_ref.shape, F32)

    x = x_ref[0]
    tm = x.shape[0]
    h = h_ref[...]
    proj = lambda lo, hi: _dot(h, win_ref[:, lo:hi])
    cos2, sin2 = cos_ref[...], sin_ref[...]
    zv = proj(OFF_V, OFF_Q)
    zu = proj(OFF_U, OFF_V)
    gv = _gelu_tanh(zv)
    zq = proj(OFF_Q, OFF_K)
    v = _layer_norm(gv, lng_ref[...], lnb_ref[...])
    u = _gelu_tanh(zu)
    zk = proj(OFF_K, OFF_RV)
    qs = _rotary_heads(zq, cos2, sin2)
    rv = proj(OFF_RV, OFF_RG).astype(BF16)
    ks = _rotary_heads(zk, cos2, sin2)
    kscale = np.float32(DK ** -0.5)
    heads = range(HEADS)
    qb = [qs[hd].astype(BF16) for hd in heads]
    kf = [ks[hd] * kscale for hd in heads]
    kb = [kf[hd].astype(BF16) for hd in heads]
    nt = (((1,), (1,)), ((), ()))
    for c in range(tm // CHUNK):
        rws = slice(c * CHUNK, (c + 1) * CHUNK)
        vc = [rv[rws, hd * DV:(hd + 1) * DV] for hd in heads]
        state = [s_ref[0, 0, hd] for hd in heads]
        scores = [lax.dot_general(qb[hd][rws], kb[hd][rws], nt, preferred_element_type=F32) for hd in heads]
        inter = [_dot(qb[hd][rws], state[hd].astype(BF16)) for hd in heads]
        kdt = [jnp.transpose(kf[hd][rws] * kd_ref[:, hd:hd + 1]).astype(BF16) for hd in heads]
        upd = [_dot(kdt[hd], vc[hd]) for hd in heads]
        intra = [_dot((scores[hd] * dmask_ref[hd]).astype(BF16), vc[hd]) for hd in heads]
        for hd in heads:
            s_ref[0, 0, hd] = s_decay[hd] * state[hd] + upd[hd]
            ob_scr[rws, hd * DV:(hd + 1) * DV] = _head_rms(intra[hd] + inter[hd] * qd_ref[:, hd:hd + 1])
        if c == 0:
            gate_r = _silu(proj(OFF_RG, OFF_GA))
            gate_a = jax.nn.sigmoid(proj(OFF_GA, OFF_GB))
            _spatial_mix(u, v.astype(BF16), wmix_ref, mmask_ref, bst_ref, oa_scr)
    gate_b = jax.nn.sigmoid(proj(OFF_GB, OFF_END))
    pa = _dot(oa_scr[...], wa_ref[...])
    merged = gate_a * pa + gate_b * _dot((ob_scr[...] * gate_r).astype(BF16), wb_ref[...])
    o_ref[0] = x + g_ref[0] * _dot(merged.astype(BF16), wo_ref[...])
    if job_decay is not None:
        step = pl.program_id(0) * pl.num_programs(1) + pl.program_id(1)
        _state_step(jq_ref, jkt_ref, jrv_ref, jqd_ref, js0_ref, jinter_ref, jsnew_ref,
                    step * js0_ref.shape[1], job_decay)


def _mix_front_kernel(h_ref, win_ref, lng_ref, lnb_ref, wmix_ref, mmask_ref, bst_ref,
                      cos_ref, sin_ref, dmask_ref, kd_ref, wa_ref,
                      vrow_ref, mg_ref, q_ref, kt_ref, rv_ref, ob_ref, oa_scr):
    _, nseq, ntok, _ = vrow_ref.shape
    rows = nseq * ntok
    h = h_ref[...]
    merged, v = _gmlp_branch(h, win_ref, lng_ref, lnb_ref, wmix_ref, mmask_ref, bst_ref, wa_ref, oa_scr)
    mg_ref[...] = merged
    vrow_ref[0] = v.reshape(nseq, ntok, D_MODEL)
    cos2, sin2 = cos_ref[...], sin_ref[...]
    qs = _rotary_heads(_dot(h, win_ref[:, OFF_Q:OFF_K]), cos2, sin2)
    ks = _rotary_heads(_dot(h, win_ref[:, OFF_K:OFF_RV]), cos2, sin2)
    rv = _dot(h, win_ref[:, OFF_RV:OFF_RG]).astype(BF16)
    kscale = np.float32(DK ** -0.5)
    heads = range(HEADS)
    kf = [ks[hd] * kscale for hd in heads]
    nt = (((1,), (1,)), ((), ()))
    for c in range(rows // CHUNK):
        rws = slice(c * CHUNK, (c + 1) * CHUNK)
        rv_ref[c] = rv[rws]
        scores = [lax.dot_general(qs[hd][rws].astype(BF16), kf[hd][rws].astype(BF16), nt,
                                  preferred_element_type=F32) for hd in heads]
        for hd in heads:
            kt_ref[c, hd * DK:(hd + 1) * DK, :] = jnp.transpose(kf[hd][rws] * kd_ref[:, hd:hd + 1])
        for hd in heads:
            ob_ref[rws, hd * DV:(hd + 1) * DV] = _dot(
                (scores[hd] * dmask_ref[hd]).astype(BF16), rv[rws, hd * DV:(hd + 1) * DV])
    for hd in heads:
        q_ref[:, hd * DK:(hd + 1) * DK] = qs[hd]


def _mix_back_kernel(x_ref, g_ref, h_ref, mg_ref, ob_ref, inter_ref, wrg_ref, wgb_ref, wb_ref, wo_ref,
                     o_ref, on_scr):
    nseq, ntok, _ = x_ref.shape
    h = h_ref[...]
    for hd in range(HEADS):
        cols = slice(hd * DV, (hd + 1) * DV)
        on_scr[:, cols] = _head_rms(ob_ref[:, cols] + inter_ref[:, cols])
    ob = (on_scr[...] * _silu(_dot(h, wrg_ref[...]))).astype(BF16)
    merged = mg_ref[...] + jax.nn.sigmoid(_dot(h, wgb_ref[...])) * _dot(ob, wb_ref[...])
    out = _dot(merged.astype(BF16), wo_ref[...])
    o_ref[...] = x_ref[...] + g_ref[...] * out.reshape(nseq, ntok, D_MODEL)


def _retention_tables(seq_len):
    lc = min(seq_len, CHUNK)
    log_gamma = np.log1p(-np.power(2.0, -5.0 - np.arange(HEADS)))
    idx = np.arange(lc, dtype=np.float64)
    diff = idx[:, None] - idx[None, :]
    decay = np.where(diff >= 0, np.exp(np.maximum(diff, 0.0)[None] * log_gamma[:, None, None]), 0.0)
    q_decay = np.exp((idx + 1.0)[:, None] * log_gamma[None, :])
    k_decay = np.exp((lc - 1.0 - idx)[:, None] * log_gamma[None, :])
    s_decay = tuple(float(v) for v in np.exp(lc * log_gamma).astype(np.float32))
    rep = CHUNK // lc
    blockdiag = np.kron(np.eye(rep), np.ones((lc, lc)))
    dmask = np.tile(decay, (1, rep, rep)) * blockdiag[None]
    mmask = np.tile(np.tril(np.ones((lc, lc))), (rep, rep)) * blockdiag
    f32 = lambda a: jnp.asarray(a.astype(np.float32))
    return f32(dmask), f32(np.tile(q_decay, (rep, 1))), f32(np.tile(k_decay, (rep, 1))), s_decay, f32(mmask)


def _rotary_tables(pos, rows):
    half = DK // 2
    inv = ROPE_BASE ** (-np.arange(half, dtype=np.float64) / half)
    ang = np.asarray(pos, np.float64)[:, None] * inv[None, :]
    cos, sin = np.cos(ang), np.sin(ang)
    rep = rows // len(pos)
    cos2 = np.tile(np.concatenate([cos, cos], axis=-1), (rep, 1))
    sin2 = np.tile(np.concatenate([-sin, sin], axis=-1), (rep, 1))
    return jnp.asarray(cos2.astype(np.float32)), jnp.asarray(sin2.astype(np.float32))


def _mix_weight_specs():
    return [_const_spec((D_MODEL, OFF_END)),
            _const_spec((1, D_MODEL)), _const_spec((1, D_MODEL)),
            _const_spec((GROUPS, CHUNK, CHUNK)), _const_spec((CHUNK, CHUNK)),
            _const_spec((CHUNK, GROUPS))]


def _mix_out_weight_specs():
    return [_const_spec((HEADS, CHUNK, CHUNK)),
            _const_spec((CHUNK, HEADS)), _const_spec((CHUNK, HEADS)),
            _const_spec((D_MODEL, D_MODEL)), _const_spec((RET_V, D_MODEL)), _const_spec((D_MODEL, D_MODEL))]


def _mix_prompt(x, h, mods, mod_row0, w_in, ln_g, ln_b, gm_ws, gm_bs, w_a, w_b, w_o, job):
    nb, seq, _ = x.shape
    tm = MIX_ROWS
    nj = seq // tm
    dmask, qd, kd, s_decay, mmask = _retention_tables(seq)
    cos2, sin2 = _rotary_tables(np.arange(seq), seq)
    xspec = pl.BlockSpec((1, tm, D_MODEL), lambda b, j: (b, j, 0))
    hspec = pl.BlockSpec((tm, D_MODEL), lambda b, j: (b * nj + j, 0))
    tspec = pl.BlockSpec((tm, DK), lambda b, j: (j, 0))
    state_shape = (1, nb, HEADS, DK, DV)

    jq, jkt, jrv, jstate, ntok = job
    nseq_total = jstate.shape[1]
    nstep = nseq_total // (nb * nj)
    seq_per_chunk = CHUNK // ntok
    assert nstep * nb * nj == nseq_total and seq_per_chunk % nstep == 0 and (nstep * ntok) % F32_SUBLANES == 0
    _, jqd, _, job_decay, _ = _retention_tables(ntok)
    step = lambda b, j: b * nj + j
    chunk = lambda b, j: step(b, j) * nstep // seq_per_chunk
    jsspec = pl.BlockSpec((1, nstep, HEADS, DK, DV), lambda b, j: (0, step(b, j), 0, 0, 0))
    jispec = pl.BlockSpec((nstep * ntok, RET_V), lambda b, j: (step(b, j), 0))
    job_in = [pl.BlockSpec((nstep * ntok, D_MODEL), lambda b, j: (step(b, j), 0)),
              pl.BlockSpec((1, D_MODEL, CHUNK), lambda b, j: (chunk(b, j), 0, 0)),
              pl.BlockSpec((1, CHUNK, RET_V), lambda b, j: (chunk(b, j), 0, 0)),
              _const_spec((ntok, HEADS)), jsspec]

    return pl.pallas_call(
        functools.partial(_mix_prompt_kernel, s_decay=s_decay, job_decay=job_decay),
        grid=(nb, nj),
        in_specs=([xspec, hspec, _mod_spec(1, mod_row0, MIX_GATE_CHUNK)] + _mix_weight_specs() + [tspec, tspec]
                  + _mix_out_weight_specs() + job_in),
        out_specs=[xspec, pl.BlockSpec((1, 1, HEADS, DK, DV), lambda b, j: (0, b, 0, 0, 0)), jispec, jsspec],
        out_shape=[jax.ShapeDtypeStruct(x.shape, F32), jax.ShapeDtypeStruct(state_shape, F32),
                   jax.ShapeDtypeStruct((nseq_total * ntok, RET_V), F32), jax.ShapeDtypeStruct(jstate.shape, F32)],
        scratch_shapes=[pltpu.VMEM((tm, D_MODEL), BF16), pltpu.VMEM((tm, RET_V), F32)],
        compiler_params=pltpu.CompilerParams(
            dimension_semantics=("arbitrary", "arbitrary"), vmem_limit_bytes=VMEM_LIMIT_V7X),
        name="mix_prompt",
    )(x, h, mods, w_in, ln_g.reshape(1, D_MODEL), ln_b.reshape(1, D_MODEL),
      gm_ws, mmask, jnp.transpose(gm_bs), cos2, sin2, dmask, qd, kd, w_a, w_b, w_o,
      jq, jkt, jrv, jqd[:ntok], jstate)


def _mix_sample_front(h, nb, ntok, w_in, ln_g, ln_b, gm_ws, gm_bs, w_a):
    rows = MIX_ROWS
    nseq = rows // ntok
    nsub = rows // CHUNK
    nrow, nchunk = nb * ntok, nb * ntok // CHUNK
    rep = CHUNK // ntok
    dmask, _, kd, _, mmask = _retention_tables(ntok)
    cos2, sin2 = _rotary_tables(PAST_LEN + np.arange(ntok), rows)
    onehot = jnp.asarray(np.tile(np.eye(ntok, dtype=np.float32), (rep, 1)))
    wmix = jnp.einsum("ra,gab,cb->grc", onehot, gm_ws[:, :ntok, :ntok], onehot, precision=lax.Precision.HIGHEST)
    bst = jnp.einsum("ra,ga->rg", onehot, gm_bs[:, :ntok], precision=lax.Precision.HIGHEST)
    row_spec = lambda width: pl.BlockSpec((rows, width), lambda i: (i, 0))
    return pl.pallas_call(
        _mix_front_kernel,
        grid=(nb // nseq,),
        in_specs=([row_spec(D_MODEL)] + _mix_weight_specs()
                  + [_const_spec((rows, DK)), _const_spec((rows, DK)),
                     _const_spec((HEADS, CHUNK, CHUNK)), _const_spec((CHUNK, HEADS)),
                     _const_spec((D_MODEL, D_MODEL))]),
        out_specs=[pl.BlockSpec((1, nseq, ntok, D_MODEL), lambda i: (0, i, 0, 0)),
                   row_spec(D_MODEL), row_spec(D_MODEL),
                   pl.BlockSpec((nsub, D_MODEL, CHUNK), lambda i: (i, 0, 0)),
                   pl.BlockSpec((nsub, CHUNK, RET_V), lambda i: (i, 0, 0)),
                   row_spec(RET_V)],
        out_shape=[jax.ShapeDtypeStruct((1, nb, ntok, D_MODEL), F32),
                   jax.ShapeDtypeStruct((nrow, D_MODEL), F32),
                   jax.ShapeDtypeStruct((nrow, D_MODEL), F32),
                   jax.ShapeDtypeStruct((nchunk, D_MODEL, CHUNK), F32),
                   jax.ShapeDtypeStruct((nchunk, CHUNK, RET_V), BF16),
                   jax.ShapeDtypeStruct((nrow, RET_V), F32)],
        scratch_shapes=[pltpu.VMEM((rows, D_MODEL), BF16)],
        compiler_params=pltpu.CompilerParams(
            dimension_semantics=("arbitrary",), vmem_limit_bytes=VMEM_LIMIT_V7X),
        name="mix_sample_front",
    )(h, w_in, ln_g.reshape(1, D_MODEL), ln_b.reshape(1, D_MODEL),
      wmix, mmask, bst, cos2, sin2, dmask, kd, w_a)


def _mix_sample_back(x, mods, mod_row0, h, merged, ob, inter, w_in, w_b, w_o):
    nb, ntok, _ = x.shape
    rows = MIX_ROWS
    nseq = rows // ntok
    xspec = pl.BlockSpec((nseq, ntok, D_MODEL), lambda i: (i, 0, 0))
    row_spec = lambda width: pl.BlockSpec((rows, width), lambda i: (i, 0))
    col_spec = lambda lo, hi: pl.BlockSpec((D_MODEL, hi - lo), lambda i: (0, lo // (hi - lo)),
                                           pipeline_mode=pl.Buffered(1))
    assert OFF_RG % (OFF_GA - OFF_RG) == 0 and OFF_GB % (OFF_END - OFF_GB) == 0
    return pl.pallas_call(
        _mix_back_kernel,
        grid=(nb // nseq,),
        in_specs=[xspec, _mod_spec(nseq, mod_row0, MIX_GATE_CHUNK), row_spec(D_MODEL), row_spec(D_MODEL),
                  row_spec(RET_V), row_spec(RET_V),
                  col_spec(OFF_RG, OFF_GA), col_spec(OFF_GB, OFF_END),
                  _const_spec((RET_V, D_MODEL)), _const_spec((D_MODEL, D_MODEL))],
        out_specs=xspec,
        out_shape=jax.ShapeDtypeStruct(x.shape, F32),
        scratch_shapes=[pltpu.VMEM((rows, RET_V), F32)],
        compiler_params=pltpu.CompilerParams(
            dimension_semantics=("arbitrary",), vmem_limit_bytes=VMEM_LIMIT_V7X),
        name="mix_sample_back",
    )(x, mods, h, merged, ob, inter, w_in, w_in, w_b, w_o)


def kernel(x_prompt, x_sample, state_ret, c_prompt, c_sample, w_ada, b_ada, n1_g, w1_gate, w1_up, w1_down,
           nm_g, w_in, gm_ln_g, gm_ln_b, gm_ws, gm_bs, w_a, w_b, w_o, n2_g, w2_gate, w2_up, w2_down, final_g):
    assert w_ada.shape[0] == 1, "single-layer step"
    nbs = x_sample.shape[0]

    mods, _ = _ada(jnp.concatenate([c_sample, c_prompt], axis=0), w_ada[0], b_ada[0], cast=())
    row_s, row_p = 0, nbs

    w1 = (w1_gate[0], w1_up[0], w1_down[0])
    ((yp, hp), (ys, hs)), later = _ffn(
        [(x_prompt, row_p), (x_sample, row_s)], mods, FFN1_MOD_CHUNK, n1_g[0], *w1, nm_g[0], emit_next=True,
        cast=(w_in[0], w_a[0], w_b[0], w_o[0], w2_gate[0], w2_up[0], w2_down[0]))
    w_in_b, w_a_b, w_b_b, w_o_b = later[:4]
    w2 = later[4:]
    gmlp = (w_in_b, gm_ln_g[0], gm_ln_b[0], gm_ws[0], gm_bs[0], w_a_b)

    nbs, ntok, _ = x_sample.shape
    vs, merged_s, q_s, kt_s, rv_s, ob_s = _mix_sample_front(hs, nbs, ntok, *gmlp)
    yp, sp, inter_s, ss = _mix_prompt(yp, hp, mods, row_p, *gmlp, w_b_b, w_o_b,
                                      job=(q_s, kt_s, rv_s, state_ret.astype(F32), ntok))
    ys = _mix_sample_back(ys, mods, row_s, hs, merged_s, ob_s, inter_s, w_in_b, w_b_b, w_o_b)

    (yp, ys), _ = _ffn([(yp, row_p), (ys, row_s)], mods, FFN2_MOD_CHUNK, n2_g[0], *w2, final_g, emit_next=False)
    return (yp, ys, sp, ss, vs)
```

```python
import functools

import jax
import jax.numpy as jnp
import numpy as np
from jax import lax
from jax.experimental import pallas as pl
from jax.experimental.pallas import tpu as pltpu

F32 = jnp.float32
BF16 = jnp.bfloat16

D_MODEL = 1024
D_FF = 2816
N_MOD = 9
EPS = 1e-6
ROPE_BASE = 10000.0
PAST_LEN = 16384
CHUNK = 128
GROUPS = 8
GROUP_DIM = D_MODEL // GROUPS
HEADS = 8
DK = D_MODEL // HEADS
DV = 2 * DK
RET_V = HEADS * DV
OFF_U, OFF_V, OFF_Q, OFF_K, OFF_RV, OFF_RG, OFF_GA, OFF_GB, OFF_END = (
    0, 1024, 2048, 3072, 4096, 6144, 8192, 9216, 10240)

VMEM_LIMIT_V7X = 56 * 1024 * 1024
FFN_ROWS = 512
FFN_SPLIT = 2
MIX_ROWS = 256
ADA_STEPS = 8
ADA_IN_BUFFERS = 3
BF16_SUBLANES = 16
F32_SUBLANES = 8
FFN1_MOD_CHUNK, MIX_GATE_CHUNK, FFN2_MOD_CHUNK = 0, 5, 6


def _dot(a, b):
    return jnp.dot(a, b, preferred_element_type=F32)


def _silu(x):
    return x * jax.nn.sigmoid(x)


def _gelu_tanh(x):
    c = np.float32(np.sqrt(2.0 / np.pi))
    return 0.5 * x * (1.0 + jnp.tanh(c * (x + 0.044715 * (x * x * x))))


def _rms_norm(x, gain):
    return x * lax.rsqrt(jnp.mean(x * x, axis=-1, keepdims=True) + EPS) * gain


def _const_spec(shape):
    nd = len(shape)
    return pl.BlockSpec(shape, lambda *_: (0,) * nd, pipeline_mode=pl.Buffered(1))


def _mod_spec(bb, row0, chunk):
    assert row0 % bb == 0
    return pl.BlockSpec((bb, 1, D_MODEL), lambda i, *_: (row0 // bb + i, 0, chunk))


def _cast_plan(weights, nsteps):
    in_specs, out_specs, out_shapes = [], [], []
    for w in weights:
        rows, cols = w.shape
        nblk = max(n for n in range(1, nsteps + 1) if rows % n == 0 and (rows // n) % BF16_SUBLANES == 0)
        spec = pl.BlockSpec((rows // nblk, cols), lambda step, nblk=nblk: (step * nblk // nsteps, 0))
        in_specs.append(spec)
        out_specs.append(spec)
        out_shapes.append(jax.ShapeDtypeStruct(w.shape, BF16))
    return in_specs, out_specs, out_shapes


def _cast_blocks(src_refs, dst_refs):
    for src, dst in zip(src_refs, dst_refs, strict=True):
        dst[...] = src[...].astype(BF16)


def _ada_kernel(c_ref, w_ref, b_ref, *refs):
    ncast = (len(refs) - 1) // 2
    o_ref = refs[ncast]
    s = _silu(c_ref[...]).astype(BF16)
    m = _dot(s, w_ref[...].astype(BF16)) + b_ref[...]
    for r in range(m.shape[0]):
        o_ref[r] = m[r:r + 1, :]
    _cast_blocks(refs[:ncast], refs[ncast + 1:])


def _ada_outer(c_ref, w_hbm, b_hbm, *hbm_refs, in_specs, out_specs):
    body = functools.partial(_ada_kernel, c_ref)
    pltpu.emit_pipeline(body, grid=(ADA_STEPS,), in_specs=in_specs, out_specs=out_specs)(w_hbm, b_hbm, *hbm_refs)


def _ada(c_all, w_ada, b_ada, cast):
    rows = c_all.shape[0]
    n = w_ada.shape[1]
    bn = n // ADA_STEPS
    cast_in, cast_out, cast_shapes = _cast_plan(cast, ADA_STEPS)
    deep = lambda spec: pl.BlockSpec(spec.block_shape, spec.index_map, pipeline_mode=pl.Buffered(ADA_IN_BUFFERS))
    in_specs = [deep(pl.BlockSpec((D_MODEL, bn), lambda j: (0, j))),
                pl.BlockSpec((1, bn), lambda j: (0, j))] + [deep(s) for s in cast_in]
    out_specs = [pl.BlockSpec((rows, 1, bn), lambda j: (0, 0, j))] + cast_out
    hbm = pl.BlockSpec(memory_space=pl.ANY)
    out = pl.pallas_call(
        functools.partial(_ada_outer, in_specs=in_specs, out_specs=out_specs),
        in_specs=[pl.BlockSpec(memory_space=pltpu.VMEM)] + [hbm] * (2 + len(cast)),
        out_specs=[hbm] * (1 + len(cast)),
        out_shape=[jax.ShapeDtypeStruct((rows, 1, n), F32)] + cast_shapes,
        compiler_params=pltpu.CompilerParams(vmem_limit_bytes=VMEM_LIMIT_V7X),
        name="ada",
    )(c_all, w_ada, b_ada.reshape(1, n), *cast)
    return out[0], out[1:]


def _ffn_rows(x_ref, m_ref, ng_ref, wg_ref, wu_ref, wd_ref, eg_ref, o_ref, h_ref):
    nb, nr, _ = x_ref.shape
    chunk = lambda seqs, k: m_ref[seqs, :, k * D_MODEL:(k + 1) * D_MODEL]
    if nb > 1:
        groups = [(slice(s * nb // FFN_SPLIT, (s + 1) * nb // FFN_SPLIT), slice(None)) for s in range(FFN_SPLIT)]
    else:
        groups = [(slice(None), slice(s * nr // FFN_SPLIT, (s + 1) * nr // FFN_SPLIT)) for s in range(FFN_SPLIT)]
    xs = [x_ref[seqs, toks, :] for seqs, toks in groups]
    bb, r, _ = xs[0].shape
    hs = [(_rms_norm(x, ng_ref[...]) * (1.0 + chunk(seqs, 1)) + chunk(seqs, 0)).reshape(bb * r, D_MODEL).astype(BF16)
          for x, (seqs, _) in zip(xs, groups)]
    gates = [_dot(h, wg_ref[...]) for h in hs]
    ups = [_dot(h, wu_ref[...]) for h in hs]
    acts = [(_silu(gt) * up).astype(BF16) for gt, up in zip(gates, ups)]
    ys = [_dot(act, wd_ref[...]).reshape(bb, r, D_MODEL) for act in acts]
    for s, (x, y, (seqs, toks)) in enumerate(zip(xs, ys, groups)):
        out = x + (0.5 * chunk(seqs, 2)) * y
        if h_ref is None:
            o_ref[seqs, toks, :] = _rms_norm(out, eg_ref[...])
        else:
            o_ref[seqs, toks, :] = out
            hn = _rms_norm(out, eg_ref[...]) * (1.0 + chunk(seqs, 4)) + chunk(seqs, 3)
            h_ref[s * bb * r:(s + 1) * bb * r, :] = hn.reshape(bb * r, D_MODEL).astype(BF16)


N_FFN_WEIGHT_IN = 5


def _ffn_kernel(*refs, emit_next, batch_steps):
    nbatch = len(batch_steps)
    per_in = 2
    per_out = 2 if emit_next else 1
    n_in = nbatch * per_in + N_FFN_WEIGHT_IN
    ncast = (len(refs) - n_in - nbatch * per_out) // 2
    weights = refs[nbatch * per_in:n_in]
    outs = refs[n_in + ncast:]
    step = pl.program_id(0)
    first = 0
    for b, nsteps in enumerate(batch_steps):
        ins = refs[b * per_in:(b + 1) * per_in]
        bouts = outs[b * per_out:(b + 1) * per_out]

        @pl.when((step >= first) & (step < first + nsteps))
        def _(ins=ins, bouts=bouts):
            _ffn_rows(*ins, *weights, bouts[0], bouts[1] if emit_next else None)

        first += nsteps
    _cast_blocks(refs[n_in:n_in + ncast], outs[nbatch * per_out:])


def _ffn(batches, mods, mod_first, norm_g, wg, wu, wd, out_gain, *, emit_next, cast=()):
    operands, in_specs, out_specs, out_shapes, batch_steps = [], [], [], [], []
    first = 0
    for x, mod_row0 in batches:
        nb, r, _ = x.shape
        br = min(r, FFN_ROWS)
        bb = FFN_ROWS // br
        nj = r // br
        nsteps = (nb // bb) * nj
        assert mod_row0 % bb == 0
        local = lambda step, first=first, nsteps=nsteps: jnp.clip(step - first, 0, nsteps - 1)
        xspec = pl.BlockSpec((bb, br, D_MODEL), lambda step, local=local, nj=nj: (local(step) // nj, local(step) % nj, 0))
        nmod = 5 if emit_next else 3
        assert mod_first % nmod == 0
        mspec = pl.BlockSpec((bb, 1, nmod * D_MODEL),
                             lambda step, local=local, nj=nj, row=mod_row0 // bb: (row + local(step) // nj, 0, mod_first // nmod))
        operands += [x, mods]
        in_specs += [xspec, mspec]
        out_specs.append(xspec)
        out_shapes.append(jax.ShapeDtypeStruct(x.shape, F32))
        if emit_next:
            out_specs.append(pl.BlockSpec((bb * br, D_MODEL), lambda step, local=local: (local(step), 0)))
            out_shapes.append(jax.ShapeDtypeStruct((nb * r, D_MODEL), BF16))
        batch_steps.append(nsteps)
        first += nsteps
    cast_in, cast_out, cast_shapes = _cast_plan(cast, first)
    out = pl.pallas_call(
        functools.partial(_ffn_kernel, emit_next=emit_next, batch_steps=tuple(batch_steps)),
        grid=(first,),
        in_specs=in_specs + [
            _const_spec((1, D_MODEL)),
            _const_spec((D_MODEL, D_FF)), _const_spec((D_MODEL, D_FF)), _const_spec((D_FF, D_MODEL)),
            _const_spec((1, D_MODEL))] + cast_in,
        out_specs=out_specs + cast_out,
        out_shape=out_shapes + cast_shapes,
        compiler_params=pltpu.CompilerParams(
            dimension_semantics=("arbitrary",), vmem_limit_bytes=VMEM_LIMIT_V7X),
        name="ffn" if emit_next else "ffn_final",
    )(*operands, norm_g.reshape(1, D_MODEL), wg, wu, wd, out_gain.reshape(1, D_MODEL), *cast)
    nout = len(batches) * (2 if emit_next else 1)
    ys = [tuple(out[2 * b:2 * b + 2]) for b in range(len(batches))] if emit_next else list(out[:nout])
    return ys, out[nout:]


def _layer_norm(x, gain, bias):
    mu = jnp.mean(x, axis=-1, keepdims=True)
    var = jnp.mean(jnp.square(x - mu), axis=-1, keepdims=True)
    return (x - mu) * lax.rsqrt(var + EPS) * gain + bias


def _spatial_mix(u, vb, wmix_ref, mmask_ref, bst_ref, oa_scr):
    mmask = mmask_ref[...]
    for g in range(GROUPS):
        wg = (wmix_ref[g] * mmask).astype(BF16)
        cols = slice(g * GROUP_DIM, (g + 1) * GROUP_DIM)
        for c in range(u.shape[0] // CHUNK):
            rws = slice(c * CHUNK, (c + 1) * CHUNK)
            mix = _dot(wg, vb[rws, cols]) + bst_ref[:, g:g + 1]
            oa_scr[rws, cols] = (u[rws, cols] * mix).astype(BF16)


def _gmlp_branch(h, win_ref, lng_ref, lnb_ref, wmix_ref, mmask_ref, bst_ref, wa_ref, oa_scr):
    u = _gelu_tanh(_dot(h, win_ref[:, OFF_U:OFF_V]))
    v = _layer_norm(_gelu_tanh(_dot(h, win_ref[:, OFF_V:OFF_Q])), lng_ref[...], lnb_ref[...])
    _spatial_mix(u, v.astype(BF16), wmix_ref, mmask_ref, bst_ref, oa_scr)
    pa = _dot(oa_scr[...], wa_ref[...])
    ga = _dot(h, win_ref[:, OFF_GA:OFF_GB])
    return jax.nn.sigmoid(ga) * pa, v


def _rotary_heads(z, cos2, sin2):
    out = []
    for hd in range(HEADS):
        zh = z[:, hd * DK:(hd + 1) * DK]
        out.append(zh * cos2 + pltpu.roll(zh, DK // 2, 1) * sin2)
    return out


def _head_rms(o):
    return o * lax.rsqrt(jnp.mean(o * o, axis=-1, keepdims=True) + EPS)


def _state_step(q_ref, kt_ref, rv_ref, qd_ref, s0_ref, inter_ref, snew_ref, first_seq, s_decay):
    nstep = s0_ref.shape[1]
    ntok = qd_ref.shape[0]
    seq_per_chunk = CHUNK // ntok
    for t in range(nstep):
        tok = slice(t * ntok, (t + 1) * ntok)
        in_seq = lax.broadcasted_iota(jnp.int32, (DK, CHUNK), 1) // ntok == (first_seq + t) % seq_per_chunk
        for hd in range(HEADS):
            state = s0_ref[0, t, hd]
            qj = q_ref[tok, hd * DK:(hd + 1) * DK].astype(BF16)
            inter_ref[tok, hd * DV:(hd + 1) * DV] = _dot(qj, state.astype(BF16)) * qd_ref[:, hd:hd + 1]
            kt = jnp.where(in_seq, kt_ref[0, hd * DK:(hd + 1) * DK, :], 0.0).astype(BF16)
            upd = _dot(kt, rv_ref[0, :, hd * DV:(hd + 1) * DV])
            snew_ref[0, t, hd] = s_decay[hd] * state + upd


N_MIX_IN = 17
N_JOB_IN = 5


def _mix_prompt_kernel(*refs, s_decay, job_decay):
    (x_ref, h_ref, g_ref, win_ref, lng_ref, lnb_ref, wmix_ref, mmask_ref,
     bst_ref, cos_ref, sin_ref, dmask_ref, qd_ref, kd_ref, wa_ref, wb_ref, wo_ref) = refs[:N_MIX_IN]
    rest = refs[N_MIX_IN:]
    if job_decay is not None:
        jq_ref, jkt_ref, jrv_ref, jqd_ref, js0_ref = rest[:N_JOB_IN]
        o_ref, s_ref, jinter_ref, jsnew_ref, oa_scr, ob_scr = rest[N_JOB_IN:]
    else:
        o_ref, s_ref, oa_scr, ob_scr = rest

    @pl.when(pl.program_id(1) == 0)
    def _():
        s_ref[...] = jnp.zeros(s_ref.shape, F32)

    x = x_ref[0]
    tm = x.shape[0]
    h = h_ref[...]
    proj = lambda lo, hi: _dot(h, win_ref[:, lo:hi])
    cos2, sin2 = cos_ref[...], sin_ref[...]
    zv = proj(OFF_V, OFF_Q)
    zu = proj(OFF_U, OFF_V)
    gv = _gelu_tanh(zv)
    zq = proj(OFF_Q, OFF_K)
    v = _layer_norm(gv, lng_ref[...], lnb_ref[...])
    u = _gelu_tanh(zu)
    zk = proj(OFF_K, OFF_RV)
    qs = _rotary_heads(zq, cos2, sin2)
    rv = proj(OFF_RV, OFF_RG).astype(BF16)
    ks = _rotary_heads(zk, cos2, sin2)
    kscale = np.float32(DK ** -0.5)
    heads = range(HEADS)
    qb = [qs[hd].astype(BF16) for hd in heads]
    kf = [ks[hd] * kscale for hd in heads]
    kb = [kf[hd].astype(BF16) for hd in heads]
    nt = (((1,), (1,)), ((), ()))
    for c in range(tm // CHUNK):
        rws = slice(c * CHUNK, (c + 1) * CHUNK)
        vc = [rv[rws, hd * DV:(hd + 1) * DV] for hd in heads]
        state = [s_ref[0, 0, hd] for hd in heads]
        scores = [lax.dot_general(qb[hd][rws], kb[hd][rws], nt, preferred_element_type=F32) for hd in heads]
        inter = [_dot(qb[hd][rws], state[hd].astype(BF16)) for hd in heads]
        kdt = [jnp.transpose(kf[hd][rws] * kd_ref[:, hd:hd + 1]).astype(BF16) for hd in heads]
        upd = [_dot(kdt[hd], vc[hd]) for hd in heads]
        intra = [_dot((scores[hd] * dmask_ref[hd]).astype(BF16), vc[hd]) for hd in heads]
        for hd in heads:
            s_ref[0, 0, hd] = s_decay[hd] * state[hd] + upd[hd]
            ob_scr[rws, hd * DV:(hd + 1) * DV] = _head_rms(intra[hd] + inter[hd] * qd_ref[:, hd:hd + 1])
        if c == 0:
            gate_r = _silu(proj(OFF_RG, OFF_GA))
            gate_a = jax.nn.sigmoid(proj(OFF_GA, OFF_GB))
            _spatial_mix(u, v.astype(BF16), wmix_ref, mmask_ref, bst_ref, oa_scr)
    gate_b = jax.nn.sigmoid(proj(OFF_GB, OFF_END))
    pa = _dot(oa_scr[...], wa_ref[...])
    merged = gate_a * pa + gate_b * _dot((ob_scr[...] * gate_r).astype(BF16), wb_ref[...])
    o_ref[0] = x + g_ref[0] * _dot(merged.astype(BF16), wo_ref[...])
    if job_decay is not None:
        step = pl.program_id(0) * pl.num_programs(1) + pl.program_id(1)
        _state_step(jq_ref, jkt_ref, jrv_ref, jqd_ref, js0_ref, jinter_ref, jsnew_ref,
                    step * js0_ref.shape[1], job_decay)


def _mix_front_kernel(h_ref, win_ref, lng_ref, lnb_ref, wmix_ref, mmask_ref, bst_ref,
                      cos_ref, sin_ref, dmask_ref, kd_ref, wa_ref,
                      vrow_ref, mg_ref, q_ref, kt_ref, rv_ref, ob_ref, oa_scr):
    _, nseq, ntok, _ = vrow_ref.shape
    rows = nseq * ntok
    h = h_ref[...]
    merged, v = _gmlp_branch(h, win_ref, lng_ref, lnb_ref, wmix_ref, mmask_ref, bst_ref, wa_ref, oa_scr)
    mg_ref[...] = merged
    vrow_ref[0] = v.reshape(nseq, ntok, D_MODEL)
    cos2, sin2 = cos_ref[...], sin_ref[...]
    qs = _rotary_heads(_dot(h, win_ref[:, OFF_Q:OFF_K]), cos2, sin2)
    ks = _rotary_heads(_dot(h, win_ref[:, OFF_K:OFF_RV]), cos2, sin2)
    rv = _dot(h, win_ref[:, OFF_RV:OFF_RG]).astype(BF16)
    kscale = np.float32(DK ** -0.5)
    heads = range(HEADS)
    kf = [ks[hd] * kscale for hd in heads]
    nt = (((1,), (1,)), ((), ()))
    for c in range(rows // CHUNK):
        rws = slice(c * CHUNK, (c + 1) * CHUNK)
        rv_ref[c] = rv[rws]
        scores = [lax.dot_general(qs[hd][rws].astype(BF16), kf[hd][rws].astype(BF16), nt,
                                  preferred_element_type=F32) for hd in heads]
        for hd in heads:
            kt_ref[c, hd * DK:(hd + 1) * DK, :] = jnp.transpose(kf[hd][rws] * kd_ref[:, hd:hd + 1])
        for hd in heads:
            ob_ref[rws, hd * DV:(hd + 1) * DV] = _dot(
                (scores[hd] * dmask_ref[hd]).astype(BF16), rv[rws, hd * DV:(hd + 1) * DV])
    for hd in heads:
        q_ref[:, hd * DK:(hd + 1) * DK] = qs[hd]


def _mix_back_kernel(x_ref, g_ref, h_ref, mg_ref, ob_ref, inter_ref, wrg_ref, wgb_ref, wb_ref, wo_ref,
                     o_ref, on_scr):
    nseq, ntok, _ = x_ref.shape
    h = h_ref[...]
    for hd in range(HEADS):
        cols = slice(hd * DV, (hd + 1) * DV)
        on_scr[:, cols] = _head_rms(ob_ref[:, cols] + inter_ref[:, cols])
    ob = (on_scr[...] * _silu(_dot(h, wrg_ref[...]))).astype(BF16)
    merged = mg_ref[...] + jax.nn.sigmoid(_dot(h, wgb_ref[...])) * _dot(ob, wb_ref[...])
    out = _dot(merged.astype(BF16), wo_ref[...])
    o_ref[...] = x_ref[...] + g_ref[...] * out.reshape(nseq, ntok, D_MODEL)


def _retention_tables(seq_len):
    lc = min(seq_len, CHUNK)
    log_gamma = np.log1p(-np.power(2.0, -5.0 - np.arange(HEADS)))
    idx = np.arange(lc, dtype=np.float64)
    diff = idx[:, None] - idx[None, :]
    decay = np.where(diff >= 0, np.exp(np.maximum(diff, 0.0)[None] * log_gamma[:, None, None]), 0.0)
    q_decay = np.exp((idx + 1.0)[:, None] * log_gamma[None, :])
    k_decay = np.exp((lc - 1.0 - idx)[:, None] * log_gamma[None, :])
    s_decay = tuple(float(v) for v in np.exp(lc * log_gamma).astype(np.float32))
    rep = CHUNK // lc
    blockdiag = np.kron(np.eye(rep), np.ones((lc, lc)))
    dmask = np.tile(decay, (1, rep, rep)) * blockdiag[None]
    mmask = np.tile(np.tril(np.ones((lc, lc))), (rep, rep)) * blockdiag
    f32 = lambda a: jnp.asarray(a.astype(np.float32))
    return f32(dmask), f32(np.tile(q_decay, (rep, 1))), f32(np.tile(k_decay, (rep, 1))), s_decay, f32(mmask)


def _rotary_tables(pos, rows):
    half = DK // 2
    inv = ROPE_BASE ** (-np.arange(half, dtype=np.float64) / half)
    ang = np.asarray(pos, np.float64)[:, None] * inv[None, :]
    cos, sin = np.cos(ang), np.sin(ang)
    rep = rows // len(pos)
    cos2 = np.tile(np.concatenate([cos, cos], axis=-1), (rep, 1))
    sin2 = np.tile(np.concatenate([-sin, sin], axis=-1), (rep, 1))
    return jnp.asarray(cos2.astype(np.float32)), jnp.asarray(sin2.astype(np.float32))


def _mix_weight_specs():
    return [_const_spec((D_MODEL, OFF_END)),
            _const_spec((1, D_MODEL)), _const_spec((1, D_MODEL)),
            _const_spec((GROUPS, CHUNK, CHUNK)), _const_spec((CHUNK, CHUNK)),
            _const_spec((CHUNK, GROUPS))]


def _mix_out_weight_specs():
    return [_const_spec((HEADS, CHUNK, CHUNK)),
            _const_spec((CHUNK, HEADS)), _const_spec((CHUNK, HEADS)),
            _const_spec((D_MODEL, D_MODEL)), _const_spec((RET_V, D_MODEL)), _const_spec((D_MODEL, D_MODEL))]


def _mix_prompt(x, h, mods, mod_row0, w_in, ln_g, ln_b, gm_ws, gm_bs, w_a, w_b, w_o, job):
    nb, seq, _ = x.shape
    tm = MIX_ROWS
    nj = seq // tm
    dmask, qd, kd, s_decay, mmask = _retention_tables(seq)
    cos2, sin2 = _rotary_tables(np.arange(seq), seq)
    xspec = pl.BlockSpec((1, tm, D_MODEL), lambda b, j: (b, j, 0))
    hspec = pl.BlockSpec((tm, D_MODEL), lambda b, j: (b * nj + j, 0))
    tspec = pl.BlockSpec((tm, DK), lambda b, j: (j, 0))
    state_shape = (1, nb, HEADS, DK, DV)

    jq, jkt, jrv, jstate, ntok = job
    nseq_total = jstate.shape[1]
    nstep = nseq_total // (nb * nj)
    seq_per_chunk = CHUNK // ntok
    assert nstep * nb * nj == nseq_total and seq_per_chunk % nstep == 0 and (nstep * ntok) % F32_SUBLANES == 0
    _, jqd, _, job_decay, _ = _retention_tables(ntok)
    step = lambda b, j: b * nj + j
    chunk = lambda b, j: step(b, j) * nstep // seq_per_chunk
    jsspec = pl.BlockSpec((1, nstep, HEADS, DK, DV), lambda b, j: (0, step(b, j), 0, 0, 0))
    jispec = pl.BlockSpec((nstep * ntok, RET_V), lambda b, j: (step(b, j), 0))
    job_in = [pl.BlockSpec((nstep * ntok, D_MODEL), lambda b, j: (step(b, j), 0)),
              pl.BlockSpec((1, D_MODEL, CHUNK), lambda b, j: (chunk(b, j), 0, 0)),
              pl.BlockSpec((1, CHUNK, RET_V), lambda b, j: (chunk(b, j), 0, 0)),
              _const_spec((ntok, HEADS)), jsspec]

    return pl.pallas_call(
        functools.partial(_mix_prompt_kernel, s_decay=s_decay, job_decay=job_decay),
        grid=(nb, nj),
        in_specs=([xspec, hspec, _mod_spec(1, mod_row0, MIX_GATE_CHUNK)] + _mix_weight_specs() + [tspec, tspec]
                  + _mix_out_weight_specs() + job_in),
        out_specs=[xspec, pl.BlockSpec((1, 1, HEADS, DK, DV), lambda b, j: (0, b, 0, 0, 0)), jispec, jsspec],
        out_shape=[jax.ShapeDtypeStruct(x.shape, F32), jax.ShapeDtypeStruct(state_shape, F32),
                   jax.ShapeDtypeStruct((nseq_total * ntok, RET_V), F32), jax.ShapeDtypeStruct(jstate.shape, F32)],
        scratch_shapes=[pltpu.VMEM((tm, D_MODEL), BF16), pltpu.VMEM((tm, RET_V), F32)],
        compiler_params=pltpu.CompilerParams(
            dimension_semantics=("arbitrary", "arbitrary"), vmem_limit_bytes=VMEM_LIMIT_V7X),
        name="mix_prompt",
    )(x, h, mods, w_in, ln_g.reshape(1, D_MODEL), ln_b.reshape(1, D_MODEL),
      gm_ws, mmask, jnp.transpose(gm_bs), cos2, sin2, dmask, qd, kd, w_a, w_b, w_o,
      jq, jkt, jrv, jqd[:ntok], jstate)


def _mix_sample_front(h, nb, ntok, w_in, ln_g, ln_b, gm_ws, gm_bs, w_a):
    rows = MIX_ROWS
    nseq = rows // ntok
    nsub = rows // CHUNK
    nrow, nchunk = nb * ntok, nb * ntok // CHUNK
    rep = CHUNK // ntok
    dmask, _, kd, _, mmask = _retention_tables(ntok)
    cos2, sin2 = _rotary_tables(PAST_LEN + np.arange(ntok), rows)
    onehot = jnp.asarray(np.tile(np.eye(ntok, dtype=np.float32), (rep, 1)))
    wmix = jnp.einsum("ra,gab,cb->grc", onehot, gm_ws[:, :ntok, :ntok], onehot, precision=lax.Precision.HIGHEST)
    bst = jnp.einsum("ra,ga->rg", onehot, gm_bs[:, :ntok], precision=lax.Precision.HIGHEST)
    row_spec = lambda width: pl.BlockSpec((rows, width), lambda i: (i, 0))
    return pl.pallas_call(
        _mix_front_kernel,
        grid=(nb // nseq,),
        in_specs=([row_spec(D_MODEL)] + _mix_weight_specs()
                  + [_const_spec((rows, DK)), _const_spec((rows, DK)),
                     _const_spec((HEADS, CHUNK, CHUNK)), _const_spec((CHUNK, HEADS)),
                     _const_spec((D_MODEL, D_MODEL))]),
        out_specs=[pl.BlockSpec((1, nseq, ntok, D_MODEL), lambda i: (0, i, 0, 0)),
                   row_spec(D_MODEL), row_spec(D_MODEL),
                   pl.BlockSpec((nsub, D_MODEL, CHUNK), lambda i: (i, 0, 0)),
                   pl.BlockSpec((nsub, CHUNK, RET_V), lambda i: (i, 0, 0)),
                   row_spec(RET_V)],
        out_shape=[jax.ShapeDtypeStruct((1, nb, ntok, D_MODEL), F32),
                   jax.ShapeDtypeStruct((nrow, D_MODEL), F32),
                   jax.ShapeDtypeStruct((nrow, D_MODEL), F32),
                   jax.ShapeDtypeStruct((nchunk, D_MODEL, CHUNK), F32),
                   jax.ShapeDtypeStruct((nchunk, CHUNK, RET_V), BF16),
                   jax.ShapeDtypeStruct((nrow, RET_V), F32)],
        scratch_shapes=[pltpu.VMEM((rows, D_MODEL), BF16)],
        compiler_params=pltpu.CompilerParams(
            dimension_semantics=("arbitrary",), vmem_limit_bytes=VMEM_LIMIT_V7X),
        name="mix_sample_front",
    )(h, w_in, ln_g.reshape(1, D_MODEL), ln_b.reshape(1, D_MODEL),
      wmix, mmask, bst, cos2, sin2, dmask, kd, w_a)


def _mix_sample_back(x, mods, mod_row0, h, merged, ob, inter, w_in, w_b, w_o):
    nb, ntok, _ = x.shape
    rows = MIX_ROWS
    nseq = rows // ntok
    xspec = pl.BlockSpec((nseq, ntok, D_MODEL), lambda i: (i, 0, 0))
    row_spec = lambda width: pl.BlockSpec((rows, width), lambda i: (i, 0))
    col_spec = lambda lo, hi: pl.BlockSpec((D_MODEL, hi - lo), lambda i: (0, lo // (hi - lo)),
                                           pipeline_mode=pl.Buffered(1))
    assert OFF_RG % (OFF_GA - OFF_RG) == 0 and OFF_GB % (OFF_END - OFF_GB) == 0
    return pl.pallas_call(
        _mix_back_kernel,
        grid=(nb // nseq,),
        in_specs=[xspec, _mod_spec(nseq, mod_row0, MIX_GATE_CHUNK), row_spec(D_MODEL), row_spec(D_MODEL),
                  row_spec(RET_V), row_spec(RET_V),
                  col_spec(OFF_RG, OFF_GA), col_spec(OFF_GB, OFF_END),
                  _const_spec((RET_V, D_MODEL)), _const_spec((D_MODEL, D_MODEL))],
        out_specs=xspec,
        out_shape=jax.ShapeDtypeStruct(x.shape, F32),
        scratch_shapes=[pltpu.VMEM((rows, RET_V), F32)],
        compiler_params=pltpu.CompilerParams(
            dimension_semantics=("arbitrary",), vmem_limit_bytes=VMEM_LIMIT_V7X),
        name="mix_sample_back",
    )(x, mods, h, merged, ob, inter, w_in, w_in, w_b, w_o)


def kernel(x_prompt, x_sample, state_ret, c_prompt, c_sample, w_ada, b_ada, n1_g, w1_gate, w1_up, w1_down,
           nm_g, w_in, gm_ln_g, gm_ln_b, gm_ws, gm_bs, w_a, w_b, w_o, n2_g, w2_gate, w2_up, w2_down, final_g):
    assert w_ada.shape[0] == 1, "single-layer step"
    nbs = x_sample.shape[0]

    mods, w1 = _ada(jnp.concatenate([c_sample, c_prompt], axis=0), w_ada[0], b_ada[0],
                    cast=(w1_gate[0], w1_up[0], w1_down[0]))
    row_s, row_p = 0, nbs

    ((yp, hp), (ys, hs)), later = _ffn(
        [(x_prompt, row_p), (x_sample, row_s)], mods, FFN1_MOD_CHUNK, n1_g[0], *w1, nm_g[0], emit_next=True,
        cast=(w_in[0], w_a[0], w_b[0], w_o[0], w2_gate[0], w2_up[0], w2_down[0]))
    w_in_b, w_a_b, w_b_b, w_o_b = later[:4]
    w2 = later[4:]
    gmlp = (w_in_b, gm_ln_g[0], gm_ln_b[0], gm_ws[0], gm_bs[0], w_a_b)

    nbs, ntok, _ = x_sample.shape
    vs, merged_s, q_s, kt_s, rv_s, ob_s = _mix_sample_front(hs, nbs, ntok, *gmlp)
    yp, sp, inter_s, ss = _mix_prompt(yp, hp, mods, row_p, *gmlp, w_b_b, w_o_b,
                                      job=(q_s, kt_s, rv_s, state_ret.astype(F32), ntok))
    ys = _mix_sample_back(ys, mods, row_s, hs, merged_s, ob_s, inter_s, w_in_b, w_b_b, w_o_b)

    (yp, ys), _ = _ffn([(yp, row_p), (ys, row_s)], mods, FFN2_MOD_CHUNK, n2_g[0], *w2, final_g, emit_next=False)
    return (yp, ys, sp, ss, vs)
```

```python
import functools

import jax
import jax.numpy as jnp
import numpy as np
from jax import lax
from jax.experimental import pallas as pl
from jax.experimental.pallas import tpu as pltpu

F32 = jnp.float32
BF16 = jnp.bfloat16

D_MODEL = 1024
D_FF = 2816
N_MOD = 9
EPS = 1e-6
ROPE_BASE = 10000.0
PAST_LEN = 16384
CHUNK = 128
GROUPS = 8
GROUP_DIM = D_MODEL // GROUPS
HEADS = 8
DK = D_MODEL // HEADS
DV = 2 * DK
RET_V = HEADS * DV
OFF_U, OFF_V, OFF_Q, OFF_K, OFF_RV, OFF_RG, OFF_GA, OFF_GB, OFF_END = (
    0, 1024, 2048, 3072, 4096, 6144, 8192, 9216, 10240)

VMEM_LIMIT_V7X = 56 * 1024 * 1024
FFN_ROWS = 512
FFN_SPLIT = 2
MIX_ROWS = 256
ADA_STEPS = 8
BF16_SUBLANES = 16
F32_SUBLANES = 8
FFN1_MOD_CHUNK, MIX_GATE_CHUNK, FFN2_MOD_CHUNK = 0, 5, 6


def _dot(a, b):
    return jnp.dot(a, b, preferred_element_type=F32)


def _silu(x):
    return x * jax.nn.sigmoid(x)


def _gelu_tanh(x):
    c = np.float32(np.sqrt(2.0 / np.pi))
    return 0.5 * x * (1.0 + jnp.tanh(c * (x + 0.044715 * (x * x * x))))


def _rms_norm(x, gain):
    return x * lax.rsqrt(jnp.mean(x * x, axis=-1, keepdims=True) + EPS) * gain


def _const_spec(shape):
    nd = len(shape)
    return pl.BlockSpec(shape, lambda *_: (0,) * nd, pipeline_mode=pl.Buffered(1))


def _mod_spec(bb, row0, chunk):
    assert row0 % bb == 0
    return pl.BlockSpec((bb, 1, D_MODEL), lambda i, *_: (row0 // bb + i, 0, chunk))


def _cast_plan(weights, nsteps):
    in_specs, out_specs, out_shapes = [], [], []
    for w in weights:
        rows, cols = w.shape
        nblk = max(n for n in range(1, nsteps + 1) if rows % n == 0 and (rows // n) % BF16_SUBLANES == 0)
        spec = pl.BlockSpec((rows // nblk, cols), lambda step, nblk=nblk: (step * nblk // nsteps, 0))
        in_specs.append(spec)
        out_specs.append(spec)
        out_shapes.append(jax.ShapeDtypeStruct(w.shape, BF16))
    return in_specs, out_specs, out_shapes


def _cast_blocks(src_refs, dst_refs):
    for src, dst in zip(src_refs, dst_refs, strict=True):
        dst[...] = src[...].astype(BF16)


def _ada_kernel(c_ref, w_ref, b_ref, *refs):
    ncast = (len(refs) - 1) // 2
    o_ref = refs[ncast]
    s = _silu(c_ref[...]).astype(BF16)
    m = _dot(s, w_ref[...].astype(BF16)) + b_ref[...]
    for r in range(m.shape[0]):
        o_ref[r] = m[r:r + 1, :]
    _cast_blocks(refs[:ncast], refs[ncast + 1:])


def _ada(c_all, w_ada, b_ada, cast):
    rows = c_all.shape[0]
    n = w_ada.shape[1]
    grid = (ADA_STEPS,)
    bn = n // ADA_STEPS
    cast_in, cast_out, cast_shapes = _cast_plan(cast, ADA_STEPS)
    out = pl.pallas_call(
        _ada_kernel,
        grid=grid,
        in_specs=[pl.BlockSpec((rows, D_MODEL), lambda j: (0, 0)),
                  pl.BlockSpec((D_MODEL, bn), lambda j: (0, j)),
                  pl.BlockSpec((1, bn), lambda j: (0, j))] + cast_in,
        out_specs=[pl.BlockSpec((rows, 1, bn), lambda j: (0, 0, j))] + cast_out,
        out_shape=[jax.ShapeDtypeStruct((rows, 1, n), F32)] + cast_shapes,
        compiler_params=pltpu.CompilerParams(
            dimension_semantics=("arbitrary",), vmem_limit_bytes=VMEM_LIMIT_V7X,
            allow_input_fusion=[True] + [False] * (2 + len(cast))),
        name="ada",
    )(c_all, w_ada, b_ada.reshape(1, n), *cast)
    return out[0], out[1:]


def _ffn_rows(x_ref, m_ref, ng_ref, wg_ref, wu_ref, wd_ref, eg_ref, o_ref, h_ref):
    nb, nr, _ = x_ref.shape
    chunk = lambda seqs, k: m_ref[seqs, :, k * D_MODEL:(k + 1) * D_MODEL]
    if nb > 1:
        groups = [(slice(s * nb // FFN_SPLIT, (s + 1) * nb // FFN_SPLIT), slice(None)) for s in range(FFN_SPLIT)]
    else:
        groups = [(slice(None), slice(s * nr // FFN_SPLIT, (s + 1) * nr // FFN_SPLIT)) for s in range(FFN_SPLIT)]
    xs = [x_ref[seqs, toks, :] for seqs, toks in groups]
    bb, r, _ = xs[0].shape
    hs = [(_rms_norm(x, ng_ref[...]) * (1.0 + chunk(seqs, 1)) + chunk(seqs, 0)).reshape(bb * r, D_MODEL).astype(BF16)
          for x, (seqs, _) in zip(xs, groups)]
    gates = [_dot(h, wg_ref[...]) for h in hs]
    ups = [_dot(h, wu_ref[...]) for h in hs]
    acts = [(_silu(gt) * up).astype(BF16) for gt, up in zip(gates, ups)]
    ys = [_dot(act, wd_ref[...]).reshape(bb, r, D_MODEL) for act in acts]
    for s, (x, y, (seqs, toks)) in enumerate(zip(xs, ys, groups)):
        out = x + (0.5 * chunk(seqs, 2)) * y
        if h_ref is None:
            o_ref[seqs, toks, :] = _rms_norm(out, eg_ref[...])
        else:
            o_ref[seqs, toks, :] = out
            hn = _rms_norm(out, eg_ref[...]) * (1.0 + chunk(seqs, 4)) + chunk(seqs, 3)
            h_ref[s * bb * r:(s + 1) * bb * r, :] = hn.reshape(bb * r, D_MODEL).astype(BF16)


N_FFN_WEIGHT_IN = 5


def _ffn_kernel(*refs, emit_next, batch_steps):
    nbatch = len(batch_steps)
    per_in = 2
    per_out = 2 if emit_next else 1
    n_in = nbatch * per_in + N_FFN_WEIGHT_IN
    ncast = (len(refs) - n_in - nbatch * per_out) // 2
    weights = refs[nbatch * per_in:n_in]
    outs = refs[n_in + ncast:]
    step = pl.program_id(0)
    first = 0
    for b, nsteps in enumerate(batch_steps):
        ins = refs[b * per_in:(b + 1) * per_in]
        bouts = outs[b * per_out:(b + 1) * per_out]

        @pl.when((step >= first) & (step < first + nsteps))
        def _(ins=ins, bouts=bouts):
            _ffn_rows(*ins, *weights, bouts[0], bouts[1] if emit_next else None)

        first += nsteps
    _cast_blocks(refs[n_in:n_in + ncast], outs[nbatch * per_out:])


def _ffn(batches, mods, mod_first, norm_g, wg, wu, wd, out_gain, *, emit_next, cast=()):
    operands, in_specs, out_specs, out_shapes, batch_steps = [], [], [], [], []
    first = 0
    for x, mod_row0 in batches:
        nb, r, _ = x.shape
        br = min(r, FFN_ROWS)
        bb = FFN_ROWS // br
        nj = r // br
        nsteps = (nb // bb) * nj
        assert mod_row0 % bb == 0
        local = lambda step, first=first, nsteps=nsteps: jnp.clip(step - first, 0, nsteps - 1)
        xspec = pl.BlockSpec((bb, br, D_MODEL), lambda step, local=local, nj=nj: (local(step) // nj, local(step) % nj, 0))
        nmod = 5 if emit_next else 3
        assert mod_first % nmod == 0
        mspec = pl.BlockSpec((bb, 1, nmod * D_MODEL),
                             lambda step, local=local, nj=nj, row=mod_row0 // bb: (row + local(step) // nj, 0, mod_first // nmod))
        operands += [x, mods]
        in_specs += [xspec, mspec]
        out_specs.append(xspec)
        out_shapes.append(jax.ShapeDtypeStruct(x.shape, F32))
        if emit_next:
            out_specs.append(pl.BlockSpec((bb * br, D_MODEL), lambda step, local=local: (local(step), 0)))
            out_shapes.append(jax.ShapeDtypeStruct((nb * r, D_MODEL), BF16))
        batch_steps.append(nsteps)
        first += nsteps
    cast_in, cast_out, cast_shapes = _cast_plan(cast, first)
    out = pl.pallas_call(
        functools.partial(_ffn_kernel, emit_next=emit_next, batch_steps=tuple(batch_steps)),
        grid=(first,),
        in_specs=in_specs + [
            _const_spec((1, D_MODEL)),
            _const_spec((D_MODEL, D_FF)), _const_spec((D_MODEL, D_FF)), _const_spec((D_FF, D_MODEL)),
            _const_spec((1, D_MODEL))] + cast_in,
        out_specs=out_specs + cast_out,
        out_shape=out_shapes + cast_shapes,
        compiler_params=pltpu.CompilerParams(
            dimension_semantics=("arbitrary",), vmem_limit_bytes=VMEM_LIMIT_V7X),
        name="ffn" if emit_next else "ffn_final",
    )(*operands, norm_g.reshape(1, D_MODEL), wg, wu, wd, out_gain.reshape(1, D_MODEL), *cast)
    nout = len(batches) * (2 if emit_next else 1)
    ys = [tuple(out[2 * b:2 * b + 2]) for b in range(len(batches))] if emit_next else list(out[:nout])
    return ys, out[nout:]


def _layer_norm(x, gain, bias):
    mu = jnp.mean(x, axis=-1, keepdims=True)
    var = jnp.mean(jnp.square(x - mu), axis=-1, keepdims=True)
    return (x - mu) * lax.rsqrt(var + EPS) * gain + bias


def _spatial_mix(u, vb, wmix_ref, mmask_ref, bst_ref, oa_scr):
    mmask = mmask_ref[...]
    for g in range(GROUPS):
        wg = (wmix_ref[g] * mmask).astype(BF16)
        cols = slice(g * GROUP_DIM, (g + 1) * GROUP_DIM)
        for c in range(u.shape[0] // CHUNK):
            rws = slice(c * CHUNK, (c + 1) * CHUNK)
            mix = _dot(wg, vb[rws, cols]) + bst_ref[:, g:g + 1]
            oa_scr[rws, cols] = (u[rws, cols] * mix).astype(BF16)


def _gmlp_branch(h, win_ref, lng_ref, lnb_ref, wmix_ref, mmask_ref, bst_ref, wa_ref, oa_scr):
    u = _gelu_tanh(_dot(h, win_ref[:, OFF_U:OFF_V]))
    v = _layer_norm(_gelu_tanh(_dot(h, win_ref[:, OFF_V:OFF_Q])), lng_ref[...], lnb_ref[...])
    _spatial_mix(u, v.astype(BF16), wmix_ref, mmask_ref, bst_ref, oa_scr)
    pa = _dot(oa_scr[...], wa_ref[...])
    ga = _dot(h, win_ref[:, OFF_GA:OFF_GB])
    return jax.nn.sigmoid(ga) * pa, v


def _rotary_heads(z, cos2, sin2):
    out = []
    for hd in range(HEADS):
        zh = z[:, hd * DK:(hd + 1) * DK]
        out.append(zh * cos2 + pltpu.roll(zh, DK // 2, 1) * sin2)
    return out


def _head_rms(o):
    return o * lax.rsqrt(jnp.mean(o * o, axis=-1, keepdims=True) + EPS)


def _state_step(q_ref, kt_ref, rv_ref, qd_ref, s0_ref, inter_ref, snew_ref, first_seq, s_decay):
    nstep = s0_ref.shape[1]
    ntok = qd_ref.shape[0]
    seq_per_chunk = CHUNK // ntok
    for t in range(nstep):
        tok = slice(t * ntok, (t + 1) * ntok)
        in_seq = lax.broadcasted_iota(jnp.int32, (DK, CHUNK), 1) // ntok == (first_seq + t) % seq_per_chunk
        for hd in range(HEADS):
            state = s0_ref[0, t, hd]
            qj = q_ref[tok, hd * DK:(hd + 1) * DK].astype(BF16)
            inter_ref[tok, hd * DV:(hd + 1) * DV] = _dot(qj, state.astype(BF16)) * qd_ref[:, hd:hd + 1]
            kt = jnp.where(in_seq, kt_ref[0, hd * DK:(hd + 1) * DK, :], 0.0).astype(BF16)
            upd = _dot(kt, rv_ref[0, :, hd * DV:(hd + 1) * DV])
            snew_ref[0, t, hd] = s_decay[hd] * state + upd


N_MIX_IN = 17
N_JOB_IN = 5


def _mix_prompt_kernel(*refs, s_decay, job_decay):
    (x_ref, h_ref, g_ref, win_ref, lng_ref, lnb_ref, wmix_ref, mmask_ref,
     bst_ref, cos_ref, sin_ref, dmask_ref, qd_ref, kd_ref, wa_ref, wb_ref, wo_ref) = refs[:N_MIX_IN]
    rest = refs[N_MIX_IN:]
    if job_decay is not None:
        jq_ref, jkt_ref, jrv_ref, jqd_ref, js0_ref = rest[:N_JOB_IN]
        o_ref, s_ref, jinter_ref, jsnew_ref, oa_scr, ob_scr = rest[N_JOB_IN:]
    else:
        o_ref, s_ref, oa_scr, ob_scr = rest

    @pl.when(pl.program_id(1) == 0)
    def _():
        s_ref[...] = jnp.zeros(s_ref.shape, F32)

    x = x_ref[0]
    tm = x.shape[0]
    h = h_ref[...]
    proj = lambda lo, hi: _dot(h, win_ref[:, lo:hi])
    cos2, sin2 = cos_ref[...], sin_ref[...]
    zv = proj(OFF_V, OFF_Q)
    zu = proj(OFF_U, OFF_V)
    gv = _gelu_tanh(zv)
    zq = proj(OFF_Q, OFF_K)
    v = _layer_norm(gv, lng_ref[...], lnb_ref[...])
    u = _gelu_tanh(zu)
    zk = proj(OFF_K, OFF_RV)
    qs = _rotary_heads(zq, cos2, sin2)
    rv = proj(OFF_RV, OFF_RG).astype(BF16)
    ks = _rotary_heads(zk, cos2, sin2)
    kscale = np.float32(DK ** -0.5)
    heads = range(HEADS)
    qb = [qs[hd].astype(BF16) for hd in heads]
    kf = [ks[hd] * kscale for hd in heads]
    kb = [kf[hd].astype(BF16) for hd in heads]
    nt = (((1,), (1,)), ((), ()))
    for c in range(tm // CHUNK):
        rws = slice(c * CHUNK, (c + 1) * CHUNK)
        vc = [rv[rws, hd * DV:(hd + 1) * DV] for hd in heads]
        state = [s_ref[0, 0, hd] for hd in heads]
        scores = [lax.dot_general(qb[hd][rws], kb[hd][rws], nt, preferred_element_type=F32) for hd in heads]
        inter = [_dot(qb[hd][rws], state[hd].astype(BF16)) for hd in heads]
        kdt = [jnp.transpose(kf[hd][rws] * kd_ref[:, hd:hd + 1]).astype(BF16) for hd in heads]
        upd = [_dot(kdt[hd], vc[hd]) for hd in heads]
        intra = [_dot((scores[hd] * dmask_ref[hd]).astype(BF16), vc[hd]) for hd in heads]
        for hd in heads:
            s_ref[0, 0, hd] = s_decay[hd] * state[hd] + upd[hd]
            ob_scr[rws, hd * DV:(hd + 1) * DV] = _head_rms(intra[hd] + inter[hd] * qd_ref[:, hd:hd + 1])
        if c == 0:
            gate_r = _silu(proj(OFF_RG, OFF_GA))
            gate_a = jax.nn.sigmoid(proj(OFF_GA, OFF_GB))
            _spatial_mix(u, v.astype(BF16), wmix_ref, mmask_ref, bst_ref, oa_scr)
    gate_b = jax.nn.sigmoid(proj(OFF_GB, OFF_END))
    pa = _dot(oa_scr[...], wa_ref[...])
    merged = gate_a * pa + gate_b * _dot((ob_scr[...] * gate_r).astype(BF16), wb_ref[...])
    o_ref[0] = x + g_ref[0] * _dot(merged.astype(BF16), wo_ref[...])
    if job_decay is not None:
        step = pl.program_id(0) * pl.num_programs(1) + pl.program_id(1)
        _state_step(jq_ref, jkt_ref, jrv_ref, jqd_ref, js0_ref, jinter_ref, jsnew_ref,
                    step * js0_ref.shape[1], job_decay)


def _mix_front_kernel(h_ref, win_ref, lng_ref, lnb_ref, wmix_ref, mmask_ref, bst_ref,
                      cos_ref, sin_ref, dmask_ref, kd_ref, wa_ref,
                      vrow_ref, mg_ref, q_ref, kt_ref, rv_ref, ob_ref, oa_scr):
    _, nseq, ntok, _ = vrow_ref.shape
    rows = nseq * ntok
    h = h_ref[...]
    merged, v = _gmlp_branch(h, win_ref, lng_ref, lnb_ref, wmix_ref, mmask_ref, bst_ref, wa_ref, oa_scr)
    mg_ref[...] = merged
    vrow_ref[0] = v.reshape(nseq, ntok, D_MODEL)
    cos2, sin2 = cos_ref[...], sin_ref[...]
    qs = _rotary_heads(_dot(h, win_ref[:, OFF_Q:OFF_K]), cos2, sin2)
    ks = _rotary_heads(_dot(h, win_ref[:, OFF_K:OFF_RV]), cos2, sin2)
    rv = _dot(h, win_ref[:, OFF_RV:OFF_RG]).astype(BF16)
    kscale = np.float32(DK ** -0.5)
    heads = range(HEADS)
    kf = [ks[hd] * kscale for hd in heads]
    nt = (((1,), (1,)), ((), ()))
    for c in range(rows // CHUNK):
        rws = slice(c * CHUNK, (c + 1) * CHUNK)
        rv_ref[c] = rv[rws]
        scores = [lax.dot_general(qs[hd][rws].astype(BF16), kf[hd][rws].astype(BF16), nt,
                                  preferred_element_type=F32) for hd in heads]
        for hd in heads:
            kt_ref[c, hd * DK:(hd + 1) * DK, :] = jnp.transpose(kf[hd][rws] * kd_ref[:, hd:hd + 1])
        for hd in heads:
            ob_ref[rws, hd * DV:(hd + 1) * DV] = _dot(
                (scores[hd] * dmask_ref[hd]).astype(BF16), rv[rws, hd * DV:(hd + 1) * DV])
    for hd in heads:
        q_ref[:, hd * DK:(hd + 1) * DK] = qs[hd]


def _mix_back_kernel(x_ref, g_ref, h_ref, mg_ref, ob_ref, inter_ref, wrg_ref, wgb_ref, wb_ref, wo_ref,
                     o_ref, on_scr):
    nseq, ntok, _ = x_ref.shape
    h = h_ref[...]
    for hd in range(HEADS):
        cols = slice(hd * DV, (hd + 1) * DV)
        on_scr[:, cols] = _head_rms(ob_ref[:, cols] + inter_ref[:, cols])
    ob = (on_scr[...] * _silu(_dot(h, wrg_ref[...]))).astype(BF16)
    merged = mg_ref[...] + jax.nn.sigmoid(_dot(h, wgb_ref[...])) * _dot(ob, wb_ref[...])
    out = _dot(merged.astype(BF16), wo_ref[...])
    o_ref[...] = x_ref[...] + g_ref[...] * out.reshape(nseq, ntok, D_MODEL)


def _retention_tables(seq_len):
    lc = min(seq_len, CHUNK)
    log_gamma = np.log1p(-np.power(2.0, -5.0 - np.arange(HEADS)))
    idx = np.arange(lc, dtype=np.float64)
    diff = idx[:, None] - idx[None, :]
    decay = np.where(diff >= 0, np.exp(np.maximum(diff, 0.0)[None] * log_gamma[:, None, None]), 0.0)
    q_decay = np.exp((idx + 1.0)[:, None] * log_gamma[None, :])
    k_decay = np.exp((lc - 1.0 - idx)[:, None] * log_gamma[None, :])
    s_decay = tuple(float(v) for v in np.exp(lc * log_gamma).astype(np.float32))
    rep = CHUNK // lc
    blockdiag = np.kron(np.eye(rep), np.ones((lc, lc)))
    dmask = np.tile(decay, (1, rep, rep)) * blockdiag[None]
    mmask = np.tile(np.tril(np.ones((lc, lc))), (rep, rep)) * blockdiag
    f32 = lambda a: jnp.asarray(a.astype(np.float32))
    return f32(dmask), f32(np.tile(q_decay, (rep, 1))), f32(np.tile(k_decay, (rep, 1))), s_decay, f32(mmask)


def _rotary_tables(pos, rows):
    half = DK // 2
    inv = ROPE_BASE ** (-np.arange(half, dtype=np.float64) / half)
    ang = np.asarray(pos, np.float64)[:, None] * inv[None, :]
    cos, sin = np.cos(ang), np.sin(ang)
    rep = rows // len(pos)
    cos2 = np.tile(np.concatenate([cos, cos], axis=-1), (rep, 1))
    sin2 = np.tile(np.concatenate([-sin, sin], axis=-1), (rep, 1))
    return jnp.asarray(cos2.astype(np.float32)), jnp.asarray(sin2.astype(np.float32))


def _mix_weight_specs():
    return [_const_spec((D_MODEL, OFF_END)),
            _const_spec((1, D_MODEL)), _const_spec((1, D_MODEL)),
            _const_spec((GROUPS, CHUNK, CHUNK)), _const_spec((CHUNK, CHUNK)),
            _const_spec((CHUNK, GROUPS))]


def _mix_out_weight_specs():
    return [_const_spec((HEADS, CHUNK, CHUNK)),
            _const_spec((CHUNK, HEADS)), _const_spec((CHUNK, HEADS)),
            _const_spec((D_MODEL, D_MODEL)), _const_spec((RET_V, D_MODEL)), _const_spec((D_MODEL, D_MODEL))]


def _mix_prompt(x, h, mods, mod_row0, w_in, ln_g, ln_b, gm_ws, gm_bs, w_a, w_b, w_o, job):
    nb, seq, _ = x.shape
    tm = MIX_ROWS
    nj = seq // tm
    dmask, qd, kd, s_decay, mmask = _retention_tables(seq)
    cos2, sin2 = _rotary_tables(np.arange(seq), seq)
    xspec = pl.BlockSpec((1, tm, D_MODEL), lambda b, j: (b, j, 0))
    hspec = pl.BlockSpec((tm, D_MODEL), lambda b, j: (b * nj + j, 0))
    tspec = pl.BlockSpec((tm, DK), lambda b, j: (j, 0))
    state_shape = (1, nb, HEADS, DK, DV)

    jq, jkt, jrv, jstate, ntok = job
    nseq_total = jstate.shape[1]
    nstep = nseq_total // (nb * nj)
    seq_per_chunk = CHUNK // ntok
    assert nstep * nb * nj == nseq_total and seq_per_chunk % nstep == 0 and (nstep * ntok) % F32_SUBLANES == 0
    _, jqd, _, job_decay, _ = _retention_tables(ntok)
    step = lambda b, j: b * nj + j
    chunk = lambda b, j: step(b, j) * nstep // seq_per_chunk
    jsspec = pl.BlockSpec((1, nstep, HEADS, DK, DV), lambda b, j: (0, step(b, j), 0, 0, 0))
    jispec = pl.BlockSpec((nstep * ntok, RET_V), lambda b, j: (step(b, j), 0))
    job_in = [pl.BlockSpec((nstep * ntok, D_MODEL), lambda b, j: (step(b, j), 0)),
              pl.BlockSpec((1, D_MODEL, CHUNK), lambda b, j: (chunk(b, j), 0, 0)),
              pl.BlockSpec((1, CHUNK, RET_V), lambda b, j: (chunk(b, j), 0, 0)),
              _const_spec((ntok, HEADS)), jsspec]

    return pl.pallas_call(
        functools.partial(_mix_prompt_kernel, s_decay=s_decay, job_decay=job_decay),
        grid=(nb, nj),
        in_specs=([xspec, hspec, _mod_spec(1, mod_row0, MIX_GATE_CHUNK)] + _mix_weight_specs() + [tspec, tspec]
                  + _mix_out_weight_specs() + job_in),
        out_specs=[xspec, pl.BlockSpec((1, 1, HEADS, DK, DV), lambda b, j: (0, b, 0, 0, 0)), jispec, jsspec],
        out_shape=[jax.ShapeDtypeStruct(x.shape, F32), jax.ShapeDtypeStruct(state_shape, F32),
                   jax.ShapeDtypeStruct((nseq_total * ntok, RET_V), F32), jax.ShapeDtypeStruct(jstate.shape, F32)],
        scratch_shapes=[pltpu.VMEM((tm, D_MODEL), BF16), pltpu.VMEM((tm, RET_V), F32)],
        compiler_params=pltpu.CompilerParams(
            dimension_semantics=("arbitrary", "arbitrary"), vmem_limit_bytes=VMEM_LIMIT_V7X),
        name="mix_prompt",
    )(x, h, mods, w_in, ln_g.reshape(1, D_MODEL), ln_b.reshape(1, D_MODEL),
      gm_ws, mmask, jnp.transpose(gm_bs), cos2, sin2, dmask, qd, kd, w_a, w_b, w_o,
      jq, jkt, jrv, jqd[:ntok], jstate)


def _mix_sample_front(h, nb, ntok, w_in, ln_g, ln_b, gm_ws, gm_bs, w_a):
    rows = MIX_ROWS
    nseq = rows // ntok
    nsub = rows // CHUNK
    nrow, nchunk = nb * ntok, nb * ntok // CHUNK
    rep = CHUNK // ntok
    dmask, _, kd, _, mmask = _retention_tables(ntok)
    cos2, sin2 = _rotary_tables(PAST_LEN + np.arange(ntok), rows)
    onehot = jnp.asarray(np.tile(np.eye(ntok, dtype=np.float32), (rep, 1)))
    wmix = jnp.einsum("ra,gab,cb->grc", onehot, gm_ws[:, :ntok, :ntok], onehot, precision=lax.Precision.HIGHEST)
    bst = jnp.einsum("ra,ga->rg", onehot, gm_bs[:, :ntok], precision=lax.Precision.HIGHEST)
    row_spec = lambda width: pl.BlockSpec((rows, width), lambda i: (i, 0))
    return pl.pallas_call(
        _mix_front_kernel,
        grid=(nb // nseq,),
        in_specs=([row_spec(D_MODEL)] + _mix_weight_specs()
                  + [_const_spec((rows, DK)), _const_spec((rows, DK)),
                     _const_spec((HEADS, CHUNK, CHUNK)), _const_spec((CHUNK, HEADS)),
                     _const_spec((D_MODEL, D_MODEL))]),
        out_specs=[pl.BlockSpec((1, nseq, ntok, D_MODEL), lambda i: (0, i, 0, 0)),
                   row_spec(D_MODEL), row_spec(D_MODEL),
                   pl.BlockSpec((nsub, D_MODEL, CHUNK), lambda i: (i, 0, 0)),
                   pl.BlockSpec((nsub, CHUNK, RET_V), lambda i: (i, 0, 0)),
                   row_spec(RET_V)],
        out_shape=[jax.ShapeDtypeStruct((1, nb, ntok, D_MODEL), F32),
                   jax.ShapeDtypeStruct((nrow, D_MODEL), F32),
                   jax.ShapeDtypeStruct((nrow, D_MODEL), F32),
                   jax.ShapeDtypeStruct((nchunk, D_MODEL, CHUNK), F32),
                   jax.ShapeDtypeStruct((nchunk, CHUNK, RET_V), BF16),
                   jax.ShapeDtypeStruct((nrow, RET_V), F32)],
        scratch_shapes=[pltpu.VMEM((rows, D_MODEL), BF16)],
        compiler_params=pltpu.CompilerParams(
            dimension_semantics=("arbitrary",), vmem_limit_bytes=VMEM_LIMIT_V7X),
        name="mix_sample_front",
    )(h, w_in, ln_g.reshape(1, D_MODEL), ln_b.reshape(1, D_MODEL),
      wmix, mmask, bst, cos2, sin2, dmask, kd, w_a)


def _mix_sample_back(x, mods, mod_row0, h, merged, ob, inter, w_in, w_b, w_o):
    nb, ntok, _ = x.shape
    rows = MIX_ROWS
    nseq = rows // ntok
    xspec = pl.BlockSpec((nseq, ntok, D_MODEL), lambda i: (i, 0, 0))
    row_spec = lambda width: pl.BlockSpec((rows, width), lambda i: (i, 0))
    col_spec = lambda lo, hi: pl.BlockSpec((D_MODEL, hi - lo), lambda i: (0, lo // (hi - lo)),
                                           pipeline_mode=pl.Buffered(1))
    assert OFF_RG % (OFF_GA - OFF_RG) == 0 and OFF_GB % (OFF_END - OFF_GB) == 0
    return pl.pallas_call(
        _mix_back_kernel,
        grid=(nb // nseq,),
        in_specs=[xspec, _mod_spec(nseq, mod_row0, MIX_GATE_CHUNK), row_spec(D_MODEL), row_spec(D_MODEL),
                  row_spec(RET_V), row_spec(RET_V),
                  col_spec(OFF_RG, OFF_GA), col_spec(OFF_GB, OFF_END),
                  _const_spec((RET_V, D_MODEL)), _const_spec((D_MODEL, D_MODEL))],
        out_specs=xspec,
        out_shape=jax.ShapeDtypeStruct(x.shape, F32),
        scratch_shapes=[pltpu.VMEM((rows, RET_V), F32)],
        compiler_params=pltpu.CompilerParams(
            dimension_semantics=("arbitrary",), vmem_limit_bytes=VMEM_LIMIT_V7X),
        name="mix_sample_back",
    )(x, mods, h, merged, ob, inter, w_in, w_in, w_b, w_o)


def kernel(x_prompt, x_sample, state_ret, c_prompt, c_sample, w_ada, b_ada, n1_g, w1_gate, w1_up, w1_down,
           nm_g, w_in, gm_ln_g, gm_ln_b, gm_ws, gm_bs, w_a, w_b, w_o, n2_g, w2_gate, w2_up, w2_down, final_g):
    assert w_ada.shape[0] == 1, "single-layer step"
    nbs = x_sample.shape[0]

    mods, w1 = _ada(jnp.concatenate([c_sample, c_prompt], axis=0), w_ada[0], b_ada[0],
                    cast=(w1_gate[0], w1_up[0], w1_down[0]))
    row_s, row_p = 0, nbs

    ((yp, hp), (ys, hs)), later = _ffn(
        [(x_prompt, row_p), (x_sample, row_s)], mods, FFN1_MOD_CHUNK, n1_g[0], *w1, nm_g[0], emit_next=True,
        cast=(w_in[0], w_a[0], w_b[0], w_o[0], w2_gate[0], w2_up[0], w2_down[0]))
    w_in_b, w_a_b, w_b_b, w_o_b = later[:4]
    w2 = later[4:]
    gmlp = (w_in_b, gm_ln_g[0], gm_ln_b[0], gm_ws[0], gm_bs[0], w_a_b)

    nbs, ntok, _ = x_sample.shape
    vs, merged_s, q_s, kt_s, rv_s, ob_s = _mix_sample_front(hs, nbs, ntok, *gmlp)
    yp, sp, inter_s, ss = _mix_prompt(yp, hp, mods, row_p, *gmlp, w_b_b, w_o_b,
                                      job=(q_s, kt_s, rv_s, state_ret.astype(F32), ntok))
    ys = _mix_sample_back(ys, mods, row_s, hs, merged_s, ob_s, inter_s, w_in_b, w_b_b, w_o_b)

    (yp, ys), _ = _ffn([(yp, row_p), (ys, row_s)], mods, FFN2_MOD_CHUNK, n2_g[0], *w2, final_g, emit_next=False)
    return (yp, ys, sp, ss, vs)
```

```python
import functools

import jax
import jax.numpy as jnp
import numpy as np
from jax import lax
from jax.experimental import pallas as pl
from jax.experimental.pallas import tpu as pltpu

F32 = jnp.float32
BF16 = jnp.bfloat16

D_MODEL = 1024
D_FF = 2816
N_MOD = 9
EPS = 1e-6
ROPE_BASE = 10000.0
PAST_LEN = 16384
CHUNK = 128
GROUPS = 8
GROUP_DIM = D_MODEL // GROUPS
HEADS = 8
DK = D_MODEL // HEADS
DV = 2 * DK
RET_V = HEADS * DV
OFF_U, OFF_V, OFF_Q, OFF_K, OFF_RV, OFF_RG, OFF_GA, OFF_GB, OFF_END = (
    0, 1024, 2048, 3072, 4096, 6144, 8192, 9216, 10240)

VMEM_LIMIT_V7X = 56 * 1024 * 1024
FFN_ROWS = 512
FFN_SPLIT = 2
MIX_ROWS = 256
ADA_STEPS = 8
BF16_SUBLANES = 16
F32_SUBLANES = 8
FFN1_MOD_CHUNK, MIX_GATE_CHUNK, FFN2_MOD_CHUNK = 0, 5, 6


def _dot(a, b):
    return jnp.dot(a, b, preferred_element_type=F32)


def _silu(x):
    return x * jax.nn.sigmoid(x)


def _gelu_tanh(x):
    c = np.float32(np.sqrt(2.0 / np.pi))
    return 0.5 * x * (1.0 + jnp.tanh(c * (x + 0.044715 * (x * x * x))))


def _rms_norm(x, gain):
    return x * lax.rsqrt(jnp.mean(x * x, axis=-1, keepdims=True) + EPS) * gain


def _const_spec(shape):
    nd = len(shape)
    return pl.BlockSpec(shape, lambda *_: (0,) * nd, pipeline_mode=pl.Buffered(1))


def _mod_spec(bb, row0, chunk):
    assert row0 % bb == 0
    return pl.BlockSpec((bb, 1, D_MODEL), lambda i, *_: (row0 // bb + i, 0, chunk))


def _cast_plan(weights, nsteps):
    in_specs, out_specs, out_shapes = [], [], []
    for w in weights:
        rows, cols = w.shape
        nblk = max(n for n in range(1, nsteps + 1) if rows % n == 0 and (rows // n) % BF16_SUBLANES == 0)
        spec = pl.BlockSpec((rows // nblk, cols), lambda step, nblk=nblk: (step * nblk // nsteps, 0))
        in_specs.append(spec)
        out_specs.append(spec)
        out_shapes.append(jax.ShapeDtypeStruct(w.shape, BF16))
    return in_specs, out_specs, out_shapes


def _cast_blocks(src_refs, dst_refs):
    for src, dst in zip(src_refs, dst_refs, strict=True):
        dst[...] = src[...].astype(BF16)


def _ada_kernel(c_ref, w_ref, b_ref, *refs):
    ncast = (len(refs) - 1) // 2
    o_ref = refs[ncast]
    s = _silu(c_ref[...]).astype(BF16)
    m = _dot(s, w_ref[...].astype(BF16)) + b_ref[...]
    for r in range(m.shape[0]):
        o_ref[r] = m[r:r + 1, :]
    _cast_blocks(refs[:ncast], refs[ncast + 1:])


def _ada(c_all, w_ada, b_ada, cast):
    rows = c_all.shape[0]
    n = w_ada.shape[1]
    grid = (ADA_STEPS,)
    bn = n // ADA_STEPS
    cast_in, cast_out, cast_shapes = _cast_plan(cast, ADA_STEPS)
    out = pl.pallas_call(
        _ada_kernel,
        grid=grid,
        in_specs=[pl.BlockSpec((rows, D_MODEL), lambda j: (0, 0)),
                  pl.BlockSpec((D_MODEL, bn), lambda j: (0, j)),
                  pl.BlockSpec((1, bn), lambda j: (0, j))] + cast_in,
        out_specs=[pl.BlockSpec((rows, 1, bn), lambda j: (0, 0, j))] + cast_out,
        out_shape=[jax.ShapeDtypeStruct((rows, 1, n), F32)] + cast_shapes,
        compiler_params=pltpu.CompilerParams(
            dimension_semantics=("arbitrary",), vmem_limit_bytes=VMEM_LIMIT_V7X,
            allow_input_fusion=[True] + [False] * (2 + len(cast))),
        name="ada",
    )(c_all, w_ada, b_ada.reshape(1, n), *cast)
    return out[0], out[1:]


def _ffn_rows(x_ref, m_ref, ng_ref, wg_ref, wu_ref, wd_ref, eg_ref, o_ref, h_ref):
    nb, nr, _ = x_ref.shape
    chunk = lambda seqs, k: m_ref[seqs, :, k * D_MODEL:(k + 1) * D_MODEL]
    if nb > 1:
        groups = [(slice(s * nb // FFN_SPLIT, (s + 1) * nb // FFN_SPLIT), slice(None)) for s in range(FFN_SPLIT)]
    else:
        groups = [(slice(None), slice(s * nr // FFN_SPLIT, (s + 1) * nr // FFN_SPLIT)) for s in range(FFN_SPLIT)]
    xs = [x_ref[seqs, toks, :] for seqs, toks in groups]
    bb, r, _ = xs[0].shape
    hs = [(_rms_norm(x, ng_ref[...]) * (1.0 + chunk(seqs, 1)) + chunk(seqs, 0)).reshape(bb * r, D_MODEL).astype(BF16)
          for x, (seqs, _) in zip(xs, groups)]
    gates = [_dot(h, wg_ref[...]) for h in hs]
    ups = [_dot(h, wu_ref[...]) for h in hs]
    acts = [(_silu(gt) * up).astype(BF16) for gt, up in zip(gates, ups)]
    ys = [_dot(act, wd_ref[...]).reshape(bb, r, D_MODEL) for act in acts]
    for s, (x, y, (seqs, toks)) in enumerate(zip(xs, ys, groups)):
        out = x + (0.5 * chunk(seqs, 2)) * y
        if h_ref is None:
            o_ref[seqs, toks, :] = _rms_norm(out, eg_ref[...])
        else:
            o_ref[seqs, toks, :] = out
            hn = _rms_norm(out, eg_ref[...]) * (1.0 + chunk(seqs, 4)) + chunk(seqs, 3)
            h_ref[s * bb * r:(s + 1) * bb * r, :] = hn.reshape(bb * r, D_MODEL).astype(BF16)


N_FFN_WEIGHT_IN = 5


def _ffn_kernel(*refs, emit_next, batch_steps):
    nbatch = len(batch_steps)
    per_in = 2
    per_out = 2 if emit_next else 1
    n_in = nbatch * per_in + N_FFN_WEIGHT_IN
    ncast = (len(refs) - n_in - nbatch * per_out) // 2
    weights = refs[nbatch * per_in:n_in]
    outs = refs[n_in + ncast:]
    step = pl.program_id(0)
    first = 0
    for b, nsteps in enumerate(batch_steps):
        ins = refs[b * per_in:(b + 1) * per_in]
        bouts = outs[b * per_out:(b + 1) * per_out]

        @pl.when((step >= first) & (step < first + nsteps))
        def _(ins=ins, bouts=bouts):
            _ffn_rows(*ins, *weights, bouts[0], bouts[1] if emit_next else None)

        first += nsteps
    _cast_blocks(refs[n_in:n_in + ncast], outs[nbatch * per_out:])


def _ffn(batches, mods, mod_first, norm_g, wg, wu, wd, out_gain, *, emit_next, cast=()):
    operands, in_specs, out_specs, out_shapes, batch_steps = [], [], [], [], []
    first = 0
    for x, mod_row0 in batches:
        nb, r, _ = x.shape
        br = min(r, FFN_ROWS)
        bb = FFN_ROWS // br
        nj = r // br
        nsteps = (nb // bb) * nj
        assert mod_row0 % bb == 0
        local = lambda step, first=first, nsteps=nsteps: jnp.clip(step - first, 0, nsteps - 1)
        xspec = pl.BlockSpec((bb, br, D_MODEL), lambda step, local=local, nj=nj: (local(step) // nj, local(step) % nj, 0))
        nmod = 5 if emit_next else 3
        assert mod_first % nmod == 0
        mspec = pl.BlockSpec((bb, 1, nmod * D_MODEL),
                             lambda step, local=local, nj=nj, row=mod_row0 // bb: (row + local(step) // nj, 0, mod_first // nmod))
        operands += [x, mods]
        in_specs += [xspec, mspec]
        out_specs.append(xspec)
        out_shapes.append(jax.ShapeDtypeStruct(x.shape, F32))
        if emit_next:
            out_specs.append(pl.BlockSpec((bb * br, D_MODEL), lambda step, local=local: (local(step), 0)))
            out_shapes.append(jax.ShapeDtypeStruct((nb * r, D_MODEL), BF16))
        batch_steps.append(nsteps)
        first += nsteps
    cast_in, cast_out, cast_shapes = _cast_plan(cast, first)
    out = pl.pallas_call(
        functools.partial(_ffn_kernel, emit_next=emit_next, batch_steps=tuple(batch_steps)),
        grid=(first,),
        in_specs=in_specs + [
            _const_spec((1, D_MODEL)),
            _const_spec((D_MODEL, D_FF)), _const_spec((D_MODEL, D_FF)), _const_spec((D_FF, D_MODEL)),
            _const_spec((1, D_MODEL))] + cast_in,
        out_specs=out_specs + cast_out,
        out_shape=out_shapes + cast_shapes,
        compiler_params=pltpu.CompilerParams(
            dimension_semantics=("arbitrary",), vmem_limit_bytes=VMEM_LIMIT_V7X),
        name="ffn" if emit_next else "ffn_final",
    )(*operands, norm_g.reshape(1, D_MODEL), wg, wu, wd, out_gain.reshape(1, D_MODEL), *cast)
    nout = len(batches) * (2 if emit_next else 1)
    ys = [tuple(out[2 * b:2 * b + 2]) for b in range(len(batches))] if emit_next else list(out[:nout])
    return ys, out[nout:]


def _layer_norm(x, gain, bias):
    mu = jnp.mean(x, axis=-1, keepdims=True)
    var = jnp.mean(jnp.square(x - mu), axis=-1, keepdims=True)
    return (x - mu) * lax.rsqrt(var + EPS) * gain + bias


def _spatial_mix(u, vb, wmix_ref, mmask_ref, bst_ref, oa_scr):
    mmask = mmask_ref[...]
    for g in range(GROUPS):
        wg = (wmix_ref[g] * mmask).astype(BF16)
        cols = slice(g * GROUP_DIM, (g + 1) * GROUP_DIM)
        for c in range(u.shape[0] // CHUNK):
            rws = slice(c * CHUNK, (c + 1) * CHUNK)
            mix = _dot(wg, vb[rws, cols]) + bst_ref[:, g:g + 1]
            oa_scr[rws, cols] = (u[rws, cols] * mix).astype(BF16)


def _rotary_heads(z, cos2, sin2):
    out = []
    for hd in range(HEADS):
        zh = z[:, hd * DK:(hd + 1) * DK]
        out.append(zh * cos2 + pltpu.roll(zh, DK // 2, 1) * sin2)
    return out


def _head_rms(o):
    return o * lax.rsqrt(jnp.mean(o * o, axis=-1, keepdims=True) + EPS)


def _state_step(q_ref, kt_ref, rv_ref, qd_ref, s0_ref, inter_ref, snew_ref, first_seq, s_decay):
    nstep = s0_ref.shape[1]
    ntok = qd_ref.shape[0]
    seq_per_chunk = CHUNK // ntok
    for t in range(nstep):
        tok = slice(t * ntok, (t + 1) * ntok)
        in_seq = lax.broadcasted_iota(jnp.int32, (DK, CHUNK), 1) // ntok == (first_seq + t) % seq_per_chunk
        for hd in range(HEADS):
            state = s0_ref[0, t, hd]
            qj = q_ref[tok, hd * DK:(hd + 1) * DK].astype(BF16)
            inter_ref[tok, hd * DV:(hd + 1) * DV] = _dot(qj, state.astype(BF16)) * qd_ref[:, hd:hd + 1]
            kt = jnp.where(in_seq, kt_ref[0, hd * DK:(hd + 1) * DK, :], 0.0).astype(BF16)
            upd = _dot(kt, rv_ref[0, :, hd * DV:(hd + 1) * DV])
            snew_ref[0, t, hd] = s_decay[hd] * state + upd


N_MIX_IN = 17
N_JOB_IN = 5


def _mix_prompt_kernel(*refs, s_decay, job_decay):
    (x_ref, h_ref, g_ref, win_ref, lng_ref, lnb_ref, wmix_ref, mmask_ref,
     bst_ref, cos_ref, sin_ref, dmask_ref, qd_ref, kd_ref, wa_ref, wb_ref, wo_ref) = refs[:N_MIX_IN]
    rest = refs[N_MIX_IN:]
    if job_decay is not None:
        jq_ref, jkt_ref, jrv_ref, jqd_ref, js0_ref = rest[:N_JOB_IN]
        o_ref, s_ref, jinter_ref, jsnew_ref, oa_scr, ob_scr = rest[N_JOB_IN:]
    else:
        o_ref, s_ref, oa_scr, ob_scr = rest

    @pl.when(pl.program_id(1) == 0)
    def _():
        s_ref[...] = jnp.zeros(s_ref.shape, F32)

    x = x_ref[0]
    tm = x.shape[0]
    h = h_ref[...]
    proj = lambda lo, hi: _dot(h, win_ref[:, lo:hi])
    cos2, sin2 = cos_ref[...], sin_ref[...]
    zv = proj(OFF_V, OFF_Q)
    zu = proj(OFF_U, OFF_V)
    gv = _gelu_tanh(zv)
    zq = proj(OFF_Q, OFF_K)
    v = _layer_norm(gv, lng_ref[...], lnb_ref[...])
    u = _gelu_tanh(zu)
    zk = proj(OFF_K, OFF_RV)
    qs = _rotary_heads(zq, cos2, sin2)
    rv = proj(OFF_RV, OFF_RG).astype(BF16)
    ks = _rotary_heads(zk, cos2, sin2)
    kscale = np.float32(DK ** -0.5)
    heads = range(HEADS)
    qb = [qs[hd].astype(BF16) for hd in heads]
    kf = [ks[hd] * kscale for hd in heads]
    kb = [kf[hd].astype(BF16) for hd in heads]
    nt = (((1,), (1,)), ((), ()))
    for c in range(tm // CHUNK):
        rws = slice(c * CHUNK, (c + 1) * CHUNK)
        vc = [rv[rws, hd * DV:(hd + 1) * DV] for hd in heads]
        state = [s_ref[0, 0, hd] for hd in heads]
        scores = [lax.dot_general(qb[hd][rws], kb[hd][rws], nt, preferred_element_type=F32) for hd in heads]
        inter = [_dot(qb[hd][rws], state[hd].astype(BF16)) for hd in heads]
        kdt = [jnp.transpose(kf[hd][rws] * kd_ref[:, hd:hd + 1]).astype(BF16) for hd in heads]
        upd = [_dot(kdt[hd], vc[hd]) for hd in heads]
        intra = [_dot((scores[hd] * dmask_ref[hd]).astype(BF16), vc[hd]) for hd in heads]
        for hd in heads:
            s_ref[0, 0, hd] = s_decay[hd] * state[hd] + upd[hd]
            ob_scr[rws, hd * DV:(hd + 1) * DV] = _head_rms(intra[hd] + inter[hd] * qd_ref[:, hd:hd + 1])
        if c == 0:
            gate_r = _silu(proj(OFF_RG, OFF_GA))
            gate_a = jax.nn.sigmoid(proj(OFF_GA, OFF_GB))
            _spatial_mix(u, v.astype(BF16), wmix_ref, mmask_ref, bst_ref, oa_scr)
    gate_b = jax.nn.sigmoid(proj(OFF_GB, OFF_END))
    pa = _dot(oa_scr[...], wa_ref[...])
    merged = gate_a * pa + gate_b * _dot((ob_scr[...] * gate_r).astype(BF16), wb_ref[...])
    o_ref[0] = x + g_ref[0] * _dot(merged.astype(BF16), wo_ref[...])
    if job_decay is not None:
        step = pl.program_id(0) * pl.num_programs(1) + pl.program_id(1)
        _state_step(jq_ref, jkt_ref, jrv_ref, jqd_ref, js0_ref, jinter_ref, jsnew_ref,
                    step * js0_ref.shape[1], job_decay)


def _mix_front_kernel(h_ref, win_ref, lng_ref, lnb_ref, wmix_ref, mmask_ref, bst_ref,
                      cos_ref, sin_ref, dmask_ref, kd_ref, wa_ref,
                      vrow_ref, mg_ref, q_ref, kt_ref, rv_ref, ob_ref, oa_scr):
    _, nseq, ntok, _ = vrow_ref.shape
    rows = nseq * ntok
    h = h_ref[...]
    proj = lambda lo, hi: _dot(h, win_ref[:, lo:hi])
    cos2, sin2 = cos_ref[...], sin_ref[...]
    zv, zu = proj(OFF_V, OFF_Q), proj(OFF_U, OFF_V)
    zq, zk = proj(OFF_Q, OFF_K), proj(OFF_K, OFF_RV)
    rv = proj(OFF_RV, OFF_RG).astype(BF16)
    gate_a = jax.nn.sigmoid(proj(OFF_GA, OFF_GB))
    v = _layer_norm(_gelu_tanh(zv), lng_ref[...], lnb_ref[...])
    vrow_ref[0] = v.reshape(nseq, ntok, D_MODEL)
    u = _gelu_tanh(zu)
    qs = _rotary_heads(zq, cos2, sin2)
    ks = _rotary_heads(zk, cos2, sin2)
    kscale = np.float32(DK ** -0.5)
    heads = range(HEADS)
    kf = [ks[hd] * kscale for hd in heads]
    nt = (((1,), (1,)), ((), ()))
    for c in range(rows // CHUNK):
        rws = slice(c * CHUNK, (c + 1) * CHUNK)
        rv_ref[c] = rv[rws]
        scores = [lax.dot_general(qs[hd][rws].astype(BF16), kf[hd][rws].astype(BF16), nt,
                                  preferred_element_type=F32) for hd in heads]
        for hd in heads:
            kt_ref[c, hd * DK:(hd + 1) * DK, :] = jnp.transpose(kf[hd][rws] * kd_ref[:, hd:hd + 1])
        for hd in heads:
            ob_ref[rws, hd * DV:(hd + 1) * DV] = _dot(
                (scores[hd] * dmask_ref[hd]).astype(BF16), rv[rws, hd * DV:(hd + 1) * DV])
    for hd in heads:
        q_ref[:, hd * DK:(hd + 1) * DK] = qs[hd]
    _spatial_mix(u, v.astype(BF16), wmix_ref, mmask_ref, bst_ref, oa_scr)
    mg_ref[...] = gate_a * _dot(oa_scr[...], wa_ref[...])


def _mix_back_kernel(x_ref, g_ref, h_ref, mg_ref, ob_ref, inter_ref, wrg_ref, wgb_ref, wb_ref, wo_ref,
                     o_ref, on_scr):
    nseq, ntok, _ = x_ref.shape
    h = h_ref[...]
    for hd in range(HEADS):
        cols = slice(hd * DV, (hd + 1) * DV)
        on_scr[:, cols] = _head_rms(ob_ref[:, cols] + inter_ref[:, cols])
    ob = (on_scr[...] * _silu(_dot(h, wrg_ref[...]))).astype(BF16)
    merged = mg_ref[...] + jax.nn.sigmoid(_dot(h, wgb_ref[...])) * _dot(ob, wb_ref[...])
    out = _dot(merged.astype(BF16), wo_ref[...])
    o_ref[...] = x_ref[...] + g_ref[...] * out.reshape(nseq, ntok, D_MODEL)


def _retention_tables(seq_len):
    lc = min(seq_len, CHUNK)
    log_gamma = np.log1p(-np.power(2.0, -5.0 - np.arange(HEADS)))
    idx = np.arange(lc, dtype=np.float64)
    diff = idx[:, None] - idx[None, :]
    decay = np.where(diff >= 0, np.exp(np.maximum(diff, 0.0)[None] * log_gamma[:, None, None]), 0.0)
    q_decay = np.exp((idx + 1.0)[:, None] * log_gamma[None, :])
    k_decay = np.exp((lc - 1.0 - idx)[:, None] * log_gamma[None, :])
    s_decay = tuple(float(v) for v in np.exp(lc * log_gamma).astype(np.float32))
    rep = CHUNK // lc
    blockdiag = np.kron(np.eye(rep), np.ones((lc, lc)))
    dmask = np.tile(decay, (1, rep, rep)) * blockdiag[None]
    mmask = np.tile(np.tril(np.ones((lc, lc))), (rep, rep)) * blockdiag
    f32 = lambda a: jnp.asarray(a.astype(np.float32))
    return f32(dmask), f32(np.tile(q_decay, (rep, 1))), f32(np.tile(k_decay, (rep, 1))), s_decay, f32(mmask)


def _rotary_tables(pos, rows):
    half = DK // 2
    inv = ROPE_BASE ** (-np.arange(half, dtype=np.float64) / half)
    ang = np.asarray(pos, np.float64)[:, None] * inv[None, :]
    cos, sin = np.cos(ang), np.sin(ang)
    rep = rows // len(pos)
    cos2 = np.tile(np.concatenate([cos, cos], axis=-1), (rep, 1))
    sin2 = np.tile(np.concatenate([-sin, sin], axis=-1), (rep, 1))
    return jnp.asarray(cos2.astype(np.float32)), jnp.asarray(sin2.astype(np.float32))


def _mix_weight_specs():
    return [_const_spec((D_MODEL, OFF_END)),
            _const_spec((1, D_MODEL)), _const_spec((1, D_MODEL)),
            _const_spec((GROUPS, CHUNK, CHUNK)), _const_spec((CHUNK, CHUNK)),
            _const_spec((CHUNK, GROUPS))]


def _mix_out_weight_specs():
    return [_const_spec((HEADS, CHUNK, CHUNK)),
            _const_spec((CHUNK, HEADS)), _const_spec((CHUNK, HEADS)),
            _const_spec((D_MODEL, D_MODEL)), _const_spec((RET_V, D_MODEL)), _const_spec((D_MODEL, D_MODEL))]


def _mix_prompt(x, h, mods, mod_row0, w_in, ln_g, ln_b, gm_ws, gm_bs, w_a, w_b, w_o, job):
    nb, seq, _ = x.shape
    tm = MIX_ROWS
    nj = seq // tm
    dmask, qd, kd, s_decay, mmask = _retention_tables(seq)
    cos2, sin2 = _rotary_tables(np.arange(seq), seq)
    xspec = pl.BlockSpec((1, tm, D_MODEL), lambda b, j: (b, j, 0))
    hspec = pl.BlockSpec((tm, D_MODEL), lambda b, j: (b * nj + j, 0))
    tspec = pl.BlockSpec((tm, DK), lambda b, j: (j, 0))
    state_shape = (1, nb, HEADS, DK, DV)

    jq, jkt, jrv, jstate, ntok = job
    nseq_total = jstate.shape[1]
    nstep = nseq_total // (nb * nj)
    seq_per_chunk = CHUNK // ntok
    assert nstep * nb * nj == nseq_total and seq_per_chunk % nstep == 0 and (nstep * ntok) % F32_SUBLANES == 0
    _, jqd, _, job_decay, _ = _retention_tables(ntok)
    step = lambda b, j: b * nj + j
    chunk = lambda b, j: step(b, j) * nstep // seq_per_chunk
    jsspec = pl.BlockSpec((1, nstep, HEADS, DK, DV), lambda b, j: (0, step(b, j), 0, 0, 0))
    jispec = pl.BlockSpec((nstep * ntok, RET_V), lambda b, j: (step(b, j), 0))
    job_in = [pl.BlockSpec((nstep * ntok, D_MODEL), lambda b, j: (step(b, j), 0)),
              pl.BlockSpec((1, D_MODEL, CHUNK), lambda b, j: (chunk(b, j), 0, 0)),
              pl.BlockSpec((1, CHUNK, RET_V), lambda b, j: (chunk(b, j), 0, 0)),
              _const_spec((ntok, HEADS)), jsspec]

    return pl.pallas_call(
        functools.partial(_mix_prompt_kernel, s_decay=s_decay, job_decay=job_decay),
        grid=(nb, nj),
        in_specs=([xspec, hspec, _mod_spec(1, mod_row0, MIX_GATE_CHUNK)] + _mix_weight_specs() + [tspec, tspec]
                  + _mix_out_weight_specs() + job_in),
        out_specs=[xspec, pl.BlockSpec((1, 1, HEADS, DK, DV), lambda b, j: (0, b, 0, 0, 0)), jispec, jsspec],
        out_shape=[jax.ShapeDtypeStruct(x.shape, F32), jax.ShapeDtypeStruct(state_shape, F32),
                   jax.ShapeDtypeStruct((nseq_total * ntok, RET_V), F32), jax.ShapeDtypeStruct(jstate.shape, F32)],
        scratch_shapes=[pltpu.VMEM((tm, D_MODEL), BF16), pltpu.VMEM((tm, RET_V), F32)],
        compiler_params=pltpu.CompilerParams(
            dimension_semantics=("arbitrary", "arbitrary"), vmem_limit_bytes=VMEM_LIMIT_V7X),
        name="mix_prompt",
    )(x, h, mods, w_in, ln_g.reshape(1, D_MODEL), ln_b.reshape(1, D_MODEL),
      gm_ws, mmask, jnp.transpose(gm_bs), cos2, sin2, dmask, qd, kd, w_a, w_b, w_o,
      jq, jkt, jrv, jqd[:ntok], jstate)


def _mix_sample_front(h, nb, ntok, w_in, ln_g, ln_b, gm_ws, gm_bs, w_a):
    rows = MIX_ROWS
    nseq = rows // ntok
    nsub = rows // CHUNK
    nrow, nchunk = nb * ntok, nb * ntok // CHUNK
    rep = CHUNK // ntok
    dmask, _, kd, _, mmask = _retention_tables(ntok)
    cos2, sin2 = _rotary_tables(PAST_LEN + np.arange(ntok), rows)
    onehot = jnp.asarray(np.tile(np.eye(ntok, dtype=np.float32), (rep, 1)))
    wmix = jnp.einsum("ra,gab,cb->grc", onehot, gm_ws[:, :ntok, :ntok], onehot, precision=lax.Precision.HIGHEST)
    bst = jnp.einsum("ra,ga->rg", onehot, gm_bs[:, :ntok], precision=lax.Precision.HIGHEST)
    row_spec = lambda width: pl.BlockSpec((rows, width), lambda i: (i, 0))
    return pl.pallas_call(
        _mix_front_kernel,
        grid=(nb // nseq,),
        in_specs=([row_spec(D_MODEL)] + _mix_weight_specs()
                  + [_const_spec((rows, DK)), _const_spec((rows, DK)),
                     _const_spec((HEADS, CHUNK, CHUNK)), _const_spec((CHUNK, HEADS)),
                     _const_spec((D_MODEL, D_MODEL))]),
        out_specs=[pl.BlockSpec((1, nseq, ntok, D_MODEL), lambda i: (0, i, 0, 0)),
                   row_spec(D_MODEL), row_spec(D_MODEL),
                   pl.BlockSpec((nsub, D_MODEL, CHUNK), lambda i: (i, 0, 0)),
                   pl.BlockSpec((nsub, CHUNK, RET_V), lambda i: (i, 0, 0)),
                   row_spec(RET_V)],
        out_shape=[jax.ShapeDtypeStruct((1, nb, ntok, D_MODEL), F32),
                   jax.ShapeDtypeStruct((nrow, D_MODEL), F32),
                   jax.ShapeDtypeStruct((nrow, D_MODEL), F32),
                   jax.ShapeDtypeStruct((nchunk, D_MODEL, CHUNK), F32),
                   jax.ShapeDtypeStruct((nchunk, CHUNK, RET_V), BF16),
                   jax.ShapeDtypeStruct((nrow, RET_V), F32)],
        scratch_shapes=[pltpu.VMEM((rows, D_MODEL), BF16)],
        compiler_params=pltpu.CompilerParams(
            dimension_semantics=("arbitrary",), vmem_limit_bytes=VMEM_LIMIT_V7X),
        name="mix_sample_front",
    )(h, w_in, ln_g.reshape(1, D_MODEL), ln_b.reshape(1, D_MODEL),
      wmix, mmask, bst, cos2, sin2, dmask, kd, w_a)


def _mix_sample_back(x, mods, mod_row0, h, merged, ob, inter, w_in, w_b, w_o):
    nb, ntok, _ = x.shape
    rows = MIX_ROWS
    nseq = rows // ntok
    xspec = pl.BlockSpec((nseq, ntok, D_MODEL), lambda i: (i, 0, 0))
    row_spec = lambda width: pl.BlockSpec((rows, width), lambda i: (i, 0))
    col_spec = lambda lo, hi: pl.BlockSpec((D_MODEL, hi - lo), lambda i: (0, lo // (hi - lo)),
                                           pipeline_mode=pl.Buffered(1))
    assert OFF_RG % (OFF_GA - OFF_RG) == 0 and OFF_GB % (OFF_END - OFF_GB) == 0
    return pl.pallas_call(
        _mix_back_kernel,
        grid=(nb // nseq,),
        in_specs=[xspec, _mod_spec(nseq, mod_row0, MIX_GATE_CHUNK), row_spec(D_MODEL), row_spec(D_MODEL),
                  row_spec(RET_V), row_spec(RET_V),
                  col_spec(OFF_RG, OFF_GA), col_spec(OFF_GB, OFF_END),
                  _const_spec((RET_V, D_MODEL)), _const_spec((D_MODEL, D_MODEL))],
        out_specs=xspec,
        out_shape=jax.ShapeDtypeStruct(x.shape, F32),
        scratch_shapes=[pltpu.VMEM((rows, RET_V), F32)],
        compiler_params=pltpu.CompilerParams(
            dimension_semantics=("arbitrary",), vmem_limit_bytes=VMEM_LIMIT_V7X),
        name="mix_sample_back",
    )(x, mods, h, merged, ob, inter, w_in, w_in, w_b, w_o)


def kernel(x_prompt, x_sample, state_ret, c_prompt, c_sample, w_ada, b_ada, n1_g, w1_gate, w1_up, w1_down,
           nm_g, w_in, gm_ln_g, gm_ln_b, gm_ws, gm_bs, w_a, w_b, w_o, n2_g, w2_gate, w2_up, w2_down, final_g):
    assert w_ada.shape[0] == 1, "single-layer step"
    nbs = x_sample.shape[0]

    mods, w1 = _ada(jnp.concatenate([c_sample, c_prompt], axis=0), w_ada[0], b_ada[0],
                    cast=(w1_gate[0], w1_up[0], w1_down[0]))
    row_s, row_p = 0, nbs

    ((yp, hp), (ys, hs)), later = _ffn(
        [(x_prompt, row_p), (x_sample, row_s)], mods, FFN1_MOD_CHUNK, n1_g[0], *w1, nm_g[0], emit_next=True,
        cast=(w_in[0], w_a[0], w_b[0], w_o[0], w2_gate[0], w2_up[0], w2_down[0]))
    w_in_b, w_a_b, w_b_b, w_o_b = later[:4]
    w2 = later[4:]
    gmlp = (w_in_b, gm_ln_g[0], gm_ln_b[0], gm_ws[0], gm_bs[0], w_a_b)

    nbs, ntok, _ = x_sample.shape
    vs, merged_s, q_s, kt_s, rv_s, ob_s = _mix_sample_front(hs, nbs, ntok, *gmlp)
    yp, sp, inter_s, ss = _mix_prompt(yp, hp, mods, row_p, *gmlp, w_b_b, w_o_b,
                                      job=(q_s, kt_s, rv_s, state_ret.astype(F32), ntok))
    ys = _mix_sample_back(ys, mods, row_s, hs, merged_s, ob_s, inter_s, w_in_b, w_b_b, w_o_b)

    (yp, ys), _ = _ffn([(yp, row_p), (ys, row_s)], mods, FFN2_MOD_CHUNK, n2_g[0], *w2, final_g, emit_next=False)
    return (yp, ys, sp, ss, vs)
```
